```python
import jax, jax.numpy as jnp
from jax import lax
import numpy as np

D_MODEL = 4096
BATCH = 4
SEQ = 4096
DEPTH = 2
DEC_BATCH = 2
DEC_SEQ = 4096
PAST_LEN = 128

MEM_TOKENS = 256
MIX_WIDTH = D_MODEL
GLA_WIDTH = MIX_WIDTH // 2
MLSTM_WIDTH = MIX_WIDTH - GLA_WIDTH
GLA_HEADS = 4
GLA_DV = GLA_WIDTH // GLA_HEADS
GLA_DK = GLA_DV // 2
GLA_QK = GLA_HEADS * GLA_DK
GLA_RANK = 16
GLA_TAU = 16.0
GLA_MIN_LOG_DECAY = -1.0
MLSTM_HEADS = 4
MLSTM_DV = MLSTM_WIDTH // MLSTM_HEADS
MLSTM_DK = MLSTM_DV // 2
MLSTM_QK = MLSTM_HEADS * MLSTM_DK
CHUNK = 64
XATTN_HEADS = 4
XATTN_HEAD_DIM = 128
XATTN_WIDTH = XATTN_HEADS * XATTN_HEAD_DIM
D_FF = 11008
EPS = 1e-6
NEG_BIG = -1e30

IN_SIZES = (GLA_QK, GLA_QK, GLA_WIDTH, GLA_WIDTH, GLA_RANK, GLA_RANK,
            MLSTM_QK, MLSTM_QK, MLSTM_WIDTH, MLSTM_WIDTH, 4 * MLSTM_HEADS)
IN_WIDTH = sum(IN_SIZES)
IN_SPLITS = tuple(int(s) for s in np.cumsum(IN_SIZES)[:-1])

kernel_name = "hymba_gla_mlstm_macaron_encoder"


def rmsnorm(x, w):
    xf = x.astype(jnp.float32)
    y = xf * lax.rsqrt(jnp.mean(xf * xf, axis=-1, keepdims=True) + EPS)
    return (y * w.astype(jnp.float32)).astype(x.dtype)


def head_rmsnorm(x, w):
    H, d = x.shape[-2:]
    y = x * lax.rsqrt(jnp.mean(x * x, axis=-1, keepdims=True) + EPS)
    return y * w.reshape(H, d).astype(jnp.float32)


def swiglu(x, w_gate, w_up, w_down):
    return (jax.nn.silu(x @ w_gate) * (x @ w_up)) @ w_down


def to_chunks(t):
    B, L, H, d = t.shape
    return t.reshape(B, L // CHUNK, CHUNK, H, d).transpose(1, 0, 3, 2, 4)


def from_chunks(t):
    nc, B, H, C, d = t.shape
    return t.transpose(1, 0, 3, 2, 4).reshape(B, nc * C, H, d)


def gate_chunks(t):
    B, L, H = t.shape
    return t.reshape(B, L // CHUNK, CHUNK, H).transpose(1, 0, 3, 2)


def gla_scan(q, k, v, g):
    B, L, H, dk = q.shape
    dv = v.shape[-1]
    qc, kc, vc, gc = to_chunks(q), to_chunks(k), to_chunks(v), to_chunks(g)
    b = jnp.cumsum(gc, axis=3)
    b_last = b[:, :, :, -1:, :]
    q_dec = qc * jnp.exp(b)
    k_inv = kc * jnp.exp(-b)
    k_end = kc * jnp.exp(b_last - b)
    mask = jnp.tril(jnp.ones((CHUNK, CHUNK), dtype=bool))
    a = jnp.where(mask, jnp.einsum("nbhid,nbhjd->nbhij", q_dec, k_inv), 0.0)
    o_intra = jnp.einsum("nbhij,nbhjv->nbhiv", a, vc)

    def step(S, xs):
        qd, ke, vv, bl = xs
        o = jnp.einsum("bhid,bhdv->bhiv", qd, S)
        S = jnp.exp(bl[:, :, 0, :])[..., None] * S + jnp.einsum("bhid,bhiv->bhdv", ke, vv)
        return S, o

    S0 = jnp.zeros((B, H, dk, dv), jnp.float32)
    _, o_inter = lax.scan(step, S0, (q_dec, k_end, vc, b_last))
    return from_chunks(o_intra + o_inter)


def mlstm_scan(q, k, v, log_i, log_f):
    B, L, H, dk = q.shape
    dv = v.shape[-1]
    qc, kc, vc = to_chunks(q), to_chunks(k), to_chunks(v)
    ic, fc = gate_chunks(log_i), gate_chunks(log_f)
    b = jnp.cumsum(fc, axis=-1)
    g = b[..., -1]
    mask = jnp.tril(jnp.ones((CHUNK, CHUNK), dtype=bool))
    log_d = jnp.where(mask, b[..., :, None] - b[..., None, :] + ic[..., None, :], NEG_BIG)
    m_intra = jnp.max(log_d, axis=-1)
    s = jnp.einsum("nbhjd,nbhid->nbhji", qc, kc) * jnp.exp(log_d - m_intra[..., None])
    num_intra = jnp.einsum("nbhji,nbhiv->nbhjv", s, vc)
    den_intra = jnp.sum(s, axis=-1)
    a_end = g[..., None] - b + ic

    def step(carry, xs):
        Cs, ns, m = carry
        qq, kk, vv, bb, gg, ae, mi, nu, de = xs
        m_j = jnp.maximum(bb + m[..., None], mi)
        w_inter = jnp.exp(bb + m[..., None] - m_j)
        w_intra = jnp.exp(mi - m_j)
        num = w_inter[..., None] * jnp.einsum("bhjd,bhdv->bhjv", qq, Cs) + w_intra[..., None] * nu
        den = w_inter * jnp.einsum("bhjd,bhd->bhj", qq, ns) + w_intra * de
        h = num / jnp.maximum(jnp.abs(den), jnp.exp(-m_j))[..., None]
        m_new = jnp.maximum(gg + m, jnp.max(ae, axis=-1))
        scale_old = jnp.exp(gg + m - m_new)
        kw = kk * jnp.exp(ae - m_new[..., None])[..., None]
        Cs = scale_old[..., None, None] * Cs + jnp.einsum("bhid,bhiv->bhdv", kw, vv)
        ns = scale_old[..., None] * ns + jnp.sum(kw, axis=2)
        return (Cs, ns, m_new), h

    carry0 = (jnp.zeros((B, H, dk, dv), jnp.float32),
              jnp.zeros((B, H, dk), jnp.float32),
              jnp.full((B, H), NEG_BIG, jnp.float32))
    _, h = lax.scan(step, carry0, (qc, kc, vc, b, g, a_end, m_intra, num_intra, den_intra))
    return from_chunks(h)


def flip(t):
    return jnp.flip(t, axis=1)


def parallel_mixer(h, w_in, gla_w_lr, gla_b_lr, gla_out_norm, mlstm_gate_b, mlstm_out_norm, w_out):
    B, L, _ = h.shape
    proj = (h @ w_in).astype(jnp.float32)
    (gq, gk, gv, gg, glr_f, glr_b, mq, mk, mv, mo, mgates) = jnp.split(proj, IN_SPLITS, axis=-1)

    q = gq.reshape(B, L, GLA_HEADS, GLA_DK) * (GLA_DK ** -0.5)
    k = gk.reshape(B, L, GLA_HEADS, GLA_DK)
    v = gv.reshape(B, L, GLA_HEADS, GLA_DV)

    def gla_decay(lr, d):
        z = lr @ gla_w_lr[d] + gla_b_lr[d]
        g = jnp.maximum(jax.nn.log_sigmoid(z) / GLA_TAU, GLA_MIN_LOG_DECAY)
        return g.astype(jnp.float32).reshape(B, L, GLA_HEADS, GLA_DK)

    g_fwd = gla_decay(glr_f, 0)
    g_bwd = gla_decay(glr_b, 1)
    o_gla = gla_scan(q, k, v, g_fwd) + flip(gla_scan(flip(q), flip(k), flip(v), flip(g_bwd)))
    o_gla = head_rmsnorm(o_gla, gla_out_norm) * jax.nn.silu(gg).reshape(B, L, GLA_HEADS, GLA_DV)

    q = mq.reshape(B, L, MLSTM_HEADS, MLSTM_DK) * (MLSTM_DK ** -0.5)
    k = mk.reshape(B, L, MLSTM_HEADS, MLSTM_DK)
    v = mv.reshape(B, L, MLSTM_HEADS, MLSTM_DV)
    gates = (mgates.reshape(B, L, 4, MLSTM_HEADS) + mlstm_gate_b).astype(jnp.float32)
    log_i_fwd = gates[:, :, 0]
    log_f_fwd = jax.nn.log_sigmoid(gates[:, :, 1])
    log_i_bwd = gates[:, :, 2]
    log_f_bwd = jax.nn.log_sigmoid(gates[:, :, 3])
    h_m = (mlstm_scan(q, k, v, log_i_fwd, log_f_fwd)
           + flip(mlstm_scan(flip(q), flip(k), flip(v), flip(log_i_bwd), flip(log_f_bwd))))
    h_m = head_rmsnorm(h_m, mlstm_out_norm) * jax.nn.sigmoid(mo).reshape(B, L, MLSTM_HEADS, MLSTM_DV)

    merged = jnp.concatenate([o_gla.reshape(B, L, GLA_WIDTH), h_m.reshape(B, L, MLSTM_WIDTH)], axis=-1)
    return merged.astype(h.dtype) @ w_out


def mem_cross_attn(h, mem_n, wq, wk, wv, wo):
    B, L, _ = h.shape
    M = mem_n.shape[1]
    q = (h @ wq).reshape(B, L, XATTN_HEADS, XATTN_HEAD_DIM).astype(jnp.float32)
    k = (mem_n @ wk).reshape(B, M, XATTN_HEADS, XATTN_HEAD_DIM).astype(jnp.float32)
    v = (mem_n @ wv).reshape(B, M, XATTN_HEADS, XATTN_HEAD_DIM).astype(jnp.float32)
    s = jnp.einsum("blhd,bmhd->bhlm", q, k) * (XATTN_HEAD_DIM ** -0.5)
    p = jax.nn.softmax(s, axis=-1)
    o = jnp.einsum("bhlm,bmhd->blhd", p, v).reshape(B, L, XATTN_WIDTH).astype(h.dtype)
    return o @ wo


def trunk(x, mem,
          ffn1_norm, ffn1_w_gate, ffn1_w_up, ffn1_w_down,
          mix_norm, w_in, gla_w_lr, gla_b_lr, gla_out_norm, mlstm_gate_b, mlstm_out_norm, w_out,
          xattn_norm, mem_norm, xattn_wq, xattn_wk, xattn_wv, xattn_wo,
          ffn2_norm, ffn2_w_gate, ffn2_w_up, ffn2_w_down,
          final_norm):
    for l in range(DEPTH):
        x = x + 0.5 * swiglu(rmsnorm(x, ffn1_norm[l]), ffn1_w_gate[l], ffn1_w_up[l], ffn1_w_down[l])
        x = x + parallel_mixer(rmsnorm(x, mix_norm[l]), w_in[l], gla_w_lr[l], gla_b_lr[l],
                               gla_out_norm[l], mlstm_gate_b[l], mlstm_out_norm[l], w_out[l])
        x = x + mem_cross_attn(rmsnorm(x, xattn_norm[l]), rmsnorm(mem, mem_norm[l]),
                               xattn_wq[l], xattn_wk[l], xattn_wv[l], xattn_wo[l])
        x = x + 0.5 * swiglu(rmsnorm(x, ffn2_norm[l]), ffn2_w_gate[l], ffn2_w_up[l], ffn2_w_down[l])
    return rmsnorm(x, final_norm)


def setup_inputs(seed: int = 0) -> dict:
    key = jax.random.key(seed)
    ks = iter(jax.random.split(key, 64))

    def nrm(shape, scale):
        return jax.random.normal(next(ks), shape, jnp.float32) * scale

    def gain(shape):
        return 1.0 + nrm(shape, 0.02)

    D = D_MODEL
    gate_offset = (jnp.array([0.0, 1.0, 0.0, 1.0], jnp.float32)[:, None]
                   * jnp.linspace(3.0, 6.0, MLSTM_HEADS, dtype=jnp.float32)[None, :])
    return {
        "x_prompt": nrm((BATCH, SEQ, D), 1.0),
        "x_sample": nrm((DEC_BATCH, DEC_SEQ, D), 1.0),
        "mem_prompt": nrm((BATCH, MEM_TOKENS, D), 1.0),
        "mem_sample": nrm((DEC_BATCH, MEM_TOKENS, D), 1.0),
        "ffn1_norm": gain((DEPTH, D)),
        "ffn1_w_gate": nrm((DEPTH, D, D_FF), D ** -0.5),
        "ffn1_w_up": nrm((DEPTH, D, D_FF), D ** -0.5),
        "ffn1_w_down": nrm((DEPTH, D_FF, D), D_FF ** -0.5),
        "mix_norm": gain((DEPTH, D)),
        "w_in": nrm((DEPTH, D, IN_WIDTH), D ** -0.5),
        "gla_w_lr": nrm((DEPTH, 2, GLA_RANK, GLA_QK), GLA_RANK ** -0.5),
        "gla_b_lr": nrm((DEPTH, 2, GLA_QK), 0.1),
        "gla_out_norm": gain((DEPTH, GLA_WIDTH)),
        "mlstm_gate_b": gate_offset[None] + nrm((DEPTH, 4, MLSTM_HEADS), 0.1),
        "mlstm_out_norm": gain((DEPTH, MLSTM_WIDTH)),
        "w_out": nrm((DEPTH, MIX_WIDTH, D), MIX_WIDTH ** -0.5),
        "xattn_norm": gain((DEPTH, D)),
        "mem_norm": gain((DEPTH, D)),
        "xattn_wq": nrm((DEPTH, D, XATTN_WIDTH), D ** -0.5),
        "xattn_wk": nrm((DEPTH, D, XATTN_WIDTH), D ** -0.5),
        "xattn_wv": nrm((DEPTH, D, XATTN_WIDTH), D ** -0.5),
        "xattn_wo": nrm((DEPTH, XATTN_WIDTH, D), XATTN_WIDTH ** -0.5),
        "ffn2_norm": gain((DEPTH, D)),
        "ffn2_w_gate": nrm((DEPTH, D, D_FF), D ** -0.5),
        "ffn2_w_up": nrm((DEPTH, D, D_FF), D ** -0.5),
        "ffn2_w_down": nrm((DEPTH, D_FF, D), D_FF ** -0.5),
        "final_norm": gain((D,)),
    }


def reference(x_prompt, x_sample, mem_prompt, mem_sample,
              ffn1_norm, ffn1_w_gate, ffn1_w_up, ffn1_w_down,
              mix_norm, w_in, gla_w_lr, gla_b_lr, gla_out_norm, mlstm_gate_b, mlstm_out_norm, w_out,
              xattn_norm, mem_norm, xattn_wq, xattn_wk, xattn_wv, xattn_wo,
              ffn2_norm, ffn2_w_gate, ffn2_w_up, ffn2_w_down,
              final_norm):
    y_prompt = trunk(x_prompt, mem_prompt,
                     ffn1_norm, ffn1_w_gate, ffn1_w_up, ffn1_w_down,
                     mix_norm, w_in, gla_w_lr, gla_b_lr, gla_out_norm, mlstm_gate_b, mlstm_out_norm, w_out,
                     xattn_norm, mem_norm, xattn_wq, xattn_wk, xattn_wv, xattn_wo,
                     ffn2_norm, ffn2_w_gate, ffn2_w_up, ffn2_w_down,
                     final_norm)
    y_sample = trunk(x_sample, mem_sample,
                     ffn1_norm, ffn1_w_gate, ffn1_w_up, ffn1_w_down,
                     mix_norm, w_in, gla_w_lr, gla_b_lr, gla_out_norm, mlstm_gate_b, mlstm_out_norm, w_out,
                     xattn_norm, mem_norm, xattn_wq, xattn_wk, xattn_wv, xattn_wo,
                     ffn2_norm, ffn2_w_gate, ffn2_w_up, ffn2_w_down,
                     final_norm)
    return (y_prompt, y_sample)
```

```python
import functools

import jax
import jax.numpy as jnp
from jax import lax
from jax.experimental import pallas as pl
from jax.experimental.pallas import tpu as pltpu

F32 = jnp.float32
BF16 = jnp.bfloat16
HIGHEST = lax.Precision.HIGHEST

EPS = 1e-6
NEG_BIG = -1e30
HEADS = 4
GLA_RANK = 16
GLA_TAU = 16.0
GLA_MIN_LOG_DECAY = -1.0
GLA_CHUNK = 64
MLSTM_CHUNK = 128
XATTN_HEADS = 4
LANE = 128
SUBLANE = 8
VMEM_LIMIT = 56 * 1024 * 1024

NT_DIMS = (((1,), (1,)), ((), ()))
TN_DIMS = (((0,), (0,)), ((), ()))


def _pick(n, target, mult):
    if n <= target:
        return n
    t = (target // mult) * mult
    while t >= mult:
        if n % t == 0:
            return t
        t -= mult
    return n


def _params(sem):
    return pltpu.CompilerParams(dimension_semantics=sem, vmem_limit_bytes=VMEM_LIMIT)


def _log_sigmoid(x):
    return jnp.minimum(x, 0.0) - jnp.log1p(jnp.exp(-jnp.abs(x)))


def _sigmoid(x):
    return 1.0 / (1.0 + jnp.exp(-x))


def _rmsnorm_body(x_ref, w_ref, o_ref):
    x = x_ref[...]
    ms = jnp.mean(x * x, axis=-1, keepdims=True)
    o_ref[...] = (x * lax.rsqrt(ms + EPS) * w_ref[...]).astype(o_ref.dtype)


def _rmsnorm(x, w, out_dtype, row_start=0, rows=None):
    m, d = x.shape
    rows = m if rows is None else rows
    tr = _pick(rows, 256, SUBLANE * 2)
    assert row_start % tr == 0
    off = row_start // tr
    return pl.pallas_call(
        _rmsnorm_body,
        grid=(rows // tr,),
        in_specs=[pl.BlockSpec((tr, d), lambda i: (i + off, 0)),
                  pl.BlockSpec((1, d), lambda i: (0, 0))],
        out_specs=pl.BlockSpec((tr, d), lambda i: (i, 0)),
        out_shape=jax.ShapeDtypeStruct((rows, d), out_dtype),
        compiler_params=_params(("parallel",)),
        name="rmsnorm",
    )(x, w.reshape(1, d).astype(F32))


def _mm_body(a_ref, b_ref, o_ref):
    o_ref[...] = jnp.dot(a_ref[...], b_ref[...], preferred_element_type=F32).astype(o_ref.dtype)


def _matmul(a, b, out_dtype, tm=1024, tn=1024, name="matmul"):
    m, k = a.shape
    n = b.shape[1]
    tm = _pick(m, tm, 16)
    tn = _pick(n, tn, LANE)
    return pl.pallas_call(
        _mm_body,
        grid=(m // tm, n // tn),
        in_specs=[pl.BlockSpec((tm, k), lambda i, j: (i, 0)),
                  pl.BlockSpec((k, tn), lambda i, j: (0, j))],
        out_specs=pl.BlockSpec((tm, tn), lambda i, j: (i, j)),
        out_shape=jax.ShapeDtypeStruct((m, n), out_dtype),
        compiler_params=_params(("parallel", "arbitrary")),
        name=name,
    )(a, b)


def _swiglu_body(a_ref, wg_ref, wu_ref, o_ref):
    a = a_ref[...]
    g = jnp.dot(a, wg_ref[...], preferred_element_type=F32)
    u = jnp.dot(a, wu_ref[...], preferred_element_type=F32)
    o_ref[...] = (g * _sigmoid(g) * u).astype(o_ref.dtype)


def _swiglu_up(a, wg, wu, tm=1024, tn=512):
    m, k = a.shape
    n = wg.shape[1]
    tm = _pick(m, tm, 16)
    tn = _pick(n, tn, LANE)
    return pl.pallas_call(
        _swiglu_body,
        grid=(m // tm, n // tn),
        in_specs=[pl.BlockSpec((tm, k), lambda i, j: (i, 0)),
                  pl.BlockSpec((k, tn), lambda i, j: (0, j)),
                  pl.BlockSpec((k, tn), lambda i, j: (0, j))],
        out_specs=pl.BlockSpec((tm, tn), lambda i, j: (i, j)),
        out_shape=jax.ShapeDtypeStruct((m, n), BF16),
        compiler_params=_params(("parallel", "arbitrary")),
        name="swiglu_up",
    )(a, wg, wu)


def _mm_res_body(a_ref, b_ref, r_ref, o_ref, *, scale, nk):
    p = jnp.dot(a_ref[...], b_ref[...], preferred_element_type=F32)
    if nk == 1:
        o_ref[...] = r_ref[...] + scale * p
        return
    kk = pl.program_id(2)

    @pl.when(kk == 0)
    def _():
        o_ref[...] = p

    @pl.when((kk > 0) & (kk < nk - 1))
    def _():
        o_ref[...] += p

    @pl.when(kk == nk - 1)
    def _():
        o_ref[...] = r_ref[...] + scale * (o_ref[...] + p)


def _matmul_residual(a, b, res, scale, tm=1024, tn=1024, tk=None, name="matmul_res"):
    m, k = a.shape
    n = b.shape[1]
    tm = _pick(m, tm, 16)
    tn = _pick(n, tn, LANE)
    tk = k if tk is None else _pick(k, tk, LANE)
    nk = k // tk
    return pl.pallas_call(
        functools.partial(_mm_res_body, scale=scale, nk=nk),
        grid=(m // tm, n // tn, nk),
        in_specs=[pl.BlockSpec((tm, tk), lambda i, j, l: (i, l)),
                  pl.BlockSpec((tk, tn), lambda i, j, l: (l, j)),
                  pl.BlockSpec((tm, tn), lambda i, j, l: (i, j))],
        out_specs=pl.BlockSpec((tm, tn), lambda i, j, l: (i, j)),
        out_shape=jax.ShapeDtypeStruct((m, n), F32),
        compiler_params=_params(("parallel", "arbitrary", "arbitrary")),
        name=name,
    )(a, b, res)


def _mm2_res_body(a1_ref, a2_ref, b1_ref, b2_ref, r_ref, o_ref):
    p = jnp.dot(a1_ref[...], b1_ref[...], preferred_element_type=F32)
    p = p + jnp.dot(a2_ref[...], b2_ref[...], preferred_element_type=F32)
    o_ref[...] = r_ref[...] + p


def _out_proj(a1, a2, w, res, tm=1024, tn=512):
    m, k1 = a1.shape
    k2 = a2.shape[1]
    assert k1 == k2 and w.shape[0] == k1 + k2
    n = w.shape[1]
    tm = _pick(m, tm, 16)
    tn = _pick(n, tn, LANE)
    return pl.pallas_call(
        _mm2_res_body,
        grid=(m // tm, n // tn),
        in_specs=[pl.BlockSpec((tm, k1), lambda i, j: (i, 0)),
                  pl.BlockSpec((tm, k2), lambda i, j: (i, 0)),
                  pl.BlockSpec((k1, tn), lambda i, j: (0, j)),
                  pl.BlockSpec((k2, tn), lambda i, j: (1, j)),
                  pl.BlockSpec((tm, tn), lambda i, j: (i, j))],
        out_specs=pl.BlockSpec((tm, tn), lambda i, j: (i, j)),
        out_shape=jax.ShapeDtypeStruct((m, n), F32),
        compiler_params=_params(("parallel", "arbitrary")),
        name="mixer_out_proj",
    )(a1, a2, w, w, res)


def _small_proj_body(a_ref, w_ref, wt_ref, o_ref, ot_ref):
    a = a_ref[...]
    o_ref[...] = jnp.dot(a, w_ref[...], preferred_element_type=F32)
    ot_ref[...] = lax.dot_general(wt_ref[...], a, NT_DIMS, preferred_element_type=F32)


def _small_proj(a, w, wt, tm=1024):
    m, k = a.shape
    n = w.shape[1]
    nt = wt.shape[0]
    tm = _pick(m, tm, LANE)
    return pl.pallas_call(
        _small_proj_body,
        grid=(m // tm,),
        in_specs=[pl.BlockSpec((tm, k), lambda i: (i, 0)),
                  pl.BlockSpec((k, n), lambda i: (0, 0)),
                  pl.BlockSpec((nt, k), lambda i: (0, 0))],
        out_specs=[pl.BlockSpec((tm, n), lambda i: (i, 0)),
                   pl.BlockSpec((nt, tm), lambda i: (0, i))],
        out_shape=[jax.ShapeDtypeStruct((m, n), F32),
                   jax.ShapeDtypeStruct((nt, m), F32)],
        compiler_params=_params(("parallel",)),
        name="gate_proj",
    )(a, w, wt)


def _block_cumsum_matrix(lb, chunk, reverse):
    r = lax.broadcasted_iota(jnp.int32, (lb, lb), 0)
    c = lax.broadcasted_iota(jnp.int32, (lb, lb), 1)
    same = (r // chunk) == (c // chunk)
    tri = (c >= r) if reverse else (c <= r)
    return jnp.where(same & tri, 1.0, 0.0).astype(F32)


def _causal_mask(chunk, reverse):
    r = lax.broadcasted_iota(jnp.int32, (chunk, chunk), 0)
    c = lax.broadcasted_iota(jnp.int32, (chunk, chunk), 1)
    return (c >= r) if reverse else (c <= r)


def _head_norm_store(o_ref, rows, h, other_ref, gate, nw_ref):
    h = h + other_ref[rows, :]
    y = h * lax.rsqrt(jnp.mean(h * h, axis=-1, keepdims=True) + EPS) * nw_ref[...]
    o_ref[rows, :] = (y * gate).astype(o_ref.dtype)


def _gla_body(q_ref, k_ref, v_ref, lr_ref, wlr_ref, blr_ref, *rest, reverse, final, lb):
    if final:
        gg_ref, ob_ref, nw_ref, o_ref, s_ref = rest
    else:
        o_ref, s_ref = rest
    dk = q_ref.shape[-1]
    dv = v_ref.shape[-1]
    chunk = GLA_CHUNK

    @pl.when(pl.program_id(2) == 0)
    def _():
        s_ref[...] = jnp.zeros_like(s_ref)

    z = jnp.dot(lr_ref[...], wlr_ref[...], precision=HIGHEST, preferred_element_type=F32) + blr_ref[...]
    g = jnp.maximum(_log_sigmoid(z) * (1.0 / GLA_TAU), GLA_MIN_LOG_DECAY)
    bcum = jnp.dot(_block_cumsum_matrix(lb, chunk, reverse), g,
                   precision=HIGHEST, preferred_element_type=F32)
    mask = _causal_mask(chunk, reverse)
    scale = dk ** -0.5
    nch = lb // chunk
    for ch in (range(nch - 1, -1, -1) if reverse else range(nch)):
        rows = slice(ch * chunk, (ch + 1) * chunk)
        b = bcum[rows, :]
        tot = b[0:1, :] if reverse else b[chunk - 1:chunk, :]
        q = q_ref[rows, :].astype(F32)
        k = k_ref[rows, :].astype(F32)
        v = v_ref[rows, :]
        q_dec = (q * (jnp.exp(b) * scale)).astype(BF16)
        k_inv = (k * jnp.exp(-b)).astype(BF16)
        k_end = (k * jnp.exp(tot - b)).astype(BF16)
        a = lax.dot_general(q_dec, k_inv, NT_DIMS, preferred_element_type=F32)
        a = jnp.where(mask, a, 0.0).astype(BF16)
        s = s_ref[...]
        o = (jnp.dot(a, v, preferred_element_type=F32)
             + jnp.dot(q_dec, s.astype(BF16), preferred_element_type=F32))
        dec = jnp.exp(jnp.transpose(jnp.broadcast_to(tot, (LANE, dk))))
        dec = jnp.concatenate([dec] * (dv // LANE), axis=1)
        s_ref[...] = s * dec + lax.dot_general(k_end, v, TN_DIMS, preferred_element_type=F32)
        if final:
            gg = gg_ref[rows, :].astype(F32)
            _head_norm_store(o_ref, rows, o, ob_ref, gg * _sigmoid(gg), nw_ref)
        else:
            o_ref[rows, :] = o


def _gla_scan(proj, small, wlr, blr, *, reverse, dk, dv, col_q, col_k, col_v, col_gate=None,
              other=None, norm_w=None):
    bsz, seq, _ = proj.shape
    lb = _pick(seq, 512, LANE)
    nblk = seq // lb
    final = other is not None

    def tok(i):
        return (nblk - 1 - i) if reverse else i

    def col_spec(width, col0):
        return pl.BlockSpec((None, lb, width), lambda b, h, i: (b, tok(i), col0 // width + h))

    d = 1 if reverse else 0
    in_specs = [col_spec(dk, col_q), col_spec(dk, col_k), col_spec(dv, col_v),
                pl.BlockSpec((None, lb, LANE), lambda b, h, i: (b, tok(i), 0)),
                pl.BlockSpec((None, LANE, dk), lambda b, h, i: (d, 0, h)),
                pl.BlockSpec((None, 1, dk), lambda b, h, i: (d, 0, h))]
    args = [proj, proj, proj, small, wlr, blr]
    if final:
        in_specs += [col_spec(dv, col_gate),
                     pl.BlockSpec((None, lb, dv), lambda b, h, i: (b, tok(i), h)),
                     pl.BlockSpec((1, dv), lambda b, h, i: (0, h))]
        args += [proj, other, norm_w]
    return pl.pallas_call(
        functools.partial(_gla_body, reverse=reverse, final=final, lb=lb),
        grid=(bsz, HEADS, nblk),
        in_specs=in_specs,
        out_specs=pl.BlockSpec((None, lb, dv), lambda b, h, i: (b, tok(i), h)),
        out_shape=jax.ShapeDtypeStruct((bsz, seq, HEADS * dv), BF16 if final else F32),
        scratch_shapes=[pltpu.VMEM((dk, dv), F32)],
        compiler_params=_params(("parallel", "parallel", "arbitrary")),
        name="gla_fwd" if final else "gla_bwd",
    )(*args)


def _segment_cumsum_lanes(x, seg, reverse):
    n = x.shape[-1]
    pos = lax.broadcasted_iota(jnp.int32, x.shape, x.ndim - 1) % seg
    s = 1
    while s < seg:
        if reverse:
            x = x + jnp.where(pos < seg - s, pltpu.roll(x, n - s, x.ndim - 1), 0.0)
        else:
            x = x + jnp.where(pos >= s, pltpu.roll(x, s, x.ndim - 1), 0.0)
        s *= 2
    return x


def _mlstm_body(q_ref, k_ref, v_ref, gc_ref, gr_ref, bc_ref, br_ref, *rest, reverse, final, lb):
    if final:
        mo_ref, hb_ref, nw_ref, o_ref, c_ref, m_ref = rest
    else:
        o_ref, c_ref, m_ref = rest
    dk = q_ref.shape[-1]
    dv = v_ref.shape[-1]
    chunk = MLSTM_CHUNK

    @pl.when(pl.program_id(2) == 0)
    def _():
        c_ref[...] = jnp.zeros_like(c_ref)
        m_ref[...] = jnp.full_like(m_ref, NEG_BIG)

    ti, tf = (2, 3) if reverse else (0, 1)
    gc = gc_ref[...] + bc_ref[...]
    gr = gr_ref[...] + br_ref[:, 0:1]
    i_col = gc[:, ti:ti + 1]
    f_colb = jnp.broadcast_to(_log_sigmoid(gc[:, tf:tf + 1]), (lb, LANE))
    b_colb = jnp.dot(_block_cumsum_matrix(lb, chunk, reverse), f_colb,
                     precision=HIGHEST, preferred_element_type=F32)
    i_row = gr[ti:ti + 1, :]
    b_row = _segment_cumsum_lanes(_log_sigmoid(gr), chunk, reverse)[tf:tf + 1, :]
    u_row = i_row - b_row

    mask = _causal_mask(chunk, reverse)
    ones_col = jnp.where(lax.broadcasted_iota(jnp.int32, (chunk, LANE), 1) == 0, 1.0, 0.0).astype(BF16)
    scale = dk ** -0.5
    nch = lb // chunk
    for ch in (range(nch - 1, -1, -1) if reverse else range(nch)):
        rows = slice(ch * chunk, (ch + 1) * chunk)
        bc = b_colb[rows, 0:1]
        gtot = bc[0:1, :] if reverse else bc[chunk - 1:chunk, :]
        log_d = jnp.where(mask, bc + u_row[:, rows], NEG_BIG)
        m_intra = jnp.max(log_d, axis=1, keepdims=True)
        qs = (q_ref[rows, :].astype(F32) * scale).astype(BF16)
        k = k_ref[rows, :]
        v_aug = jnp.concatenate([v_ref[rows, :], ones_col], axis=1)
        s = lax.dot_general(qs, k, NT_DIMS, preferred_element_type=F32) * jnp.exp(log_d - m_intra)
        intra = jnp.dot(s.astype(BF16), v_aug, preferred_element_type=F32)
        c = c_ref[...]
        inter = jnp.dot(qs, c.astype(BF16), preferred_element_type=F32)
        m_prev = m_ref[0:1, 0:1]
        bm = bc + m_prev
        m_j = jnp.maximum(bm, m_intra)
        comb = jnp.exp(bm - m_j) * inter + jnp.exp(m_intra - m_j) * intra
        denom = jnp.maximum(jnp.abs(comb[:, dv:dv + 1]), jnp.exp(-m_j))
        h = comb[:, :dv] * (1.0 / denom)
        a_col = gtot - bc + i_col[rows, :]
        m_new = jnp.maximum(gtot + m_prev, jnp.max(a_col, axis=0, keepdims=True))
        kw = (k.astype(F32) * jnp.exp(a_col - m_new)).astype(BF16)
        c_ref[...] = (jnp.exp(gtot + m_prev - m_new) * c
                      + lax.dot_general(kw, v_aug, TN_DIMS, preferred_element_type=F32))
        m_ref[...] = jnp.broadcast_to(m_new, m_ref.shape)
        if final:
            _head_norm_store(o_ref, rows, h, hb_ref, _sigmoid(mo_ref[rows, :].astype(F32)), nw_ref)
        else:
            o_ref[rows, :] = h


def _mlstm_scan(proj, small, small_t, bias_c, bias_r, *, reverse, dk, dv, col_q, col_k, col_v,
                col_gate=None, other=None, norm_w=None):
    bsz, seq, _ = proj.shape
    lb = _pick(seq, 512, LANE)
    nblk = seq // lb
    final = other is not None

    def tok(i):
        return (nblk - 1 - i) if reverse else i

    def col_spec(width, col0):
        return pl.BlockSpec((None, lb, width), lambda b, h, i: (b, tok(i), col0 // width + h))

    in_specs = [col_spec(dk, col_q), col_spec(dk, col_k), col_spec(dv, col_v),
                pl.BlockSpec((None, lb, LANE), lambda b, h, i: (b, tok(i), 1 + h)),
                pl.BlockSpec((SUBLANE, lb), lambda b, h, i: (h, b * nblk + tok(i))),
                pl.BlockSpec((1, LANE), lambda b, h, i: (0, h)),
                pl.BlockSpec((SUBLANE, LANE), lambda b, h, i: (h, 0))]
    args = [proj, proj, proj, small, small_t, bias_c, bias_r]
    if final:
        in_specs += [col_spec(dv, col_gate),
                     pl.BlockSpec((None, lb, dv), lambda b, h, i: (b, tok(i), h)),
                     pl.BlockSpec((1, dv), lambda b, h, i: (0, h))]
        args += [proj, other, norm_w]
    return pl.pallas_call(
        functools.partial(_mlstm_body, reverse=reverse, final=final, lb=lb),
        grid=(bsz, HEADS, nblk),
        in_specs=in_specs,
        out_specs=pl.BlockSpec((None, lb, dv), lambda b, h, i: (b, tok(i), h)),
        out_shape=jax.ShapeDtypeStruct((bsz, seq, HEADS * dv), BF16 if final else F32),
        scratch_shapes=[pltpu.VMEM((dk, dv + LANE), F32), pltpu.VMEM((SUBLANE, LANE), F32)],
        compiler_params=_params(("parallel", "parallel", "arbitrary")),
        name="mlstm_fwd" if final else "mlstm_bwd",
    )(*args)


def _xattn_body(x_ref, nw_ref, wq_ref, kv_ref, wo_ref, o_ref, *, heads):
    x = x_ref[...]
    xn = (x * lax.rsqrt(jnp.mean(x * x, axis=-1, keepdims=True) + EPS) * nw_ref[...]).astype(BF16)
    width = wq_ref.shape[1]
    hd = width // heads
    q = jnp.dot(xn, wq_ref[...], preferred_element_type=F32) * (hd ** -0.5)
    q = q.astype(BF16)
    outs = []
    for h in range(heads):
        kh = kv_ref[:, h * hd:(h + 1) * hd]
        vh = kv_ref[:, width + h * hd:width + (h + 1) * hd]
        s = lax.dot_general(q[:, h * hd:(h + 1) * hd], kh, NT_DIMS, preferred_element_type=F32)
        p = jnp.exp(s - jnp.max(s, axis=-1, keepdims=True))
        p = p * (1.0 / jnp.sum(p, axis=-1, keepdims=True))
        outs.append(jnp.dot(p.astype(BF16), vh, preferred_element_type=F32))
    o = jnp.concatenate(outs, axis=1).astype(BF16)
    o_ref[...] = x + jnp.dot(o, wo_ref[...], preferred_element_type=F32)


def _xattn(x, norm_w, wq, kv, wo, tq=256):
    bsz, seq, d = x.shape
    mem = kv.shape[1]
    width = wq.shape[1]
    tq = _pick(seq, tq, 16)
    return pl.pallas_call(
        functools.partial(_xattn_body, heads=XATTN_HEADS),
        grid=(bsz, seq // tq),
        in_specs=[pl.BlockSpec((None, tq, d), lambda b, i: (b, i, 0)),
                  pl.BlockSpec((1, d), lambda b, i: (0, 0)),
                  pl.BlockSpec((d, width), lambda b, i: (0, 0)),
                  pl.BlockSpec((None, mem, 2 * width), lambda b, i: (b, 0, 0)),
                  pl.BlockSpec((width, d), lambda b, i: (0, 0))],
        out_specs=pl.BlockSpec((None, tq, d), lambda b, i: (b, i, 0)),
        out_shape=jax.ShapeDtypeStruct((bsz, seq, d), F32),
        compiler_params=_params(("parallel", "arbitrary")),
        name="mem_xattn",
    )(x, norm_w.reshape(1, d).astype(F32), wq, kv, wo)


def _pad_cols(w, n):
    return w if w.shape[-1] == n else jnp.pad(w, ((0, 0), (0, n - w.shape[-1])))


def _ffn(x, norm_w, w_gate, w_up, w_down):
    f = w_gate.shape[1]
    fp = -(-f // 1024) * 1024 if f >= 1024 else f
    wg = _pad_cols(w_gate, fp).astype(BF16)
    wu = _pad_cols(w_up, fp).astype(BF16)
    wd = jnp.pad(w_down, ((0, fp - f), (0, 0))).astype(BF16)
    xn = _rmsnorm(x, norm_w, BF16)
    hid = _swiglu_up(xn, wg, wu)
    tk = fp // 4 if fp % (4 * LANE) == 0 else fp
    return _matmul_residual(hid, wd, x, 0.5, tk=tk, name="ffn_down")


def _mixer(x, bsz, seq, norm_w, w_in, gla_w_lr, gla_b_lr, gla_out_norm, mlstm_gate_b, mlstm_out_norm, w_out):
    m, d = x.shape
    dv = d // (2 * HEADS)
    dk = dv // 2
    qk = HEADS * dk
    half = d // 2
    o_lr = 2 * qk + 2 * half
    o_m = o_lr + 2 * GLA_RANK
    o_mg = o_m + 2 * qk + 2 * half
    gla_w = 2 * qk + 2 * half
    w_big = jnp.concatenate([w_in[:, :o_lr], w_in[:, o_m:o_mg]], axis=1).astype(BF16)
    mg = w_in[:, o_mg:o_mg + 4 * HEADS].reshape(d, 4, HEADS)
    tiles = [jnp.pad(w_in[:, o_lr:o_m], ((0, 0), (0, LANE - 2 * GLA_RANK)))]
    tiles += [jnp.pad(mg[:, :, h], ((0, 0), (0, LANE - 4))) for h in range(HEADS)]
    w_small = jnp.concatenate(tiles, axis=1).astype(BF16)
    w_small_t = jnp.concatenate(
        [jnp.pad(mg[:, :, h].T, ((0, SUBLANE - 4), (0, 0))) for h in range(HEADS)], axis=0).astype(BF16)
    bias_c = jnp.concatenate(
        [jnp.pad(mlstm_gate_b[:, h].reshape(1, 4), ((0, 0), (0, LANE - 4))) for h in range(HEADS)], axis=1)
    bias_r = jnp.concatenate(
        [jnp.broadcast_to(jnp.pad(mlstm_gate_b[:, h], (0, SUBLANE - 4))[:, None], (SUBLANE, LANE))
         for h in range(HEADS)], axis=0)
    wlr = jnp.stack([jnp.pad(gla_w_lr[0], ((0, LANE - GLA_RANK), (0, 0))),
                     jnp.pad(gla_w_lr[1], ((GLA_RANK, LANE - 2 * GLA_RANK), (0, 0)))]).astype(F32)
    blr = gla_b_lr.reshape(2, 1, qk).astype(F32)

    xn = _rmsnorm(x, norm_w, BF16)
    proj = _matmul(xn, w_big, BF16, name="mixer_in_proj").reshape(bsz, seq, -1)
    small, small_t = _small_proj(xn, w_small, w_small_t)
    small = small.reshape(bsz, seq, -1)

    gla_kw = dict(dk=dk, dv=dv, col_q=0, col_k=qk, col_v=2 * qk)
    g_b = _gla_scan(proj, small, wlr, blr, reverse=True, **gla_kw)
    g_out = _gla_scan(proj, small, wlr, blr, reverse=False, col_gate=2 * qk + half, other=g_b,
                      norm_w=gla_out_norm.reshape(1, half).astype(F32), **gla_kw)
    ml_kw = dict(dk=dk, dv=dv, col_q=gla_w, col_k=gla_w + qk, col_v=gla_w + 2 * qk)
    m_b = _mlstm_scan(proj, small, small_t, bias_c, bias_r, reverse=True, **ml_kw)
    m_out = _mlstm_scan(proj, small, small_t, bias_c, bias_r, reverse=False,
                        col_gate=gla_w + 2 * qk + half, other=m_b,
                        norm_w=mlstm_out_norm.reshape(1, half).astype(F32), **ml_kw)
    return _out_proj(g_out.reshape(m, half), m_out.reshape(m, half), w_out.astype(BF16), x)


def kernel(x_prompt, x_sample, mem_prompt, mem_sample, ffn1_norm, ffn1_w_gate, ffn1_w_up, ffn1_w_down, mix_norm, w_in, gla_w_lr, gla_b_lr, gla_out_norm, mlstm_gate_b, mlstm_out_norm, w_out, xattn_norm, mem_norm, xattn_wq, xattn_wk, xattn_wv, xattn_wo, ffn2_norm, ffn2_w_gate, ffn2_w_up, ffn2_w_down, final_norm):
    assert x_prompt.shape[1:] == x_sample.shape[1:] and mem_prompt.shape[1:] == mem_sample.shape[1:]
    n_p, seq, d = x_prompt.shape
    n_s = x_sample.shape[0]
    bsz = n_p + n_s
    m = bsz * seq
    x = jnp.concatenate([x_prompt, x_sample], axis=0).reshape(m, d)
    mem = jnp.concatenate([mem_prompt, mem_sample], axis=0)
    n_mem = mem.shape[1]
    mem = mem.reshape(bsz * n_mem, d)
    depth = ffn1_norm.shape[0]
    for l in range(depth):
        x = _ffn(x, ffn1_norm[l], ffn1_w_gate[l], ffn1_w_up[l], ffn1_w_down[l])
        x = _mixer(x, bsz, seq, mix_norm[l], w_in[l], gla_w_lr[l], gla_b_lr[l], gla_out_norm[l],
                   mlstm_gate_b[l], mlstm_out_norm[l], w_out[l])
        w_kv = jnp.concatenate([xattn_wk[l], xattn_wv[l]], axis=1).astype(BF16)
        kv = _matmul(_rmsnorm(mem, mem_norm[l], BF16), w_kv, BF16, tm=512, name="mem_kv_proj")
        x = _xattn(x.reshape(bsz, seq, d), xattn_norm[l], xattn_wq[l].astype(BF16),
                   kv.reshape(bsz, n_mem, -1), xattn_wo[l].astype(BF16)).reshape(m, d)
        x = _ffn(x, ffn2_norm[l], ffn2_w_gate[l], ffn2_w_up[l], ffn2_w_down[l])
    y_p = _rmsnorm(x, final_norm, F32, 0, n_p * seq).reshape(n_p, seq, d)
    y_s = _rmsnorm(x, final_norm, F32, n_p * seq, n_s * seq).reshape(n_s, seq, d)
    return (y_p, y_s)
```

```python
import functools

import jax
import jax.numpy as jnp
from jax import lax
from jax.experimental import pallas as pl
from jax.experimental.pallas import tpu as pltpu

F32 = jnp.float32
BF16 = jnp.bfloat16

EPS = 1e-6
NEG_BIG = -1e30
HEADS = 4
GLA_RANK = 16
GLA_TAU = 16.0
GLA_MIN_LOG_DECAY = -1.0
GLA_CHUNK = 64
MLSTM_CHUNK = 128
XATTN_HEADS = 4
LANE = 128
SUBLANE = 8
VMEM_LIMIT = 56 * 1024 * 1024

NT_DIMS = (((1,), (1,)), ((), ()))
TN_DIMS = (((0,), (0,)), ((), ()))


def _pick(n, target, mult):
    if n <= target:
        return n
    t = (target // mult) * mult
    while t >= mult:
        if n % t == 0:
            return t
        t -= mult
    return n


def _params(sem):
    return pltpu.CompilerParams(dimension_semantics=sem, vmem_limit_bytes=VMEM_LIMIT)


def _log_sigmoid(x):
    return jnp.minimum(x, 0.0) - jnp.log1p(jnp.exp(-jnp.abs(x)))


def _sigmoid(x):
    return 1.0 / (1.0 + jnp.exp(-x))


def _split_bf16(x):
    hi = x.astype(BF16)
    return hi, (x - hi.astype(F32)).astype(BF16)


def _dot(a, b):
    return jnp.dot(a, b, preferred_element_type=F32)


def _rmsnorm_body(x_ref, w_ref, o_ref):
    x = x_ref[...]
    ms = jnp.mean(x * x, axis=-1, keepdims=True)
    o_ref[...] = (x * lax.rsqrt(ms + EPS) * w_ref[...]).astype(o_ref.dtype)


def _rmsnorm(x, w, l, out_dtype, row_start=0, rows=None):
    m, d = x.shape
    rows = m if rows is None else rows
    tr = _pick(rows, 256, SUBLANE * 2)
    assert row_start % tr == 0
    off = row_start // tr
    return pl.pallas_call(
        _rmsnorm_body,
        grid=(rows // tr,),
        in_specs=[pl.BlockSpec((tr, d), lambda i: (i + off, 0)),
                  pl.BlockSpec((None, 1, d), lambda i: (l, 0, 0))],
        out_specs=pl.BlockSpec((tr, d), lambda i: (i, 0)),
        out_shape=jax.ShapeDtypeStruct((rows, d), out_dtype),
        compiler_params=_params(("parallel",)),
        name="rmsnorm",
    )(x, w)


def _mm_body(a_ref, b_ref, o_ref):
    o_ref[...] = _dot(a_ref[...], b_ref[...]).astype(o_ref.dtype)


def _matmul(a, w, l, out_dtype, tm=1024, tn=1024, name="matmul"):
    m, k = a.shape
    n = w.shape[2]
    tm = _pick(m, tm, 16)
    tn = _pick(n, tn, LANE)
    return pl.pallas_call(
        _mm_body,
        grid=(m // tm, n // tn),
        in_specs=[pl.BlockSpec((tm, k), lambda i, j: (i, 0)),
                  pl.BlockSpec((None, k, tn), lambda i, j: (l, 0, j))],
        out_specs=pl.BlockSpec((tm, tn), lambda i, j: (i, j)),
        out_shape=jax.ShapeDtypeStruct((m, n), out_dtype),
        compiler_params=_params(("parallel", "arbitrary")),
        name=name,
    )(a, w)


def _swiglu_body(a_ref, wg_ref, wu_ref, o_ref):
    a = a_ref[...]
    g = _dot(a, wg_ref[...])
    u = _dot(a, wu_ref[...])
    o_ref[...] = (g * _sigmoid(g) * u).astype(o_ref.dtype)


def _swiglu_up(a, wg, wu, l, tm=1024, tn=512):
    m, k = a.shape
    n = wg.shape[2]
    tm = _pick(m, tm, 16)
    tn = min(tn, n)
    return pl.pallas_call(
        _swiglu_body,
        grid=(m // tm, pl.cdiv(n, tn)),
        in_specs=[pl.BlockSpec((tm, k), lambda i, j: (i, 0)),
                  pl.BlockSpec((None, k, tn), lambda i, j: (l, 0, j)),
                  pl.BlockSpec((None, k, tn), lambda i, j: (l, 0, j))],
        out_specs=pl.BlockSpec((tm, tn), lambda i, j: (i, j)),
        out_shape=jax.ShapeDtypeStruct((m, n), BF16),
        compiler_params=_params(("parallel", "arbitrary")),
        name="swiglu_up",
    )(a, wg, wu)


def _mm_res_body(a_ref, b_ref, r_ref, o_ref, *, scale):
    o_ref[...] = r_ref[...] + scale * _dot(a_ref[...], b_ref[...])


def _matmul_residual(a, w, l, res, scale, tm=512, tn=512, name="matmul_res"):
    m, k = a.shape
    n = w.shape[2]
    tm = _pick(m, tm, 16)
    tn = _pick(n, tn, LANE)
    return pl.pallas_call(
        functools.partial(_mm_res_body, scale=scale),
        grid=(m // tm, n // tn),
        in_specs=[pl.BlockSpec((tm, k), lambda i, j: (i, 0)),
                  pl.BlockSpec((None, k, tn), lambda i, j: (l, 0, j)),
                  pl.BlockSpec((tm, tn), lambda i, j: (i, j))],
        out_specs=pl.BlockSpec((tm, tn), lambda i, j: (i, j)),
        out_shape=jax.ShapeDtypeStruct((m, n), F32),
        compiler_params=_params(("parallel", "arbitrary")),
        name=name,
    )(a, w, res)


def _mm2_res_body(a1_ref, a2_ref, b1_ref, b2_ref, r_ref, o_ref):
    o_ref[...] = r_ref[...] + (_dot(a1_ref[...], b1_ref[...]) + _dot(a2_ref[...], b2_ref[...]))


def _out_proj(a1, a2, w, l, res, tm=1024, tn=512):
    m, k1 = a1.shape
    k2 = a2.shape[1]
    assert k1 == k2 and w.shape[1] == k1 + k2
    n = w.shape[2]
    tm = _pick(m, tm, 16)
    tn = _pick(n, tn, LANE)
    return pl.pallas_call(
        _mm2_res_body,
        grid=(m // tm, n // tn),
        in_specs=[pl.BlockSpec((tm, k1), lambda i, j: (i, 0)),
                  pl.BlockSpec((tm, k2), lambda i, j: (i, 0)),
                  pl.BlockSpec((None, k1, tn), lambda i, j: (l, 0, j)),
                  pl.BlockSpec((None, k2, tn), lambda i, j: (l, 1, j)),
                  pl.BlockSpec((tm, tn), lambda i, j: (i, j))],
        out_specs=pl.BlockSpec((tm, tn), lambda i, j: (i, j)),
        out_shape=jax.ShapeDtypeStruct((m, n), F32),
        compiler_params=_params(("parallel", "arbitrary")),
        name="mixer_out_proj",
    )(a1, a2, w, w, res)


def _gate_proj_body(a_ref, w_ref, wt_ref, o_ref, ot_ref):
    a = a_ref[...]
    o_ref[...] = _dot(a, w_ref[...])
    ot_ref[...] = lax.dot_general(wt_ref[...], a, NT_DIMS, preferred_element_type=F32)


def _gate_proj(a, w, wt, l, tm=1024):
    m, k = a.shape
    n = w.shape[2]
    nt = wt.shape[1]
    tm = _pick(m, tm, LANE)
    return pl.pallas_call(
        _gate_proj_body,
        grid=(m // tm,),
        in_specs=[pl.BlockSpec((tm, k), lambda i: (i, 0)),
                  pl.BlockSpec((None, k, n), lambda i: (l, 0, 0)),
                  pl.BlockSpec((None, nt, k), lambda i: (l, 0, 0))],
        out_specs=[pl.BlockSpec((tm, n), lambda i: (i, 0)),
                   pl.BlockSpec((nt, tm), lambda i: (0, i))],
        out_shape=[jax.ShapeDtypeStruct((m, n), F32),
                   jax.ShapeDtypeStruct((nt, m), F32)],
        compiler_params=_params(("parallel",)),
        name="gate_proj",
    )(a, w, wt)


def _causal_mask(chunk, reverse):
    r = lax.broadcasted_iota(jnp.int32, (chunk, chunk), 0)
    c = lax.broadcasted_iota(jnp.int32, (chunk, chunk), 1)
    return (c >= r) if reverse else (c <= r)


def _head_norm_store(o_ref, rows, h, other_ref, gate, nw_ref):
    h = h + other_ref[rows, :]
    y = h * lax.rsqrt(jnp.mean(h * h, axis=-1, keepdims=True) + EPS) * nw_ref[...]
    o_ref[rows, :] = (y * gate).astype(o_ref.dtype)


def _gla_body(q_ref, k_ref, v_ref, lr_ref, wlr_ref, blr_ref, *rest, reverse, final, lb):
    if final:
        gg_ref, ob_ref, nw_ref, o_ref, s_ref = rest
    else:
        o_ref, s_ref = rest
    dk = q_ref.shape[-1]
    dv = v_ref.shape[-1]
    chunk = GLA_CHUNK
    nch = lb // chunk

    @pl.when(pl.program_id(2) == 0)
    def _():
        s_ref[...] = jnp.zeros_like(s_ref)

    lh, ll = _split_bf16(lr_ref[...])
    wh, wl = _split_bf16(wlr_ref[...])
    z = _dot(lh, wh) + _dot(lh, wl) + _dot(ll, wh) + blr_ref[...]
    g = jnp.maximum(_log_sigmoid(z) * (1.0 / GLA_TAU), GLA_MIN_LOG_DECAY)
    mask = _causal_mask(chunk, reverse)
    tri = jnp.where(mask, 1.0, 0.0).astype(BF16)
    gh, gl = _split_bf16(jnp.concatenate([g[c * chunk:(c + 1) * chunk, :] for c in range(nch)], axis=1))
    bcum = _dot(tri, gh) + _dot(tri, gl)

    scale = dk ** -0.5
    for ch in (range(nch - 1, -1, -1) if reverse else range(nch)):
        rows = slice(ch * chunk, (ch + 1) * chunk)
        b = bcum[:, ch * dk:(ch + 1) * dk]
        tot = b[0:1, :] if reverse else b[chunk - 1:chunk, :]
        q = q_ref[rows, :].astype(F32)
        k = k_ref[rows, :].astype(F32)
        v = v_ref[rows, :]
        q_dec = (q * (jnp.exp(b) * scale)).astype(BF16)
        k_inv = (k * jnp.exp(-b)).astype(BF16)
        k_end = (k * jnp.exp(tot - b)).astype(BF16)
        a = lax.dot_general(q_dec, k_inv, NT_DIMS, preferred_element_type=F32)
        a = jnp.where(mask, a, 0.0).astype(BF16)
        s = s_ref[...]
        o = _dot(a, v) + _dot(q_dec, s.astype(BF16))
        dec = jnp.exp(jnp.transpose(jnp.broadcast_to(tot, (LANE, dk))))
        dec = jnp.concatenate([dec] * (dv // LANE), axis=1)
        s_ref[...] = s * dec + lax.dot_general(k_end, v, TN_DIMS, preferred_element_type=F32)
        if final:
            gg = gg_ref[rows, :].astype(F32)
            _head_norm_store(o_ref, rows, o, ob_ref, gg * _sigmoid(gg), nw_ref)
        else:
            o_ref[rows, :] = o


def _gla_scan(proj, lr, wlr, blr, l, *, reverse, dk, dv, col_q, col_k, col_v, col_gate=None,
              other=None, norm_w=None):
    bsz, seq, _ = proj.shape
    lb = _pick(seq, 512, LANE)
    nblk = seq // lb
    final = other is not None

    def tok(i):
        return (nblk - 1 - i) if reverse else i

    def col_spec(width, col0):
        return pl.BlockSpec((None, lb, width), lambda b, h, i: (b, tok(i), col0 // width + h))

    d = 1 if reverse else 0
    in_specs = [col_spec(dk, col_q), col_spec(dk, col_k), col_spec(dv, col_v),
                pl.BlockSpec((None, lb, LANE), lambda b, h, i: (b, tok(i), 0)),
                pl.BlockSpec((None, None, LANE, dk), lambda b, h, i: (l, d, 0, h)),
                pl.BlockSpec((None, None, 1, dk), lambda b, h, i: (l, d, 0, h))]
    args = [proj, proj, proj, lr, wlr, blr]
    if final:
        in_specs += [col_spec(dv, col_gate),
                     pl.BlockSpec((None, lb, dv), lambda b, h, i: (b, tok(i), h)),
                     pl.BlockSpec((None, 1, dv), lambda b, h, i: (l, 0, h))]
        args += [proj, other, norm_w]
    return pl.pallas_call(
        functools.partial(_gla_body, reverse=reverse, final=final, lb=lb),
        grid=(bsz, HEADS, nblk),
        in_specs=in_specs,
        out_specs=pl.BlockSpec((None, lb, dv), lambda b, h, i: (b, tok(i), h)),
        out_shape=jax.ShapeDtypeStruct((bsz, seq, HEADS * dv), BF16 if final else F32),
        scratch_shapes=[pltpu.VMEM((dk, dv), F32)],
        compiler_params=_params(("parallel", "parallel", "arbitrary")),
        name="gla_fwd" if final else "gla_bwd",
    )(*args)


def _segment_cumsum_lanes(x, seg, reverse):
    n = x.shape[-1]
    pos = lax.broadcasted_iota(jnp.int32, x.shape, x.ndim - 1) % seg
    s = 1
    while s < seg:
        if reverse:
            x = x + jnp.where(pos < seg - s, pltpu.roll(x, n - s, x.ndim - 1), 0.0)
        else:
            x = x + jnp.where(pos >= s, pltpu.roll(x, s, x.ndim - 1), 0.0)
        s *= 2
    return x


def _mlstm_body(q_ref, k_ref, v_ref, gr_ref, br_ref, *rest, reverse, final, lb):
    if final:
        mo_ref, hb_ref, nw_ref, o_ref, c_ref, m_ref = rest
    else:
        o_ref, c_ref, m_ref = rest
    dk = q_ref.shape[-1]
    dv = v_ref.shape[-1]
    chunk = MLSTM_CHUNK

    @pl.when(pl.program_id(2) == 0)
    def _():
        c_ref[...] = jnp.zeros_like(c_ref)
        m_ref[...] = jnp.full_like(m_ref, NEG_BIG)

    ti, tf = (2, 3) if reverse else (0, 1)
    gr = gr_ref[...] + br_ref[:, 0:1]
    bcum = _segment_cumsum_lanes(_log_sigmoid(gr), chunk, reverse)
    row_id = lax.broadcasted_iota(jnp.int32, gr.shape, 0)
    rowform = jnp.where(row_id == tf, bcum, gr)
    u_row = rowform[ti:ti + 1, :] - rowform[tf:tf + 1, :]
    pad_rows = jnp.zeros((LANE - SUBLANE, chunk), F32)

    mask = _causal_mask(chunk, reverse)
    ones_col = jnp.where(lax.broadcasted_iota(jnp.int32, (chunk, LANE), 1) == 0, 1.0, 0.0).astype(BF16)
    scale = dk ** -0.5
    nch = lb // chunk
    for ch in (range(nch - 1, -1, -1) if reverse else range(nch)):
        rows = slice(ch * chunk, (ch + 1) * chunk)
        colform = jnp.transpose(jnp.concatenate([rowform[:, rows], pad_rows], axis=0))
        bc = colform[:, tf:tf + 1]
        ic = colform[:, ti:ti + 1]
        gtot = bc[0:1, :] if reverse else bc[chunk - 1:chunk, :]
        log_d = jnp.where(mask, bc + u_row[:, rows], NEG_BIG)
        m_intra = jnp.max(log_d, axis=1, keepdims=True)
        qs = (q_ref[rows, :].astype(F32) * scale).astype(BF16)
        k = k_ref[rows, :]
        v_aug = jnp.concatenate([v_ref[rows, :], ones_col], axis=1)
        s = lax.dot_general(qs, k, NT_DIMS, preferred_element_type=F32) * jnp.exp(log_d - m_intra)
        intra = _dot(s.astype(BF16), v_aug)
        c = c_ref[...]
        inter = _dot(qs, c.astype(BF16))
        m_prev = m_ref[0:1, 0:1]
        bm = bc + m_prev
        m_j = jnp.maximum(bm, m_intra)
        comb = jnp.exp(bm - m_j) * inter + jnp.exp(m_intra - m_j) * intra
        denom = jnp.maximum(jnp.abs(comb[:, dv:dv + 1]), jnp.exp(-m_j))
        h = comb[:, :dv] * (1.0 / denom)
        a_col = gtot - bc + ic
        m_new = jnp.maximum(gtot + m_prev, jnp.max(a_col, axis=0, keepdims=True))
        kw = (k.astype(F32) * jnp.exp(a_col - m_new)).astype(BF16)
        c_ref[...] = (jnp.exp(gtot + m_prev - m_new) * c
                      + lax.dot_general(kw, v_aug, TN_DIMS, preferred_element_type=F32))
        m_ref[...] = jnp.broadcast_to(m_new, m_ref.shape)
        if final:
            _head_norm_store(o_ref, rows, h, hb_ref, _sigmoid(mo_ref[rows, :].astype(F32)), nw_ref)
        else:
            o_ref[rows, :] = h


def _mlstm_scan(proj, gates_t, bias_r, l, *, reverse, dk, dv, col_q, col_k, col_v,
                col_gate=None, other=None, norm_w=None):
    bsz, seq, _ = proj.shape
    lb = _pick(seq, 512, LANE)
    nblk = seq // lb
    final = other is not None

    def tok(i):
        return (nblk - 1 - i) if reverse else i

    def col_spec(width, col0):
        return pl.BlockSpec((None, lb, width), lambda b, h, i: (b, tok(i), col0 // width + h))

    in_specs = [col_spec(dk, col_q), col_spec(dk, col_k), col_spec(dv, col_v),
                pl.BlockSpec((SUBLANE, lb), lambda b, h, i: (h, b * nblk + tok(i))),
                pl.BlockSpec((None, SUBLANE, LANE), lambda b, h, i: (l, h, 0))]
    args = [proj, proj, proj, gates_t, bias_r]
    if final:
        in_specs += [col_spec(dv, col_gate),
                     pl.BlockSpec((None, lb, dv), lambda b, h, i: (b, tok(i), h)),
                     pl.BlockSpec((None, 1, dv), lambda b, h, i: (l, 0, h))]
        args += [proj, other, norm_w]
    return pl.pallas_call(
        functools.partial(_mlstm_body, reverse=reverse, final=final, lb=lb),
        grid=(bsz, HEADS, nblk),
        in_specs=in_specs,
        out_specs=pl.BlockSpec((None, lb, dv), lambda b, h, i: (b, tok(i), h)),
        out_shape=jax.ShapeDtypeStruct((bsz, seq, HEADS * dv), BF16 if final else F32),
        scratch_shapes=[pltpu.VMEM((dk, dv + LANE), F32), pltpu.VMEM((SUBLANE, LANE), F32)],
        compiler_params=_params(("parallel", "parallel", "arbitrary")),
        name="mlstm_fwd" if final else "mlstm_bwd",
    )(*args)


def _xattn_body(x_ref, nw_ref, wq_ref, kv_ref, wo_ref, o_ref, *, heads):
    x = x_ref[...]
    xn = (x * lax.rsqrt(jnp.mean(x * x, axis=-1, keepdims=True) + EPS) * nw_ref[...]).astype(BF16)
    width = wq_ref.shape[1]
    hd = width // heads
    q = (_dot(xn, wq_ref[...]) * (hd ** -0.5)).astype(BF16)
    outs = []
    for h in range(heads):
        kh = kv_ref[:, h * hd:(h + 1) * hd]
        vh = kv_ref[:, width + h * hd:width + (h + 1) * hd]
        s = lax.dot_general(q[:, h * hd:(h + 1) * hd], kh, NT_DIMS, preferred_element_type=F32)
        p = jnp.exp(s - jnp.max(s, axis=-1, keepdims=True))
        p = p * (1.0 / jnp.sum(p, axis=-1, keepdims=True))
        outs.append(_dot(p.astype(BF16), vh))
    o = jnp.concatenate(outs, axis=1).astype(BF16)
    o_ref[...] = x + _dot(o, wo_ref[...])


def _xattn(x, norm_w, wq, kv, wo, l, tq=256):
    bsz, seq, d = x.shape
    mem = kv.shape[1]
    width = wq.shape[2]
    tq = _pick(seq, tq, 16)
    return pl.pallas_call(
        functools.partial(_xattn_body, heads=XATTN_HEADS),
        grid=(bsz, seq // tq),
        in_specs=[pl.BlockSpec((None, tq, d), lambda b, i: (b, i, 0)),
                  pl.BlockSpec((None, 1, d), lambda b, i: (l, 0, 0)),
                  pl.BlockSpec((None, d, width), lambda b, i: (l, 0, 0)),
                  pl.BlockSpec((None, mem, 2 * width), lambda b, i: (b, 0, 0)),
                  pl.BlockSpec((None, width, d), lambda b, i: (l, 0, 0))],
        out_specs=pl.BlockSpec((None, tq, d), lambda b, i: (b, i, 0)),
        out_shape=jax.ShapeDtypeStruct((bsz, seq, d), F32),
        compiler_params=_params(("parallel", "arbitrary")),
        name="mem_xattn",
    )(x, norm_w, wq, kv, wo)


def _norm_w(w):
    return w.astype(F32).reshape(w.shape[0], 1, w.shape[1])


def _ffn(x, norm_w, w_gate, w_up, w_down, l):
    hid = _swiglu_up(_rmsnorm(x, norm_w, l, BF16), w_gate, w_up, l)
    return _matmul_residual(hid, w_down, l, x, 0.5, name="ffn_down")


def _mixer_weights(d, w_in, gla_w_lr, gla_b_lr, mlstm_gate_b):
    depth = w_in.shape[0]
    half = d // 2
    qk = half // 2
    o_lr = 2 * qk + 2 * half
    o_m = o_lr + 2 * GLA_RANK
    o_mg = o_m + 2 * qk + 2 * half
    w_big = jnp.concatenate([w_in[:, :, :o_lr], w_in[:, :, o_m:o_mg]], axis=2).astype(BF16)
    w_lr = jnp.pad(w_in[:, :, o_lr:o_m], ((0, 0), (0, 0), (0, LANE - 2 * GLA_RANK))).astype(BF16)
    mg = w_in[:, :, o_mg:o_mg + 4 * HEADS].reshape(depth, d, 4, HEADS).transpose(0, 3, 2, 1)
    w_gates_t = jnp.pad(mg, ((0, 0), (0, 0), (0, SUBLANE - 4), (0, 0))).reshape(depth, HEADS * SUBLANE, d)
    w_gates_t = w_gates_t.astype(BF16)
    bias_r = jnp.pad(mlstm_gate_b.astype(F32).transpose(0, 2, 1), ((0, 0), (0, 0), (0, SUBLANE - 4)))
    bias_r = jnp.broadcast_to(bias_r.reshape(depth, HEADS * SUBLANE, 1), (depth, HEADS * SUBLANE, LANE))
    wlr = jnp.stack([jnp.pad(gla_w_lr[:, 0], ((0, 0), (0, LANE - GLA_RANK), (0, 0))),
                     jnp.pad(gla_w_lr[:, 1], ((0, 0), (GLA_RANK, LANE - 2 * GLA_RANK), (0, 0)))],
                    axis=1).astype(F32)
    blr = gla_b_lr.astype(F32).reshape(depth, 2, 1, qk)
    return w_big, w_lr, w_gates_t, bias_r, wlr, blr


def _mixer(x, bsz, seq, l, norm_w, w_big, w_lr, w_gates_t, bias_r, wlr, blr, gla_norm, mlstm_norm, w_out):
    m, d = x.shape
    half = d // 2
    dv = half // HEADS
    dk = dv // 2
    qk = HEADS * dk
    gla_w = 2 * qk + 2 * half
    xn = _rmsnorm(x, norm_w, l, BF16)
    proj = _matmul(xn, w_big, l, BF16, name="mixer_in_proj").reshape(bsz, seq, -1)
    lr, gates_t = _gate_proj(xn, w_lr, w_gates_t, l)
    lr = lr.reshape(bsz, seq, LANE)

    gla_kw = dict(dk=dk, dv=dv, col_q=0, col_k=qk, col_v=2 * qk)
    g_b = _gla_scan(proj, lr, wlr, blr, l, reverse=True, **gla_kw)
    g_out = _gla_scan(proj, lr, wlr, blr, l, reverse=False, col_gate=2 * qk + half, other=g_b,
                      norm_w=gla_norm, **gla_kw)
    ml_kw = dict(dk=dk, dv=dv, col_q=gla_w, col_k=gla_w + qk, col_v=gla_w + 2 * qk)
    m_b = _mlstm_scan(proj, gates_t, bias_r, l, reverse=True, **ml_kw)
    m_out = _mlstm_scan(proj, gates_t, bias_r, l, reverse=False, col_gate=gla_w + 2 * qk + half,
                        other=m_b, norm_w=mlstm_norm, **ml_kw)
    return _out_proj(g_out.reshape(m, half), m_out.reshape(m, half), w_out, l, x)


def kernel(x_prompt, x_sample, mem_prompt, mem_sample, ffn1_norm, ffn1_w_gate, ffn1_w_up, ffn1_w_down, mix_norm, w_in, gla_w_lr, gla_b_lr, gla_out_norm, mlstm_gate_b, mlstm_out_norm, w_out, xattn_norm, mem_norm, xattn_wq, xattn_wk, xattn_wv, xattn_wo, ffn2_norm, ffn2_w_gate, ffn2_w_up, ffn2_w_down, final_norm):
    assert x_prompt.shape[1:] == x_sample.shape[1:] and mem_prompt.shape[1:] == mem_sample.shape[1:]
    n_p, seq, d = x_prompt.shape
    n_s = x_sample.shape[0]
    bsz = n_p + n_s
    m = bsz * seq
    x = jnp.concatenate([x_prompt, x_sample], axis=0).reshape(m, d)
    mem = jnp.concatenate([mem_prompt, mem_sample], axis=0)
    n_mem = mem.shape[1]
    mem = mem.reshape(bsz * n_mem, d)
    depth = ffn1_norm.shape[0]

    ffn1 = (_norm_w(ffn1_norm), ffn1_w_gate.astype(BF16), ffn1_w_up.astype(BF16), ffn1_w_down.astype(BF16))
    ffn2 = (_norm_w(ffn2_norm), ffn2_w_gate.astype(BF16), ffn2_w_up.astype(BF16), ffn2_w_down.astype(BF16))
    mixer_w = _mixer_weights(d, w_in, gla_w_lr, gla_b_lr, mlstm_gate_b)
    mixer_rest = (_norm_w(gla_out_norm), _norm_w(mlstm_out_norm), w_out.astype(BF16))
    w_kv = jnp.concatenate([xattn_wk, xattn_wv], axis=2).astype(BF16)
    wq = xattn_wq.astype(BF16)
    wo = xattn_wo.astype(BF16)
    mix_n, xattn_n, mem_n = _norm_w(mix_norm), _norm_w(xattn_norm), _norm_w(mem_norm)

    for l in range(depth):
        x = _ffn(x, *ffn1, l)
        x = _mixer(x, bsz, seq, l, mix_n, *mixer_w, *mixer_rest)
        kv = _matmul(_rmsnorm(mem, mem_n, l, BF16), w_kv, l, BF16, tm=512, name="mem_kv_proj")
        x = _xattn(x.reshape(bsz, seq, d), xattn_n, wq, kv.reshape(bsz, n_mem, -1), wo, l).reshape(m, d)
        x = _ffn(x, *ffn2, l)
    fin = _norm_w(final_norm.reshape(1, d))
    y_p = _rmsnorm(x, fin, 0, F32, 0, n_p * seq).reshape(n_p, seq, d)
    y_s = _rmsnorm(x, fin, 0, F32, n_p * seq, n_s * seq).reshape(n_s, seq, d)
    return (y_p, y_s)
```

```python
import functools

import jax
import jax.numpy as jnp
from jax import lax
from jax.experimental import pallas as pl
from jax.experimental.pallas import tpu as pltpu

F32 = jnp.float32
BF16 = jnp.bfloat16

EPS = 1e-6
NEG_BIG = -1e30
HEADS = 4
GLA_RANK = 16
GLA_TAU = 16.0
GLA_MIN_LOG_DECAY = -1.0
GLA_CHUNK = 64
MLSTM_CHUNK = 256
XATTN_HEADS = 4
LANE = 128
SUBLANE = 8
VMEM_LIMIT = 56 * 1024 * 1024

NT_DIMS = (((1,), (1,)), ((), ()))
TN_DIMS = (((0,), (0,)), ((), ()))


def _pick(n, target, mult):
    if n <= target:
        return n
    t = (target // mult) * mult
    while t >= mult:
        if n % t == 0:
            return t
        t -= mult
    return n


def _params(sem):
    return pltpu.CompilerParams(dimension_semantics=sem, vmem_limit_bytes=VMEM_LIMIT)


def _log_sigmoid(x):
    return jnp.minimum(x, 0.0) - jnp.log(1.0 + jnp.exp(-jnp.abs(x)))


def _sigmoid(x):
    return 0.5 * jnp.tanh(0.5 * x) + 0.5


def _split_bf16(x):
    hi = x.astype(BF16)
    return hi, (x - hi.astype(F32)).astype(BF16)


def _dot(a, b):
    return jnp.dot(a, b, preferred_element_type=F32)


def _rmsnorm_body(x_ref, w_ref, o_ref):
    x = x_ref[...]
    ms = jnp.mean(x * x, axis=-1, keepdims=True)
    o_ref[...] = (x * lax.rsqrt(ms + EPS) * w_ref[...]).astype(o_ref.dtype)


def _rmsnorm(x, w, l, out_dtype, row_start=0, rows=None):
    m, d = x.shape
    rows = m if rows is None else rows
    tr = _pick(rows, 256, SUBLANE * 2)
    assert row_start % tr == 0
    off = row_start // tr
    return pl.pallas_call(
        _rmsnorm_body,
        grid=(rows // tr,),
        in_specs=[pl.BlockSpec((tr, d), lambda i: (i + off, 0)),
                  pl.BlockSpec((None, 1, d), lambda i: (l, 0, 0))],
        out_specs=pl.BlockSpec((tr, d), lambda i: (i, 0)),
        out_shape=jax.ShapeDtypeStruct((rows, d), out_dtype),
        compiler_params=_params(("parallel",)),
        name="rmsnorm",
    )(x, w)


def _mm_body(a_ref, b_ref, o_ref):
    o_ref[...] = _dot(a_ref[...], b_ref[...]).astype(o_ref.dtype)


def _matmul(a, w, l, out_dtype, tm=1024, tn=1024, name="matmul"):
    m, k = a.shape
    n = w.shape[2]
    tm = _pick(m, tm, 16)
    tn = _pick(n, tn, LANE)
    return pl.pallas_call(
        _mm_body,
        grid=(m // tm, n // tn),
        in_specs=[pl.BlockSpec((tm, k), lambda i, j: (i, 0)),
                  pl.BlockSpec((None, k, tn), lambda i, j: (l, 0, j))],
        out_specs=pl.BlockSpec((tm, tn), lambda i, j: (i, j)),
        out_shape=jax.ShapeDtypeStruct((m, n), out_dtype),
        compiler_params=_params(("parallel", "arbitrary")),
        name=name,
    )(a, w)


def _swiglu_body(a_ref, wg_ref, wu_ref, o_ref):
    a = a_ref[...]
    g = _dot(a, wg_ref[...])
    u = _dot(a, wu_ref[...])
    o_ref[...] = (g * _sigmoid(g) * u).astype(o_ref.dtype)


def _swiglu_up(a, wg, wu, l, tm=1024, tn=512):
    m, k = a.shape
    n = wg.shape[2]
    tm = _pick(m, tm, 16)
    tn = min(tn, n)
    return pl.pallas_call(
        _swiglu_body,
        grid=(m // tm, pl.cdiv(n, tn)),
        in_specs=[pl.BlockSpec((tm, k), lambda i, j: (i, 0)),
                  pl.BlockSpec((None, k, tn), lambda i, j: (l, 0, j)),
                  pl.BlockSpec((None, k, tn), lambda i, j: (l, 0, j))],
        out_specs=pl.BlockSpec((tm, tn), lambda i, j: (i, j)),
        out_shape=jax.ShapeDtypeStruct((m, n), BF16),
        compiler_params=_params(("parallel", "arbitrary")),
        name="swiglu_up",
    )(a, wg, wu)


def _mm_res_body(a_ref, b_ref, r_ref, o_ref, *, scale):
    o_ref[...] = r_ref[...] + scale * _dot(a_ref[...], b_ref[...])


def _matmul_residual(a, w, l, res, scale, tm=512, tn=512, name="matmul_res"):
    m, k = a.shape
    n = w.shape[2]
    tm = _pick(m, tm, 16)
    tn = _pick(n, tn, LANE)
    return pl.pallas_call(
        functools.partial(_mm_res_body, scale=scale),
        grid=(m // tm, n // tn),
        in_specs=[pl.BlockSpec((tm, k), lambda i, j: (i, 0)),
                  pl.BlockSpec((None, k, tn), lambda i, j: (l, 0, j)),
                  pl.BlockSpec((tm, tn), lambda i, j: (i, j))],
        out_specs=pl.BlockSpec((tm, tn), lambda i, j: (i, j)),
        out_shape=jax.ShapeDtypeStruct((m, n), F32),
        compiler_params=_params(("parallel", "arbitrary")),
        name=name,
    )(a, w, res)


def _mm2_res_body(a1_ref, a2_ref, b1_ref, b2_ref, r_ref, o_ref):
    o_ref[...] = r_ref[...] + (_dot(a1_ref[...], b1_ref[...]) + _dot(a2_ref[...], b2_ref[...]))


def _out_proj(a1, a2, w, l, res, tm=1024, tn=512):
    m, k1 = a1.shape
    k2 = a2.shape[1]
    assert k1 == k2 and w.shape[1] == k1 + k2
    n = w.shape[2]
    tm = _pick(m, tm, 16)
    tn = _pick(n, tn, LANE)
    return pl.pallas_call(
        _mm2_res_body,
        grid=(m // tm, n // tn),
        in_specs=[pl.BlockSpec((tm, k1), lambda i, j: (i, 0)),
                  pl.BlockSpec((tm, k2), lambda i, j: (i, 0)),
                  pl.BlockSpec((None, k1, tn), lambda i, j: (l, 0, j)),
                  pl.BlockSpec((None, k2, tn), lambda i, j: (l, 1, j)),
                  pl.BlockSpec((tm, tn), lambda i, j: (i, j))],
        out_specs=pl.BlockSpec((tm, tn), lambda i, j: (i, j)),
        out_shape=jax.ShapeDtypeStruct((m, n), F32),
        compiler_params=_params(("parallel", "arbitrary")),
        name="mixer_out_proj",
    )(a1, a2, w, w, res)


def _gate_proj_body(a_ref, w_ref, wt_ref, o_ref, ot_ref):
    a = a_ref[...]
    o_ref[...] = _dot(a, w_ref[...])
    ot_ref[...] = lax.dot_general(wt_ref[...], a, NT_DIMS, preferred_element_type=F32)


def _gate_proj(a, w, wt, l, tm=1024):
    m, k = a.shape
    n = w.shape[2]
    nt = wt.shape[1]
    tm = _pick(m, tm, LANE)
    return pl.pallas_call(
        _gate_proj_body,
        grid=(m // tm,),
        in_specs=[pl.BlockSpec((tm, k), lambda i: (i, 0)),
                  pl.BlockSpec((None, k, n), lambda i: (l, 0, 0)),
                  pl.BlockSpec((None, nt, k), lambda i: (l, 0, 0))],
        out_specs=[pl.BlockSpec((tm, n), lambda i: (i, 0)),
                   pl.BlockSpec((nt, tm), lambda i: (0, i))],
        out_shape=[jax.ShapeDtypeStruct((m, n), F32),
                   jax.ShapeDtypeStruct((nt, m), F32)],
        compiler_params=_params(("parallel",)),
        name="gate_proj",
    )(a, w, wt)


def _causal_mask(chunk, reverse):
    r = lax.broadcasted_iota(jnp.int32, (chunk, chunk), 0)
    c = lax.broadcasted_iota(jnp.int32, (chunk, chunk), 1)
    return (c >= r) if reverse else (c <= r)


def _head_norm_gate(h, other, gate, nw):
    h = h + other
    return h * lax.rsqrt(jnp.mean(h * h, axis=-1, keepdims=True) + EPS) * nw * gate


def _gla_body(q_ref, k_ref, v_ref, lr_ref, wlr_ref, blr_ref, *rest, reverse, final, lb):
    if final:
        gg_ref, ob_ref, nw_ref, o_ref, s_ref = rest
    else:
        o_ref, s_ref = rest
    dk = q_ref.shape[-1] // HEADS
    dv = v_ref.shape[-1] // HEADS
    chunk = GLA_CHUNK
    nch = lb // chunk
    qk = HEADS * dk

    @pl.when(pl.program_id(1) == 0)
    def _():
        s_ref[...] = jnp.zeros_like(s_ref)

    lh, ll = _split_bf16(lr_ref[...])
    wh, wl = _split_bf16(wlr_ref[...])
    z = _dot(lh, wh) + _dot(lh, wl) + _dot(ll, wh) + blr_ref[...]
    g = jnp.maximum(_log_sigmoid(z) * (1.0 / GLA_TAU), GLA_MIN_LOG_DECAY)
    mask = _causal_mask(chunk, reverse)
    tri = jnp.where(mask, 1.0, 0.0).astype(BF16)
    gh, gl = _split_bf16(jnp.concatenate([g[c * chunk:(c + 1) * chunk, :] for c in range(nch)], axis=1))
    bcum = _dot(tri, gh) + _dot(tri, gl)

    scale = dk ** -0.5
    for ch in (range(nch - 1, -1, -1) if reverse else range(nch)):
        rows = slice(ch * chunk, (ch + 1) * chunk)
        for hh in range(HEADS):
            kc = slice(hh * dk, (hh + 1) * dk)
            vc = slice(hh * dv, (hh + 1) * dv)
            b = bcum[:, ch * qk + hh * dk:ch * qk + (hh + 1) * dk]
            tot = b[0:1, :] if reverse else b[chunk - 1:chunk, :]
            q = q_ref[rows, kc].astype(F32)
            k = k_ref[rows, kc].astype(F32)
            v = v_ref[rows, vc]
            q_dec = (q * (jnp.exp(b) * scale)).astype(BF16)
            k_inv_f = k * jnp.exp(-b)
            k_inv = k_inv_f.astype(BF16)
            k_end = (k_inv_f * jnp.exp(tot)).astype(BF16)
            a = lax.dot_general(q_dec, k_inv, NT_DIMS, preferred_element_type=F32)
            a = jnp.where(mask, a, 0.0).astype(BF16)
            s = s_ref[hh]
            o = _dot(a, v) + _dot(q_dec, s.astype(BF16))
            dec = jnp.exp(jnp.transpose(jnp.broadcast_to(tot, (LANE, dk))))
            dec = jnp.concatenate([dec] * (dv // LANE), axis=1)
            s_ref[hh] = s * dec + lax.dot_general(k_end, v, TN_DIMS, preferred_element_type=F32)
            if final:
                gg = gg_ref[rows, vc].astype(F32)
                o = _head_norm_gate(o, ob_ref[rows, vc], gg * _sigmoid(gg), nw_ref[:, vc])
            o_ref[rows, vc] = o.astype(o_ref.dtype)


def _gla_scan(proj, lr, wlr, blr, l, *, reverse, dk, dv, col_q, col_k, col_v, col_gate=None,
              other=None, norm_w=None):
    bsz, seq, _ = proj.shape
    lb = _pick(seq, 512, LANE)
    nblk = seq // lb
    final = other is not None
    qk, width = HEADS * dk, HEADS * dv

    def tok(i):
        return (nblk - 1 - i) if reverse else i

    def col_spec(w, col0):
        return pl.BlockSpec((None, lb, w), lambda b, i: (b, tok(i), col0 // w))

    d = 1 if reverse else 0
    in_specs = [col_spec(qk, col_q), col_spec(qk, col_k), col_spec(width, col_v),
                pl.BlockSpec((None, lb, LANE), lambda b, i: (b, tok(i), 0)),
                pl.BlockSpec((None, None, LANE, qk), lambda b, i: (l, d, 0, 0)),
                pl.BlockSpec((None, None, 1, qk), lambda b, i: (l, d, 0, 0))]
    args = [proj, proj, proj, lr, wlr, blr]
    if final:
        in_specs += [col_spec(width, col_gate),
                     pl.BlockSpec((None, lb, width), lambda b, i: (b, tok(i), 0)),
                     pl.BlockSpec((None, 1, width), lambda b, i: (l, 0, 0))]
        args += [proj, other, norm_w]
    return pl.pallas_call(
        functools.partial(_gla_body, reverse=reverse, final=final, lb=lb),
        grid=(bsz, nblk),
        in_specs=in_specs,
        out_specs=pl.BlockSpec((None, lb, width), lambda b, i: (b, tok(i), 0)),
        out_shape=jax.ShapeDtypeStruct((bsz, seq, width), BF16 if final else F32),
        scratch_shapes=[pltpu.VMEM((HEADS, dk, dv), F32)],
        compiler_params=_params(("parallel", "arbitrary")),
        name="gla_fwd" if final else "gla_bwd",
    )(*args)


def _segment_cumsum_lanes(x, seg, reverse):
    n = x.shape[-1]
    pos = lax.broadcasted_iota(jnp.int32, x.shape, x.ndim - 1) % seg
    s = 1
    while s < seg:
        if reverse:
            x = x + jnp.where(pos < seg - s, pltpu.roll(x, n - s, x.ndim - 1), 0.0)
        else:
            x = x + jnp.where(pos >= s, pltpu.roll(x, s, x.ndim - 1), 0.0)
        s *= 2
    return x


def _mlstm_body(q_ref, k_ref, v_ref, gr_ref, br_ref, *rest, reverse, final, lb):
    if final:
        mo_ref, hb_ref, nw_ref, o_ref, c_ref, m_ref = rest
    else:
        o_ref, c_ref, m_ref = rest
    dk = q_ref.shape[-1] // HEADS
    dv = v_ref.shape[-1] // HEADS
    chunk = min(MLSTM_CHUNK, lb)

    @pl.when(pl.program_id(1) == 0)
    def _():
        c_ref[...] = jnp.zeros_like(c_ref)
        m_ref[...] = jnp.full_like(m_ref, NEG_BIG)

    ti, tf = (2, 3) if reverse else (0, 1)
    gr = gr_ref[...] + br_ref[:, 0:1]
    bcum = _segment_cumsum_lanes(_log_sigmoid(gr), chunk, reverse)
    row_id = lax.broadcasted_iota(jnp.int32, gr.shape, 0) % SUBLANE
    rowform = jnp.where(row_id == tf, bcum, gr)
    pad_rows = jnp.zeros((LANE - SUBLANE, chunk), F32)

    mask = _causal_mask(chunk, reverse)
    ones_col = jnp.where(lax.broadcasted_iota(jnp.int32, (chunk, LANE), 1) == 0, 1.0, 0.0).astype(BF16)
    scale = dk ** -0.5
    nch = lb // chunk
    for ch in (range(nch - 1, -1, -1) if reverse else range(nch)):
        rows = slice(ch * chunk, (ch + 1) * chunk)
        for hh in range(HEADS):
            kc = slice(hh * dk, (hh + 1) * dk)
            vc = slice(hh * dv, (hh + 1) * dv)
            rf = rowform[hh * SUBLANE:(hh + 1) * SUBLANE, rows]
            u_row = rf[ti:ti + 1, :] - rf[tf:tf + 1, :]
            colform = jnp.transpose(jnp.concatenate([rf, pad_rows], axis=0))
            bc = colform[:, tf:tf + 1]
            ic = colform[:, ti:ti + 1]
            gtot = bc[0:1, :] if reverse else bc[chunk - 1:chunk, :]
            log_d = jnp.where(mask, bc + u_row, NEG_BIG)
            m_intra = jnp.max(log_d, axis=1, keepdims=True)
            q = q_ref[rows, kc]
            k = k_ref[rows, kc]
            v_aug = jnp.concatenate([v_ref[rows, vc], ones_col], axis=1)
            s = lax.dot_general(q, k, NT_DIMS, preferred_element_type=F32) * jnp.exp(log_d - m_intra)
            intra = _dot(s.astype(BF16), v_aug)
            c = c_ref[hh]
            inter = _dot(q, c.astype(BF16))
            m_prev = m_ref[hh, 0:1, 0:1]
            bm = bc + m_prev
            m_j = jnp.maximum(bm, m_intra)
            comb = (scale * jnp.exp(bm - m_j)) * inter + (scale * jnp.exp(m_intra - m_j)) * intra
            denom = jnp.maximum(jnp.abs(comb[:, dv:dv + 1]), jnp.exp(-m_j))
            h = comb[:, :dv] * (1.0 / denom)
            a_col = gtot - bc + ic
            m_new = jnp.maximum(gtot + m_prev, jnp.max(a_col, axis=0, keepdims=True))
            kw = (k.astype(F32) * jnp.exp(a_col - m_new)).astype(BF16)
            c_ref[hh] = (jnp.exp(gtot + m_prev - m_new) * c
                         + lax.dot_general(kw, v_aug, TN_DIMS, preferred_element_type=F32))
            m_ref[hh] = jnp.broadcast_to(m_new, (SUBLANE, LANE))
            if final:
                h = _head_norm_gate(h, hb_ref[rows, vc], _sigmoid(mo_ref[rows, vc].astype(F32)), nw_ref[:, vc])
            o_ref[rows, vc] = h.astype(o_ref.dtype)


def _mlstm_scan(proj, gates_t, bias_r, l, *, reverse, dk, dv, col_q, col_k, col_v,
                col_gate=None, other=None, norm_w=None):
    bsz, seq, _ = proj.shape
    lb = _pick(seq, 512, LANE)
    nblk = seq // lb
    final = other is not None
    qk, width = HEADS * dk, HEADS * dv

    def tok(i):
        return (nblk - 1 - i) if reverse else i

    def col_spec(w, col0):
        return pl.BlockSpec((None, lb, w), lambda b, i: (b, tok(i), col0 // w))

    in_specs = [col_spec(qk, col_q), col_spec(qk, col_k), col_spec(width, col_v),
                pl.BlockSpec((HEADS * SUBLANE, lb), lambda b, i: (0, b * nblk + tok(i))),
                pl.BlockSpec((None, HEADS * SUBLANE, LANE), lambda b, i: (l, 0, 0))]
    args = [proj, proj, proj, gates_t, bias_r]
    if final:
        in_specs += [col_spec(width, col_gate),
                     pl.BlockSpec((None, lb, width), lambda b, i: (b, tok(i), 0)),
                     pl.BlockSpec((None, 1, width), lambda b, i: (l, 0, 0))]
        args += [proj, other, norm_w]
    return pl.pallas_call(
        functools.partial(_mlstm_body, reverse=reverse, final=final, lb=lb),
        grid=(bsz, nblk),
        in_specs=in_specs,
        out_specs=pl.BlockSpec((None, lb, width), lambda b, i: (b, tok(i), 0)),
        out_shape=jax.ShapeDtypeStruct((bsz, seq, width), BF16 if final else F32),
        scratch_shapes=[pltpu.VMEM((HEADS, dk, dv + LANE), F32), pltpu.VMEM((HEADS, SUBLANE, LANE), F32)],
        compiler_params=_params(("parallel", "arbitrary")),
        name="mlstm_fwd" if final else "mlstm_bwd",
    )(*args)


def _xattn_body(x_ref, nw_ref, wq_ref, kv_ref, wo_ref, nw2_ref, o_ref, n_ref, *, heads):
    x = x_ref[...]
    xn = (x * lax.rsqrt(jnp.mean(x * x, axis=-1, keepdims=True) + EPS) * nw_ref[...]).astype(BF16)
    width = wq_ref.shape[1]
    hd = width // heads
    q = (_dot(xn, wq_ref[...]) * (hd ** -0.5)).astype(BF16)
    outs = []
    for h in range(heads):
        kh = kv_ref[:, h * hd:(h + 1) * hd]
        vh = kv_ref[:, width + h * hd:width + (h + 1) * hd]
        s = lax.dot_general(q[:, h * hd:(h + 1) * hd], kh, NT_DIMS, preferred_element_type=F32)
        p = jnp.exp(s - jnp.max(s, axis=-1, keepdims=True))
        p = p * (1.0 / jnp.sum(p, axis=-1, keepdims=True))
        outs.append(_dot(p.astype(BF16), vh))
    o = jnp.concatenate(outs, axis=1).astype(BF16)
    y = x + _dot(o, wo_ref[...])
    o_ref[...] = y
    n_ref[...] = (y * lax.rsqrt(jnp.mean(y * y, axis=-1, keepdims=True) + EPS) * nw2_ref[...]).astype(BF16)


def _xattn(x, norm_w, wq, kv, wo, next_norm_w, l, tq=256):
    bsz, seq, d = x.shape
    mem = kv.shape[1]
    width = wq.shape[2]
    tq = _pick(seq, tq, 16)
    return pl.pallas_call(
        functools.partial(_xattn_body, heads=XATTN_HEADS),
        grid=(bsz, seq // tq),
        in_specs=[pl.BlockSpec((None, tq, d), lambda b, i: (b, i, 0)),
                  pl.BlockSpec((None, 1, d), lambda b, i: (l, 0, 0)),
                  pl.BlockSpec((None, d, width), lambda b, i: (l, 0, 0)),
                  pl.BlockSpec((None, mem, 2 * width), lambda b, i: (b, 0, 0)),
                  pl.BlockSpec((None, width, d), lambda b, i: (l, 0, 0)),
                  pl.BlockSpec((None, 1, d), lambda b, i: (l, 0, 0))],
        out_specs=[pl.BlockSpec((None, tq, d), lambda b, i: (b, i, 0)),
                   pl.BlockSpec((None, tq, d), lambda b, i: (b, i, 0))],
        out_shape=[jax.ShapeDtypeStruct((bsz, seq, d), F32),
                   jax.ShapeDtypeStruct((bsz, seq, d), BF16)],
        compiler_params=_params(("parallel", "arbitrary")),
        name="mem_xattn",
    )(x, norm_w, wq, kv, wo, next_norm_w)


def _norm_w(w):
    return w.astype(F32).reshape(w.shape[0], 1, w.shape[1])


def _ffn(x, norm_w, w_gate, w_up, w_down, l, xn=None):
    xn = _rmsnorm(x, norm_w, l, BF16) if xn is None else xn
    hid = _swiglu_up(xn, w_gate, w_up, l)
    return _matmul_residual(hid, w_down, l, x, 0.5, name="ffn_down")


def _mixer_weights(d, w_in, gla_w_lr, gla_b_lr, mlstm_gate_b):
    depth = w_in.shape[0]
    half = d // 2
    qk = half // 2
    o_lr = 2 * qk + 2 * half
    o_m = o_lr + 2 * GLA_RANK
    o_mg = o_m + 2 * qk + 2 * half
    w_big = jnp.concatenate([w_in[:, :, :o_lr], w_in[:, :, o_m:o_mg]], axis=2).astype(BF16)
    w_lr = jnp.pad(w_in[:, :, o_lr:o_m], ((0, 0), (0, 0), (0, LANE - 2 * GLA_RANK))).astype(BF16)
    mg = w_in[:, :, o_mg:o_mg + 4 * HEADS].reshape(depth, d, 4, HEADS).transpose(0, 3, 2, 1)
    w_gates_t = jnp.pad(mg, ((0, 0), (0, 0), (0, SUBLANE - 4), (0, 0))).reshape(depth, HEADS * SUBLANE, d)
    w_gates_t = w_gates_t.astype(BF16)
    bias_r = jnp.pad(mlstm_gate_b.astype(F32).transpose(0, 2, 1), ((0, 0), (0, 0), (0, SUBLANE - 4)))
    bias_r = jnp.broadcast_to(bias_r.reshape(depth, HEADS * SUBLANE, 1), (depth, HEADS * SUBLANE, LANE))
    wlr = jnp.stack([jnp.pad(gla_w_lr[:, 0], ((0, 0), (0, LANE - GLA_RANK), (0, 0))),
                     jnp.pad(gla_w_lr[:, 1], ((0, 0), (GLA_RANK, LANE - 2 * GLA_RANK), (0, 0)))],
                    axis=1).astype(F32)
    blr = gla_b_lr.astype(F32).reshape(depth, 2, 1, qk)
    return w_big, w_lr, w_gates_t, bias_r, wlr, blr


def _mixer(x, bsz, seq, l, norm_w, w_big, w_lr, w_gates_t, bias_r, wlr, blr, gla_norm, mlstm_norm, w_out):
    m, d = x.shape
    half = d // 2
    dv = half // HEADS
    dk = dv // 2
    qk = HEADS * dk
    gla_w = 2 * qk + 2 * half
    xn = _rmsnorm(x, norm_w, l, BF16)
    proj = _matmul(xn, w_big, l, BF16, name="mixer_in_proj").reshape(bsz, seq, -1)
    lr, gates_t = _gate_proj(xn, w_lr, w_gates_t, l)
    lr = lr.reshape(bsz, seq, LANE)

    gla_kw = dict(dk=dk, dv=dv, col_q=0, col_k=qk, col_v=2 * qk)
    g_b = _gla_scan(proj, lr, wlr, blr, l, reverse=True, **gla_kw)
    g_out = _gla_scan(proj, lr, wlr, blr, l, reverse=False, col_gate=2 * qk + half, other=g_b,
                      norm_w=gla_norm, **gla_kw)
    ml_kw = dict(dk=dk, dv=dv, col_q=gla_w, col_k=gla_w + qk, col_v=gla_w + 2 * qk)
    m_b = _mlstm_scan(proj, gates_t, bias_r, l, reverse=True, **ml_kw)
    m_out = _mlstm_scan(proj, gates_t, bias_r, l, reverse=False, col_gate=gla_w + 2 * qk + half,
                        other=m_b, norm_w=mlstm_norm, **ml_kw)
    return _out_proj(g_out.reshape(m, half), m_out.reshape(m, half), w_out, l, x)


def kernel(x_prompt, x_sample, mem_prompt, mem_sample, ffn1_norm, ffn1_w_gate, ffn1_w_up, ffn1_w_down, mix_norm, w_in, gla_w_lr, gla_b_lr, gla_out_norm, mlstm_gate_b, mlstm_out_norm, w_out, xattn_norm, mem_norm, xattn_wq, xattn_wk, xattn_wv, xattn_wo, ffn2_norm, ffn2_w_gate, ffn2_w_up, ffn2_w_down, final_norm):
    assert x_prompt.shape[1:] == x_sample.shape[1:] and mem_prompt.shape[1:] == mem_sample.shape[1:]
    n_p, seq, d = x_prompt.shape
    n_s = x_sample.shape[0]
    bsz = n_p + n_s
    m = bsz * seq
    x = jnp.concatenate([x_prompt, x_sample], axis=0).reshape(m, d)
    mem = jnp.concatenate([mem_prompt, mem_sample], axis=0)
    n_mem = mem.shape[1]
    mem = mem.reshape(bsz * n_mem, d)
    depth = ffn1_norm.shape[0]

    ffn1 = (_norm_w(ffn1_norm), ffn1_w_gate.astype(BF16), ffn1_w_up.astype(BF16), ffn1_w_down.astype(BF16))
    ffn2 = (_norm_w(ffn2_norm), ffn2_w_gate.astype(BF16), ffn2_w_up.astype(BF16), ffn2_w_down.astype(BF16))
    mixer_w = _mixer_weights(d, w_in, gla_w_lr, gla_b_lr, mlstm_gate_b)
    mixer_rest = (_norm_w(gla_out_norm), _norm_w(mlstm_out_norm), w_out.astype(BF16))
    w_kv = jnp.concatenate([xattn_wk, xattn_wv], axis=2).astype(BF16)
    wq = xattn_wq.astype(BF16)
    wo = xattn_wo.astype(BF16)
    mix_n, xattn_n, mem_n = _norm_w(mix_norm), _norm_w(xattn_norm), _norm_w(mem_norm)

    for l in range(depth):
        x = _ffn(x, *ffn1, l)
        x = _mixer(x, bsz, seq, l, mix_n, *mixer_w, *mixer_rest)
        kv = _matmul(_rmsnorm(mem, mem_n, l, BF16), w_kv, l, BF16, tm=512, name="mem_kv_proj")
        x, xn = _xattn(x.reshape(bsz, seq, d), xattn_n, wq, kv.reshape(bsz, n_mem, -1), wo, ffn2[0], l)
        x = _ffn(x.reshape(m, d), *ffn2, l, xn=xn.reshape(m, d))
    fin = _norm_w(final_norm.reshape(1, d))
    y_p = _rmsnorm(x, fin, 0, F32, 0, n_p * seq).reshape(n_p, seq, d)
    y_s = _rmsnorm(x, fin, 0, F32, n_p * seq, n_s * seq).reshape(n_s, seq, d)
    return (y_p, y_s)
```

```python
import functools

import jax
import jax.numpy as jnp
from jax import lax
from jax.experimental import pallas as pl
from jax.experimental.pallas import tpu as pltpu

F32 = jnp.float32
BF16 = jnp.bfloat16

EPS = 1e-6
NEG_BIG = -1e30
HEADS = 4
GLA_RANK = 16
GLA_TAU = 16.0
GLA_MIN_LOG_DECAY = -1.0
GLA_CHUNK = 64
MLSTM_CHUNK = 256
XATTN_HEADS = 4
LANE = 128
SUBLANE = 8
VMEM_LIMIT = 56 * 1024 * 1024

NT_DIMS = (((1,), (1,)), ((), ()))
TN_DIMS = (((0,), (0,)), ((), ()))


def _pick(n, target, mult):
    if n <= target:
        return n
    t = (target // mult) * mult
    while t >= mult:
        if n % t == 0:
            return t
        t -= mult
    return n


def _params(sem):
    return pltpu.CompilerParams(dimension_semantics=sem, vmem_limit_bytes=VMEM_LIMIT)


def _log_sigmoid(x):
    return jnp.minimum(x, 0.0) - jnp.log(1.0 + jnp.exp(-jnp.abs(x)))


def _sigmoid(x):
    return 0.5 * jnp.tanh(0.5 * x) + 0.5


def _split_bf16(x):
    hi = x.astype(BF16)
    return hi, (x - hi.astype(F32)).astype(BF16)


def _dot(a, b):
    return jnp.dot(a, b, preferred_element_type=F32)


def _rmsnorm_body(x_ref, w_ref, o_ref):
    x = x_ref[...]
    ms = jnp.mean(x * x, axis=-1, keepdims=True)
    o_ref[...] = (x * lax.rsqrt(ms + EPS) * w_ref[...]).astype(o_ref.dtype)


def _pair_specs(block, na, col=lambda *g: 0):
    return [pl.BlockSpec(block, lambda *g: (jnp.minimum(g[0], na - 1), col(*g))),
            pl.BlockSpec(block, lambda *g: (jnp.maximum(g[0] - na, 0), col(*g)))]


def _rmsnorm_pair_body(xa_ref, xb_ref, w_ref, o_ref, *, na):
    @pl.when(pl.program_id(0) < na)
    def _():
        _rmsnorm_body(xa_ref, w_ref, o_ref)

    @pl.when(pl.program_id(0) >= na)
    def _():
        _rmsnorm_body(xb_ref, w_ref, o_ref)


def _rmsnorm_pair(xa, xb, w, l, out_dtype):
    d = xa.shape[1]
    tr = _pick(xa.shape[0], 256, SUBLANE * 2)
    assert xa.shape[0] % tr == 0 and xb.shape[0] % tr == 0
    na, nb = xa.shape[0] // tr, xb.shape[0] // tr
    return pl.pallas_call(
        functools.partial(_rmsnorm_pair_body, na=na),
        grid=(na + nb,),
        in_specs=_pair_specs((tr, d), na) + [pl.BlockSpec((None, 1, d), lambda i: (l, 0, 0))],
        out_specs=pl.BlockSpec((tr, d), lambda i: (i, 0)),
        out_shape=jax.ShapeDtypeStruct(((na + nb) * tr, d), out_dtype),
        compiler_params=_params(("parallel",)),
        name="rmsnorm",
    )(xa, xb, w)


def _rmsnorm(x, w, l, out_dtype, row_start=0, rows=None):
    m, d = x.shape
    rows = m if rows is None else rows
    tr = _pick(rows, 256, SUBLANE * 2)
    assert row_start % tr == 0
    off = row_start // tr
    return pl.pallas_call(
        _rmsnorm_body,
        grid=(rows // tr,),
        in_specs=[pl.BlockSpec((tr, d), lambda i: (i + off, 0)),
                  pl.BlockSpec((None, 1, d), lambda i: (l, 0, 0))],
        out_specs=pl.BlockSpec((tr, d), lambda i: (i, 0)),
        out_shape=jax.ShapeDtypeStruct((rows, d), out_dtype),
        compiler_params=_params(("parallel",)),
        name="rmsnorm",
    )(x, w)


def _mm_body(a_ref, b_ref, o_ref):
    o_ref[...] = _dot(a_ref[...], b_ref[...]).astype(o_ref.dtype)


def _matmul(a, w, l, out_dtype, tm=1024, tn=1024, name="matmul"):
    m, k = a.shape
    n = w.shape[2]
    tm = _pick(m, tm, 16)
    tn = _pick(n, tn, LANE)
    return pl.pallas_call(
        _mm_body,
        grid=(m // tm, n // tn),
        in_specs=[pl.BlockSpec((tm, k), lambda i, j: (i, 0)),
                  pl.BlockSpec((None, k, tn), lambda i, j: (l, 0, j))],
        out_specs=pl.BlockSpec((tm, tn), lambda i, j: (i, j)),
        out_shape=jax.ShapeDtypeStruct((m, n), out_dtype),
        compiler_params=_params(("parallel", "arbitrary")),
        name=name,
    )(a, w)


def _swiglu_body(a_ref, wg_ref, wu_ref, o_ref):
    a = a_ref[...]
    g = _dot(a, wg_ref[...])
    u = _dot(a, wu_ref[...])
    o_ref[...] = (g * _sigmoid(g) * u).astype(o_ref.dtype)


def _swiglu_up(a, wg, wu, l, tm=2048, tn=256):
    m, k = a.shape
    n = wg.shape[2]
    tm = _pick(m, tm, 16)
    tn = min(tn, n)
    return pl.pallas_call(
        _swiglu_body,
        grid=(m // tm, pl.cdiv(n, tn)),
        in_specs=[pl.BlockSpec((tm, k), lambda i, j: (i, 0)),
                  pl.BlockSpec((None, k, tn), lambda i, j: (l, 0, j)),
                  pl.BlockSpec((None, k, tn), lambda i, j: (l, 0, j))],
        out_specs=pl.BlockSpec((tm, tn), lambda i, j: (i, j)),
        out_shape=jax.ShapeDtypeStruct((m, n), BF16),
        compiler_params=_params(("parallel", "arbitrary")),
        name="swiglu_up",
    )(a, wg, wu)


def _mm_res_body(a_ref, b_ref, *rest, scale, na):
    p = scale * _dot(a_ref[...], b_ref[...])
    if na is None:
        r_ref, o_ref = rest
        o_ref[...] = r_ref[...] + p
        return
    ra_ref, rb_ref, o_ref = rest

    @pl.when(pl.program_id(0) < na)
    def _():
        o_ref[...] = ra_ref[...] + p

    @pl.when(pl.program_id(0) >= na)
    def _():
        o_ref[...] = rb_ref[...] + p


def _block_cols(w, tn):
    depth, k, n = w.shape
    return w.reshape(depth, k, n // tn, tn).transpose(0, 2, 1, 3)


def _matmul_residual(a, wb, l, res, scale, tm=512, name="matmul_res"):
    m, k = a.shape
    nb, tn = wb.shape[1], wb.shape[3]
    tm = _pick(m, tm, 16)
    if isinstance(res, tuple):
        assert res[0].shape[0] % tm == 0 and res[1].shape[0] % tm == 0
        na = res[0].shape[0] // tm
        res_specs = _pair_specs((tm, tn), na, col=lambda i, j: j)
    else:
        na, res = None, (res,)
        res_specs = [pl.BlockSpec((tm, tn), lambda i, j: (i, j))]
    return pl.pallas_call(
        functools.partial(_mm_res_body, scale=scale, na=na),
        grid=(m // tm, nb),
        in_specs=[pl.BlockSpec((tm, k), lambda i, j: (i, 0)),
                  pl.BlockSpec((None, None, k, tn), lambda i, j: (l, j, 0, 0))] + res_specs,
        out_specs=pl.BlockSpec((tm, tn), lambda i, j: (i, j)),
        out_shape=jax.ShapeDtypeStruct((m, nb * tn), F32),
        compiler_params=_params(("parallel", "arbitrary")),
        name=name,
    )(a, wb, *res)


def _mm2_res_body(a1_ref, a2_ref, b1_ref, b2_ref, r_ref, o_ref):
    o_ref[...] = r_ref[...] + (_dot(a1_ref[...], b1_ref[...]) + _dot(a2_ref[...], b2_ref[...]))


def _out_proj(a1, a2, w, l, res, tm=1024, tn=512):
    m, k1 = a1.shape
    k2 = a2.shape[1]
    assert k1 == k2 and w.shape[1] == k1 + k2
    n = w.shape[2]
    tm = _pick(m, tm, 16)
    tn = _pick(n, tn, LANE)
    return pl.pallas_call(
        _mm2_res_body,
        grid=(m // tm, n // tn),
        in_specs=[pl.BlockSpec((tm, k1), lambda i, j: (i, 0)),
                  pl.BlockSpec((tm, k2), lambda i, j: (i, 0)),
                  pl.BlockSpec((None, k1, tn), lambda i, j: (l, 0, j)),
                  pl.BlockSpec((None, k2, tn), lambda i, j: (l, 1, j)),
                  pl.BlockSpec((tm, tn), lambda i, j: (i, j))],
        out_specs=pl.BlockSpec((tm, tn), lambda i, j: (i, j)),
        out_shape=jax.ShapeDtypeStruct((m, n), F32),
        compiler_params=_params(("parallel", "arbitrary")),
        name="mixer_out_proj",
    )(a1, a2, w, w, res)


def _gate_proj_body(a_ref, w_ref, wt_ref, o_ref, ot_ref):
    a = a_ref[...]
    o_ref[...] = _dot(a, w_ref[...])
    ot_ref[...] = lax.dot_general(wt_ref[...], a, NT_DIMS, preferred_element_type=F32)


def _gate_proj(a, w, wt, l, tm=1024):
    m, k = a.shape
    n = w.shape[2]
    nt = wt.shape[1]
    tm = _pick(m, tm, LANE)
    return pl.pallas_call(
        _gate_proj_body,
        grid=(m // tm,),
        in_specs=[pl.BlockSpec((tm, k), lambda i: (i, 0)),
                  pl.BlockSpec((None, k, n), lambda i: (l, 0, 0)),
                  pl.BlockSpec((None, nt, k), lambda i: (l, 0, 0))],
        out_specs=[pl.BlockSpec((tm, n), lambda i: (i, 0)),
                   pl.BlockSpec((nt, tm), lambda i: (0, i))],
        out_shape=[jax.ShapeDtypeStruct((m, n), F32),
                   jax.ShapeDtypeStruct((nt, m), F32)],
        compiler_params=_params(("parallel",)),
        name="gate_proj",
    )(a, w, wt)


def _causal_mask(chunk, reverse):
    r = lax.broadcasted_iota(jnp.int32, (chunk, chunk), 0)
    c = lax.broadcasted_iota(jnp.int32, (chunk, chunk), 1)
    return (c >= r) if reverse else (c <= r)


def _head_norm_gate(h, other, gate, nw):
    h = h + other
    return h * lax.rsqrt(jnp.mean(h * h, axis=-1, keepdims=True) + EPS) * nw * gate


def _gla_body(q_ref, k_ref, v_ref, lr_ref, wlr_ref, blr_ref, *rest, reverse, final, lb):
    if final:
        gg_ref, ob_ref, nw_ref, o_ref, s_ref = rest
    else:
        o_ref, s_ref = rest
    dk = q_ref.shape[-1] // HEADS
    dv = v_ref.shape[-1] // HEADS
    chunk = GLA_CHUNK
    nch = lb // chunk
    qk = HEADS * dk

    @pl.when(pl.program_id(1) == 0)
    def _():
        s_ref[...] = jnp.zeros_like(s_ref)

    lh, ll = _split_bf16(lr_ref[...])
    wh, wl = _split_bf16(wlr_ref[...])
    z = _dot(lh, wh) + _dot(lh, wl) + _dot(ll, wh) + blr_ref[...]
    g = jnp.maximum(_log_sigmoid(z) * (1.0 / GLA_TAU), GLA_MIN_LOG_DECAY)
    mask = _causal_mask(chunk, reverse)
    tri = jnp.where(mask, 1.0, 0.0).astype(BF16)
    gh, gl = _split_bf16(jnp.concatenate([g[c * chunk:(c + 1) * chunk, :] for c in range(nch)], axis=1))
    bcum = _dot(tri, gh) + _dot(tri, gl)

    scale = dk ** -0.5
    zero_blk = jnp.zeros((chunk, chunk), BF16)
    npair = nch // 2
    for pb in (range(npair - 1, -1, -1) if reverse else range(npair)):
        rows = slice(2 * pb * chunk, (2 * pb + 2) * chunk)
        c_first, c_second = (2 * pb + 1, 2 * pb) if reverse else (2 * pb, 2 * pb + 1)
        for hh in range(HEADS):
            kc = slice(hh * dk, (hh + 1) * dk)
            vc = slice(hh * dv, (hh + 1) * dv)

            def chunk_terms(c):
                b = bcum[:, c * qk + hh * dk:c * qk + (hh + 1) * dk]
                tot = b[0:1, :] if reverse else b[chunk - 1:chunk, :]
                r = slice(c * chunk, (c + 1) * chunk)
                q_dec = q_ref[r, kc].astype(F32) * (jnp.exp(b) * scale)
                k_inv = k_ref[r, kc].astype(F32) * jnp.exp(-b)
                return tot, q_dec, k_inv, k_inv * jnp.exp(tot)

            tot1, qd1, ki1, ke1 = chunk_terms(c_first)
            tot2, qd2, ki2, ke2 = chunk_terms(c_second)
            qd1b, qd2b = qd1.astype(BF16), qd2.astype(BF16)
            a11 = lax.dot_general(qd1b, ki1.astype(BF16), NT_DIMS, preferred_element_type=F32)
            a22 = lax.dot_general(qd2b, ki2.astype(BF16), NT_DIMS, preferred_element_type=F32)
            a21 = lax.dot_general(qd2b, ke1.astype(BF16), NT_DIMS, preferred_element_type=F32)
            a11 = jnp.where(mask, a11, 0.0).astype(BF16)
            a22 = jnp.where(mask, a22, 0.0).astype(BF16)
            a21 = a21.astype(BF16)
            qs2 = (qd2 * jnp.exp(tot1)).astype(BF16)
            kx1 = (ke1 * jnp.exp(tot2)).astype(BF16)
            ke2b = ke2.astype(BF16)
            if reverse:
                amat = jnp.concatenate([jnp.concatenate([a22, a21], axis=1),
                                        jnp.concatenate([zero_blk, a11], axis=1)], axis=0)
                qmat = jnp.concatenate([qs2, qd1b], axis=0)
                kmat = jnp.concatenate([ke2b, kx1], axis=0)
            else:
                amat = jnp.concatenate([jnp.concatenate([a11, zero_blk], axis=1),
                                        jnp.concatenate([a21, a22], axis=1)], axis=0)
                qmat = jnp.concatenate([qd1b, qs2], axis=0)
                kmat = jnp.concatenate([kx1, ke2b], axis=0)
            v = v_ref[rows, vc]
            s = s_ref[hh]
            o = _dot(amat, v) + _dot(qmat, s.astype(BF16))
            dec = jnp.exp(jnp.transpose(jnp.broadcast_to(tot1 + tot2, (LANE, dk))))
            dec = jnp.concatenate([dec] * (dv // LANE), axis=1)
            s_ref[hh] = s * dec + lax.dot_general(kmat, v, TN_DIMS, preferred_element_type=F32)
            if final:
                gg = gg_ref[rows, vc].astype(F32)
                o = _head_norm_gate(o, ob_ref[rows, vc], gg * _sigmoid(gg), nw_ref[:, vc])
            o_ref[rows, vc] = o.astype(o_ref.dtype)


def _gla_scan(proj, lr, wlr, blr, l, *, reverse, dk, dv, col_q, col_k, col_v, col_gate=None,
              other=None, norm_w=None):
    bsz, seq, _ = proj.shape
    lb = _pick(seq, 512, LANE)
    assert lb % (2 * GLA_CHUNK) == 0
    nblk = seq // lb
    final = other is not None
    qk, width = HEADS * dk, HEADS * dv

    def tok(i):
        return (nblk - 1 - i) if reverse else i

    def col_spec(w, col0):
        return pl.BlockSpec((None, lb, w), lambda b, i: (b, tok(i), col0 // w))

    d = 1 if reverse else 0
    in_specs = [col_spec(qk, col_q), col_spec(qk, col_k), col_spec(width, col_v),
                pl.BlockSpec((None, lb, LANE), lambda b, i: (b, tok(i), 0)),
                pl.BlockSpec((None, None, LANE, qk), lambda b, i: (l, d, 0, 0)),
                pl.BlockSpec((None, None, 1, qk), lambda b, i: (l, d, 0, 0))]
    args = [proj, proj, proj, lr, wlr, blr]
    if final:
        in_specs += [col_spec(width, col_gate),
                     pl.BlockSpec((None, lb, width), lambda b, i: (b, tok(i), 0)),
                     pl.BlockSpec((None, 1, width), lambda b, i: (l, 0, 0))]
        args += [proj, other, norm_w]
    return pl.pallas_call(
        functools.partial(_gla_body, reverse=reverse, final=final, lb=lb),
        grid=(bsz, nblk),
        in_specs=in_specs,
        out_specs=pl.BlockSpec((None, lb, width), lambda b, i: (b, tok(i), 0)),
        out_shape=jax.ShapeDtypeStruct((bsz, seq, width), BF16 if final else F32),
        scratch_shapes=[pltpu.VMEM((HEADS, dk, dv), F32)],
        compiler_params=_params(("parallel", "arbitrary")),
        name="gla_fwd" if final else "gla_bwd",
    )(*args)


def _segment_cumsum_lanes(x, seg, reverse):
    n = x.shape[-1]
    pos = lax.broadcasted_iota(jnp.int32, x.shape, x.ndim - 1) % seg
    s = 1
    while s < seg:
        if reverse:
            x = x + jnp.where(pos < seg - s, pltpu.roll(x, n - s, x.ndim - 1), 0.0)
        else:
            x = x + jnp.where(pos >= s, pltpu.roll(x, s, x.ndim - 1), 0.0)
        s *= 2
    return x


def _mlstm_body(q_ref, k_ref, v_ref, gr_ref, br_ref, *rest, reverse, final, lb):
    if final:
        mo_ref, hb_ref, nw_ref, o_ref, c_ref, m_ref = rest
    else:
        o_ref, c_ref, m_ref = rest
    dk = q_ref.shape[-1] // HEADS
    dv = v_ref.shape[-1] // HEADS
    chunk = min(MLSTM_CHUNK, lb)

    @pl.when(pl.program_id(1) == 0)
    def _():
        c_ref[...] = jnp.zeros_like(c_ref)
        m_ref[...] = jnp.full_like(m_ref, NEG_BIG)

    ti, tf = (2, 3) if reverse else (0, 1)
    gr = gr_ref[...] + br_ref[:, 0:1]
    bcum = _segment_cumsum_lanes(_log_sigmoid(gr), chunk, reverse)
    row_id = lax.broadcasted_iota(jnp.int32, gr.shape, 0) % SUBLANE
    rowform = jnp.where(row_id == tf, bcum, gr)
    pad_rows = jnp.zeros((LANE - SUBLANE, chunk), F32)

    mask = _causal_mask(chunk, reverse)
    ones_col = jnp.where(lax.broadcasted_iota(jnp.int32, (chunk, LANE), 1) == 0, 1.0, 0.0).astype(BF16)
    scale = dk ** -0.5
    nch = lb // chunk
    for ch in (range(nch - 1, -1, -1) if reverse else range(nch)):
        rows = slice(ch * chunk, (ch + 1) * chunk)
        for hh in range(HEADS):
            kc = slice(hh * dk, (hh + 1) * dk)
            vc = slice(hh * dv, (hh + 1) * dv)
            rf = rowform[hh * SUBLANE:(hh + 1) * SUBLANE, rows]
            u_row = rf[ti:ti + 1, :] - rf[tf:tf + 1, :]
            colform = jnp.transpose(jnp.concatenate([rf, pad_rows], axis=0))
            bc = colform[:, tf:tf + 1]
            ic = colform[:, ti:ti + 1]
            gtot = bc[0:1, :] if reverse else bc[chunk - 1:chunk, :]
            log_d = jnp.where(mask, bc + u_row, NEG_BIG)
            m_intra = jnp.max(log_d, axis=1, keepdims=True)
            q = q_ref[rows, kc]
            k = k_ref[rows, kc]
            v_aug = jnp.concatenate([v_ref[rows, vc], ones_col], axis=1)
            s = lax.dot_general(q, k, NT_DIMS, preferred_element_type=F32) * jnp.exp(log_d - m_intra)
            intra = _dot(s.astype(BF16), v_aug)
            c = c_ref[hh]
            inter = _dot(q, c.astype(BF16))
            m_prev = m_ref[hh, 0:1, 0:1]
            bm = bc + m_prev
            m_j = jnp.maximum(bm, m_intra)
            comb = (scale * jnp.exp(bm - m_j)) * inter + (scale * jnp.exp(m_intra - m_j)) * intra
            denom = jnp.maximum(jnp.abs(comb[:, dv:dv + 1]), jnp.exp(-m_j))
            h = comb[:, :dv] * (1.0 / denom)
            a_col = gtot - bc + ic
            m_new = jnp.maximum(gtot + m_prev, jnp.max(a_col, axis=0, keepdims=True))
            kw = (k.astype(F32) * jnp.exp(a_col - m_new)).astype(BF16)
            c_ref[hh] = (jnp.exp(gtot + m_prev - m_new) * c
                         + lax.dot_general(kw, v_aug, TN_DIMS, preferred_element_type=F32))
            m_ref[hh] = jnp.broadcast_to(m_new, (SUBLANE, LANE))
            if final:
                h = _head_norm_gate(h, hb_ref[rows, vc], _sigmoid(mo_ref[rows, vc].astype(F32)), nw_ref[:, vc])
            o_ref[rows, vc] = h.astype(o_ref.dtype)


def _mlstm_scan(proj, gates_t, bias_r, l, *, reverse, dk, dv, col_q, col_k, col_v,
                col_gate=None, other=None, norm_w=None):
    bsz, seq, _ = proj.shape
    lb = _pick(seq, 512, LANE)
    nblk = seq // lb
    final = other is not None
    qk, width = HEADS * dk, HEADS * dv

    def tok(i):
        return (nblk - 1 - i) if reverse else i

    def col_spec(w, col0):
        return pl.BlockSpec((None, lb, w), lambda b, i: (b, tok(i), col0 // w))

    in_specs = [col_spec(qk, col_q), col_spec(qk, col_k), col_spec(width, col_v),
                pl.BlockSpec((HEADS * SUBLANE, lb), lambda b, i: (0, b * nblk + tok(i))),
                pl.BlockSpec((None, HEADS * SUBLANE, LANE), lambda b, i: (l, 0, 0))]
    args = [proj, proj, proj, gates_t, bias_r]
    if final:
        in_specs += [col_spec(width, col_gate),
                     pl.BlockSpec((None, lb, width), lambda b, i: (b, tok(i), 0)),
                     pl.BlockSpec((None, 1, width), lambda b, i: (l, 0, 0))]
        args += [proj, other, norm_w]
    return pl.pallas_call(
        functools.partial(_mlstm_body, reverse=reverse, final=final, lb=lb),
        grid=(bsz, nblk),
        in_specs=in_specs,
        out_specs=pl.BlockSpec((None, lb, width), lambda b, i: (b, tok(i), 0)),
        out_shape=jax.ShapeDtypeStruct((bsz, seq, width), BF16 if final else F32),
        scratch_shapes=[pltpu.VMEM((HEADS, dk, dv + LANE), F32), pltpu.VMEM((HEADS, SUBLANE, LANE), F32)],
        compiler_params=_params(("parallel", "arbitrary")),
        name="mlstm_fwd" if final else "mlstm_bwd",
    )(*args)


def _xattn_body(x_ref, nw_ref, wq_ref, kv_ref, wo_ref, nw2_ref, o_ref, n_ref, *, heads):
    x = x_ref[...]
    xn = (x * lax.rsqrt(jnp.mean(x * x, axis=-1, keepdims=True) + EPS) * nw_ref[...]).astype(BF16)
    width = wq_ref.shape[1]
    hd = width // heads
    q = (_dot(xn, wq_ref[...]) * (hd ** -0.5)).astype(BF16)
    outs = []
    for h in range(heads):
        kh = kv_ref[:, h * hd:(h + 1) * hd]
        vh = kv_ref[:, width + h * hd:width + (h + 1) * hd]
        s = lax.dot_general(q[:, h * hd:(h + 1) * hd], kh, NT_DIMS, preferred_element_type=F32)
        p = jnp.exp(s - jnp.max(s, axis=-1, keepdims=True))
        p = p * (1.0 / jnp.sum(p, axis=-1, keepdims=True))
        outs.append(_dot(p.astype(BF16), vh))
    o = jnp.concatenate(outs, axis=1).astype(BF16)
    y = x + _dot(o, wo_ref[...])
    o_ref[...] = y
    n_ref[...] = (y * lax.rsqrt(jnp.mean(y * y, axis=-1, keepdims=True) + EPS) * nw2_ref[...]).astype(BF16)


def _xattn(x, norm_w, wq, kv, wo, next_norm_w, l, tq=256):
    bsz, seq, d = x.shape
    mem = kv.shape[1]
    width = wq.shape[2]
    tq = _pick(seq, tq, 16)
    return pl.pallas_call(
        functools.partial(_xattn_body, heads=XATTN_HEADS),
        grid=(bsz, seq // tq),
        in_specs=[pl.BlockSpec((None, tq, d), lambda b, i: (b, i, 0)),
                  pl.BlockSpec((None, 1, d), lambda b, i: (l, 0, 0)),
                  pl.BlockSpec((None, d, width), lambda b, i: (l, 0, 0)),
                  pl.BlockSpec((None, mem, 2 * width), lambda b, i: (b, 0, 0)),
                  pl.BlockSpec((None, width, d), lambda b, i: (l, 0, 0)),
                  pl.BlockSpec((None, 1, d), lambda b, i: (l, 0, 0))],
        out_specs=[pl.BlockSpec((None, tq, d), lambda b, i: (b, i, 0)),
                   pl.BlockSpec((None, tq, d), lambda b, i: (b, i, 0))],
        out_shape=[jax.ShapeDtypeStruct((bsz, seq, d), F32),
                   jax.ShapeDtypeStruct((bsz, seq, d), BF16)],
        compiler_params=_params(("parallel", "arbitrary")),
        name="mem_xattn",
    )(x, norm_w, wq, kv, wo, next_norm_w)


def _norm_w(w):
    return w.astype(F32).reshape(w.shape[0], 1, w.shape[1])


FFN_DOWN_TN = 512


def _ffn_weights(norm_w, w_gate, w_up, w_down):
    return (_norm_w(norm_w), w_gate.astype(BF16), w_up.astype(BF16),
            _block_cols(w_down.astype(BF16), _pick(w_down.shape[2], FFN_DOWN_TN, LANE)))


def _ffn(x, norm_w, w_gate, w_up, w_down_blocks, l, xn=None):
    if xn is None:
        xn = _rmsnorm_pair(*x, norm_w, l, BF16) if isinstance(x, tuple) else _rmsnorm(x, norm_w, l, BF16)
    hid = _swiglu_up(xn, w_gate, w_up, l)
    return _matmul_residual(hid, w_down_blocks, l, x, 0.5, name="ffn_down")


def _mixer_weights(d, w_in, gla_w_lr, gla_b_lr, mlstm_gate_b):
    depth = w_in.shape[0]
    half = d // 2
    qk = half // 2
    o_lr = 2 * qk + 2 * half
    o_m = o_lr + 2 * GLA_RANK
    o_mg = o_m + 2 * qk + 2 * half
    w_big = jnp.concatenate([w_in[:, :, :o_lr], w_in[:, :, o_m:o_mg]], axis=2).astype(BF16)
    w_lr = jnp.pad(w_in[:, :, o_lr:o_m], ((0, 0), (0, 0), (0, LANE - 2 * GLA_RANK))).astype(BF16)
    mg = w_in[:, :, o_mg:o_mg + 4 * HEADS].reshape(depth, d, 4, HEADS).transpose(0, 3, 2, 1)
    w_gates_t = jnp.pad(mg, ((0, 0), (0, 0), (0, SUBLANE - 4), (0, 0))).reshape(depth, HEADS * SUBLANE, d)
    w_gates_t = w_gates_t.astype(BF16)
    bias_r = jnp.pad(mlstm_gate_b.astype(F32).transpose(0, 2, 1), ((0, 0), (0, 0), (0, SUBLANE - 4)))
    bias_r = jnp.broadcast_to(bias_r.reshape(depth, HEADS * SUBLANE, 1), (depth, HEADS * SUBLANE, LANE))
    wlr = jnp.stack([jnp.pad(gla_w_lr[:, 0], ((0, 0), (0, LANE - GLA_RANK), (0, 0))),
                     jnp.pad(gla_w_lr[:, 1], ((0, 0), (GLA_RANK, LANE - 2 * GLA_RANK), (0, 0)))],
                    axis=1).astype(F32)
    blr = gla_b_lr.astype(F32).reshape(depth, 2, 1, qk)
    return w_big, w_lr, w_gates_t, bias_r, wlr, blr


def _mixer(x, bsz, seq, l, norm_w, w_big, w_lr, w_gates_t, bias_r, wlr, blr, gla_norm, mlstm_norm, w_out):
    m, d = x.shape
    half = d // 2
    dv = half // HEADS
    dk = dv // 2
    qk = HEADS * dk
    gla_w = 2 * qk + 2 * half
    xn = _rmsnorm(x, norm_w, l, BF16)
    proj = _matmul(xn, w_big, l, BF16, name="mixer_in_proj").reshape(bsz, seq, -1)
    lr, gates_t = _gate_proj(xn, w_lr, w_gates_t, l)
    lr = lr.reshape(bsz, seq, LANE)

    gla_kw = dict(dk=dk, dv=dv, col_q=0, col_k=qk, col_v=2 * qk)
    g_b = _gla_scan(proj, lr, wlr, blr, l, reverse=True, **gla_kw)
    g_out = _gla_scan(proj, lr, wlr, blr, l, reverse=False, col_gate=2 * qk + half, other=g_b,
                      norm_w=gla_norm, **gla_kw)
    ml_kw = dict(dk=dk, dv=dv, col_q=gla_w, col_k=gla_w + qk, col_v=gla_w + 2 * qk)
    m_b = _mlstm_scan(proj, gates_t, bias_r, l, reverse=True, **ml_kw)
    m_out = _mlstm_scan(proj, gates_t, bias_r, l, reverse=False, col_gate=gla_w + 2 * qk + half,
                        other=m_b, norm_w=mlstm_norm, **ml_kw)
    return _out_proj(g_out.reshape(m, half), m_out.reshape(m, half), w_out, l, x)


def kernel(x_prompt, x_sample, mem_prompt, mem_sample, ffn1_norm, ffn1_w_gate, ffn1_w_up, ffn1_w_down, mix_norm, w_in, gla_w_lr, gla_b_lr, gla_out_norm, mlstm_gate_b, mlstm_out_norm, w_out, xattn_norm, mem_norm, xattn_wq, xattn_wk, xattn_wv, xattn_wo, ffn2_norm, ffn2_w_gate, ffn2_w_up, ffn2_w_down, final_norm):
    assert x_prompt.shape[1:] == x_sample.shape[1:] and mem_prompt.shape[1:] == mem_sample.shape[1:]
    n_p, seq, d = x_prompt.shape
    n_s = x_sample.shape[0]
    bsz = n_p + n_s
    m = bsz * seq
    x = (x_prompt.reshape(n_p * seq, d), x_sample.reshape(n_s * seq, d))
    mem = jnp.concatenate([mem_prompt, mem_sample], axis=0)
    n_mem = mem.shape[1]
    mem = mem.reshape(bsz * n_mem, d)
    depth = ffn1_norm.shape[0]

    ffn1 = _ffn_weights(ffn1_norm, ffn1_w_gate, ffn1_w_up, ffn1_w_down)
    ffn2 = _ffn_weights(ffn2_norm, ffn2_w_gate, ffn2_w_up, ffn2_w_down)
    mixer_w = _mixer_weights(d, w_in, gla_w_lr, gla_b_lr, mlstm_gate_b)
    mixer_rest = (_norm_w(gla_out_norm), _norm_w(mlstm_out_norm), w_out.astype(BF16))
    w_kv = jnp.concatenate([xattn_wk, xattn_wv], axis=2).astype(BF16)
    wq = xattn_wq.astype(BF16)
    wo = xattn_wo.astype(BF16)
    mix_n, xattn_n, mem_n = _norm_w(mix_norm), _norm_w(xattn_norm), _norm_w(mem_norm)

    for l in range(depth):
        x = _ffn(x, *ffn1, l)
        x = _mixer(x, bsz, seq, l, mix_n, *mixer_w, *mixer_rest)
        kv = _matmul(_rmsnorm(mem, mem_n, l, BF16), w_kv, l, BF16, tm=512, name="mem_kv_proj")
        x, xn = _xattn(x.reshape(bsz, seq, d), xattn_n, wq, kv.reshape(bsz, n_mem, -1), wo, ffn2[0], l)
        x = _ffn(x.reshape(m, d), *ffn2, l, xn=xn.reshape(m, d))
    fin = _norm_w(final_norm.reshape(1, d))
    y_p = _rmsnorm(x, fin, 0, F32, 0, n_p * seq).reshape(n_p, seq, d)
    y_s = _rmsnorm(x, fin, 0, F32, n_p * seq, n_s * seq).reshape(n_s, seq, d)
    return (y_p, y_s)
```

```python
import functools

import jax
import jax.numpy as jnp
from jax import lax
from jax.experimental import pallas as pl
from jax.experimental.pallas import tpu as pltpu

F32 = jnp.float32
BF16 = jnp.bfloat16

EPS = 1e-6
NEG_BIG = -1e30
HEADS = 4
GLA_RANK = 16
GLA_TAU = 16.0
GLA_MIN_LOG_DECAY = -1.0
GLA_CHUNK = 64
MLSTM_CHUNK = 256
XATTN_HEADS = 4
LANE = 128
SUBLANE = 8
VMEM_LIMIT = 56 * 1024 * 1024

NT_DIMS = (((1,), (1,)), ((), ()))
TN_DIMS = (((0,), (0,)), ((), ()))


def _pick(n, target, mult):
    if n <= target:
        return n
    t = (target // mult) * mult
    while t >= mult:
        if n % t == 0:
            return t
        t -= mult
    return n


def _params(sem):
    return pltpu.CompilerParams(dimension_semantics=sem, vmem_limit_bytes=VMEM_LIMIT)


def _log_sigmoid(x):
    return jnp.minimum(x, 0.0) - jnp.log(1.0 + jnp.exp(-jnp.abs(x)))


def _sigmoid(x):
    return 0.5 * jnp.tanh(0.5 * x) + 0.5


def _split_bf16(x):
    hi = x.astype(BF16)
    return hi, (x - hi.astype(F32)).astype(BF16)


def _dot(a, b):
    return jnp.dot(a, b, preferred_element_type=F32)


def _rmsnorm_body(x_ref, w_ref, o_ref):
    x = x_ref[...]
    ms = jnp.mean(x * x, axis=-1, keepdims=True)
    o_ref[...] = (x * lax.rsqrt(ms + EPS) * w_ref[...]).astype(o_ref.dtype)


def _pair_specs(block, na, col=lambda *g: 0):
    return [pl.BlockSpec(block, lambda *g: (jnp.minimum(g[0], na - 1), col(*g))),
            pl.BlockSpec(block, lambda *g: (jnp.maximum(g[0] - na, 0), col(*g)))]


def _rmsnorm_pair_body(xa_ref, xb_ref, w_ref, o_ref, *, na):
    @pl.when(pl.program_id(0) < na)
    def _():
        _rmsnorm_body(xa_ref, w_ref, o_ref)

    @pl.when(pl.program_id(0) >= na)
    def _():
        _rmsnorm_body(xb_ref, w_ref, o_ref)


def _rmsnorm_pair(xa, xb, w, l, out_dtype):
    d = xa.shape[1]
    tr = _pick(xa.shape[0], 256, SUBLANE * 2)
    assert xa.shape[0] % tr == 0 and xb.shape[0] % tr == 0
    na, nb = xa.shape[0] // tr, xb.shape[0] // tr
    return pl.pallas_call(
        functools.partial(_rmsnorm_pair_body, na=na),
        grid=(na + nb,),
        in_specs=_pair_specs((tr, d), na) + [pl.BlockSpec((None, 1, d), lambda i: (l, 0, 0))],
        out_specs=pl.BlockSpec((tr, d), lambda i: (i, 0)),
        out_shape=jax.ShapeDtypeStruct(((na + nb) * tr, d), out_dtype),
        compiler_params=_params(("parallel",)),
        name="rmsnorm",
    )(xa, xb, w)


def _rmsnorm(x, w, l, out_dtype, row_start=0, rows=None):
    m, d = x.shape
    rows = m if rows is None else rows
    tr = _pick(rows, 256, SUBLANE * 2)
    assert row_start % tr == 0
    off = row_start // tr
    return pl.pallas_call(
        _rmsnorm_body,
        grid=(rows // tr,),
        in_specs=[pl.BlockSpec((tr, d), lambda i: (i + off, 0)),
                  pl.BlockSpec((None, 1, d), lambda i: (l, 0, 0))],
        out_specs=pl.BlockSpec((tr, d), lambda i: (i, 0)),
        out_shape=jax.ShapeDtypeStruct((rows, d), out_dtype),
        compiler_params=_params(("parallel",)),
        name="rmsnorm",
    )(x, w)


def _row_scale(rs_ref, width):
    return jnp.concatenate([rs_ref[...]] * (width // LANE), axis=1)


def _split_act(a):
    return a if isinstance(a, tuple) else (a, None)


def _ss_spec(tm, ss):
    return [] if ss is None else [pl.BlockSpec((tm, LANE), lambda i, *_: (i, 0))]


def _mm_body(a_ref, b_ref, *rest):
    *ss_ref, o_ref = rest
    p = _dot(a_ref[...], b_ref[...])
    if ss_ref:
        p = p * _row_scale(ss_ref[0], p.shape[1])
    o_ref[...] = p.astype(o_ref.dtype)


def _matmul(a, w, l, out_dtype, tm=1024, tn=1024, name="matmul"):
    a, ss = _split_act(a)
    m, k = a.shape
    n = w.shape[2]
    tm = _pick(m, tm, 16)
    tn = _pick(n, tn, LANE)
    return pl.pallas_call(
        _mm_body,
        grid=(m // tm, n // tn),
        in_specs=[pl.BlockSpec((tm, k), lambda i, j: (i, 0)),
                  pl.BlockSpec((None, k, tn), lambda i, j: (l, 0, j))] + _ss_spec(tm, ss),
        out_specs=pl.BlockSpec((tm, tn), lambda i, j: (i, j)),
        out_shape=jax.ShapeDtypeStruct((m, n), out_dtype),
        compiler_params=_params(("parallel", "arbitrary")),
        name=name,
    )(a, w, *([] if ss is None else [ss]))


def _swiglu_body(a_ref, wg_ref, wu_ref, *rest):
    *ss_ref, o_ref = rest
    a = a_ref[...]
    g = _dot(a, wg_ref[...])
    u = _dot(a, wu_ref[...])
    if ss_ref:
        rs = _row_scale(ss_ref[0], g.shape[1])
        g, u = g * rs, u * rs
    o_ref[...] = (g * _sigmoid(g) * u).astype(o_ref.dtype)


def _swiglu_up(a, wg, wu, l, tm=2048, tn=256):
    a, ss = _split_act(a)
    m, k = a.shape
    n = wg.shape[2]
    tm = _pick(m, tm, 16)
    tn = min(tn, n)
    return pl.pallas_call(
        _swiglu_body,
        grid=(m // tm, pl.cdiv(n, tn)),
        in_specs=[pl.BlockSpec((tm, k), lambda i, j: (i, 0)),
                  pl.BlockSpec((None, k, tn), lambda i, j: (l, 0, j)),
                  pl.BlockSpec((None, k, tn), lambda i, j: (l, 0, j))] + _ss_spec(tm, ss),
        out_specs=pl.BlockSpec((tm, tn), lambda i, j: (i, j)),
        out_shape=jax.ShapeDtypeStruct((m, n), BF16),
        compiler_params=_params(("parallel", "arbitrary")),
        name="swiglu_up",
    )(a, wg, wu, *([] if ss is None else [ss]))


def _mm_res_body(a_ref, b_ref, *rest, scale, na, fuse, width):
    n_res = 1 if na is None else 2
    res_refs, rest = rest[:n_res], rest[n_res:]
    if na is None:
        r = res_refs[0][...]
    else:
        r = jnp.where(pl.program_id(0) < na, res_refs[0][...], res_refs[1][...])
    y = r + scale * _dot(a_ref[...], b_ref[...])
    if not fuse:
        rest[0][...] = y
        return
    gain_ref, o_ref, yw_ref, ss_ref = rest
    o_ref[...] = y
    yw_ref[...] = (y * gain_ref[...]).astype(yw_ref.dtype)
    sq = y * y
    part = sq[:, 0:LANE]
    for t in range(1, sq.shape[1] // LANE):
        part = part + sq[:, t * LANE:(t + 1) * LANE]

    @pl.when(pl.program_id(1) == 0)
    def _():
        ss_ref[...] = part

    @pl.when(pl.program_id(1) > 0)
    def _():
        ss_ref[...] += part

    @pl.when(pl.program_id(1) == pl.num_programs(1) - 1)
    def _():
        tot = jnp.sum(ss_ref[...], axis=1, keepdims=True)
        ss_ref[...] = jnp.broadcast_to(lax.rsqrt(tot * (1.0 / width) + EPS), ss_ref.shape)


def _matmul_residual(a, w, l, res, scale, tm=512, tn=512, next_gain=None, name="matmul_res"):
    m, k = a.shape
    n = w.shape[2]
    tm = _pick(m, tm, 16)
    tn = _pick(n, tn, LANE)
    if isinstance(res, tuple):
        assert res[0].shape[0] % tm == 0 and res[1].shape[0] % tm == 0
        na = res[0].shape[0] // tm
        res_specs = _pair_specs((tm, tn), na, col=lambda i, j: j)
    else:
        na, res = None, (res,)
        res_specs = [pl.BlockSpec((tm, tn), lambda i, j: (i, j))]
    in_specs = [pl.BlockSpec((tm, k), lambda i, j: (i, 0)),
                pl.BlockSpec((None, k, tn), lambda i, j: (l, 0, j))] + res_specs
    out_specs = pl.BlockSpec((tm, tn), lambda i, j: (i, j))
    out_shape = jax.ShapeDtypeStruct((m, n), F32)
    args = [a, w, *res]
    fuse = next_gain is not None
    if fuse:
        gains, gl = next_gain
        in_specs.append(pl.BlockSpec((None, 1, tn), lambda i, j: (gl, 0, j)))
        args.append(gains)
        out_specs = [out_specs, pl.BlockSpec((tm, tn), lambda i, j: (i, j)),
                     pl.BlockSpec((tm, LANE), lambda i, j: (i, 0))]
        out_shape = [out_shape, jax.ShapeDtypeStruct((m, n), BF16), jax.ShapeDtypeStruct((m, LANE), F32)]
    out = pl.pallas_call(
        functools.partial(_mm_res_body, scale=scale, na=na, fuse=fuse, width=n),
        grid=(m // tm, n // tn),
        in_specs=in_specs,
        out_specs=out_specs,
        out_shape=out_shape,
        compiler_params=_params(("parallel", "arbitrary")),
        name=name,
    )(*args)
    return (out[0], (out[1], out[2])) if fuse else out


def _mm2_res_body(a1_ref, a2_ref, b1_ref, b2_ref, r_ref, o_ref):
    o_ref[...] = r_ref[...] + (_dot(a1_ref[...], b1_ref[...]) + _dot(a2_ref[...], b2_ref[...]))


def _out_proj(a1, a2, w, l, res, tm=1024, tn=512):
    m, k1 = a1.shape
    k2 = a2.shape[1]
    assert k1 == k2 and w.shape[1] == k1 + k2
    n = w.shape[2]
    tm = _pick(m, tm, 16)
    tn = _pick(n, tn, LANE)
    return pl.pallas_call(
        _mm2_res_body,
        grid=(m // tm, n // tn),
        in_specs=[pl.BlockSpec((tm, k1), lambda i, j: (i, 0)),
                  pl.BlockSpec((tm, k2), lambda i, j: (i, 0)),
                  pl.BlockSpec((None, k1, tn), lambda i, j: (l, 0, j)),
                  pl.BlockSpec((None, k2, tn), lambda i, j: (l, 1, j)),
                  pl.BlockSpec((tm, tn), lambda i, j: (i, j))],
        out_specs=pl.BlockSpec((tm, tn), lambda i, j: (i, j)),
        out_shape=jax.ShapeDtypeStruct((m, n), F32),
        compiler_params=_params(("parallel", "arbitrary")),
        name="mixer_out_proj",
    )(a1, a2, w, w, res)


def _gate_proj_body(a_ref, w_ref, wt_ref, *rest):
    *ss_ref, o_ref, ot_ref = rest
    a = a_ref[...]
    o = _dot(a, w_ref[...])
    ot = lax.dot_general(wt_ref[...], a, NT_DIMS, preferred_element_type=F32)
    if ss_ref:
        o = o * _row_scale(ss_ref[0], o.shape[1])
        ot = ot * jnp.transpose(ss_ref[0][...])[0:1, :]
    o_ref[...] = o
    ot_ref[...] = ot


def _gate_proj(a, w, wt, l, tm=1024):
    a, ss = _split_act(a)
    m, k = a.shape
    n = w.shape[2]
    nt = wt.shape[1]
    tm = _pick(m, tm, LANE)
    return pl.pallas_call(
        _gate_proj_body,
        grid=(m // tm,),
        in_specs=[pl.BlockSpec((tm, k), lambda i: (i, 0)),
                  pl.BlockSpec((None, k, n), lambda i: (l, 0, 0)),
                  pl.BlockSpec((None, nt, k), lambda i: (l, 0, 0))] + _ss_spec(tm, ss),
        out_specs=[pl.BlockSpec((tm, n), lambda i: (i, 0)),
                   pl.BlockSpec((nt, tm), lambda i: (0, i))],
        out_shape=[jax.ShapeDtypeStruct((m, n), F32),
                   jax.ShapeDtypeStruct((nt, m), F32)],
        compiler_params=_params(("parallel",)),
        name="gate_proj",
    )(a, w, wt, *([] if ss is None else [ss]))


def _causal_mask(chunk, reverse):
    r = lax.broadcasted_iota(jnp.int32, (chunk, chunk), 0)
    c = lax.broadcasted_iota(jnp.int32, (chunk, chunk), 1)
    return (c >= r) if reverse else (c <= r)


def _head_norm_gate(h, other, gate, nw):
    h = h + other
    return h * lax.rsqrt(jnp.mean(h * h, axis=-1, keepdims=True) + EPS) * nw * gate


def _gla_body(q_ref, k_ref, v_ref, lr_ref, wlr_ref, blr_ref, *rest, reverse, final, lb):
    if final:
        gg_ref, ob_ref, nw_ref, o_ref, s_ref = rest
    else:
        o_ref, s_ref = rest
    dk = q_ref.shape[-1] // HEADS
    dv = v_ref.shape[-1] // HEADS
    chunk = GLA_CHUNK
    nch = lb // chunk
    qk = HEADS * dk

    @pl.when(pl.program_id(1) == 0)
    def _():
        s_ref[...] = jnp.zeros_like(s_ref)

    lh, ll = _split_bf16(lr_ref[...])
    wh, wl = _split_bf16(wlr_ref[...])
    z = _dot(lh, wh) + _dot(lh, wl) + _dot(ll, wh) + blr_ref[...]
    g = jnp.maximum(_log_sigmoid(z) * (1.0 / GLA_TAU), GLA_MIN_LOG_DECAY)
    mask = _causal_mask(chunk, reverse)
    tri = jnp.where(mask, 1.0, 0.0).astype(BF16)
    gh, gl = _split_bf16(jnp.concatenate([g[c * chunk:(c + 1) * chunk, :] for c in range(nch)], axis=1))
    bcum = _dot(tri, gh) + _dot(tri, gl)

    scale = dk ** -0.5
    zero_blk = jnp.zeros((chunk, chunk), BF16)
    npair = nch // 2
    for pb in (range(npair - 1, -1, -1) if reverse else range(npair)):
        rows = slice(2 * pb * chunk, (2 * pb + 2) * chunk)
        c_first, c_second = (2 * pb + 1, 2 * pb) if reverse else (2 * pb, 2 * pb + 1)
        for hh in range(HEADS):
            kc = slice(hh * dk, (hh + 1) * dk)
            vc = slice(hh * dv, (hh + 1) * dv)

            def chunk_terms(c):
                b = bcum[:, c * qk + hh * dk:c * qk + (hh + 1) * dk]
                tot = b[0:1, :] if reverse else b[chunk - 1:chunk, :]
                r = slice(c * chunk, (c + 1) * chunk)
                q_dec = q_ref[r, kc].astype(F32) * (jnp.exp(b) * scale)
                k_inv = k_ref[r, kc].astype(F32) * jnp.exp(-b)
                return tot, q_dec, k_inv, k_inv * jnp.exp(tot)

            tot1, qd1, ki1, ke1 = chunk_terms(c_first)
            tot2, qd2, ki2, ke2 = chunk_terms(c_second)
            qd1b, qd2b = qd1.astype(BF16), qd2.astype(BF16)
            a11 = lax.dot_general(qd1b, ki1.astype(BF16), NT_DIMS, preferred_element_type=F32)
            a22 = lax.dot_general(qd2b, ki2.astype(BF16), NT_DIMS, preferred_element_type=F32)
            a21 = lax.dot_general(qd2b, ke1.astype(BF16), NT_DIMS, preferred_element_type=F32)
            a11 = jnp.where(mask, a11, 0.0).astype(BF16)
            a22 = jnp.where(mask, a22, 0.0).astype(BF16)
            a21 = a21.astype(BF16)
            qs2 = (qd2 * jnp.exp(tot1)).astype(BF16)
            kx1 = (ke1 * jnp.exp(tot2)).astype(BF16)
            ke2b = ke2.astype(BF16)
            if reverse:
                amat = jnp.concatenate([jnp.concatenate([a22, a21], axis=1),
                                        jnp.concatenate([zero_blk, a11], axis=1)], axis=0)
                qmat = jnp.concatenate([qs2, qd1b], axis=0)
                kmat = jnp.concatenate([ke2b, kx1], axis=0)
            else:
                amat = jnp.concatenate([jnp.concatenate([a11, zero_blk], axis=1),
                                        jnp.concatenate([a21, a22], axis=1)], axis=0)
                qmat = jnp.concatenate([qd1b, qs2], axis=0)
                kmat = jnp.concatenate([kx1, ke2b], axis=0)
            v = v_ref[rows, vc]
            s = s_ref[hh]
            o = _dot(amat, v) + _dot(qmat, s.astype(BF16))
            dec = jnp.exp(jnp.transpose(jnp.broadcast_to(tot1 + tot2, (LANE, dk))))
            dec = jnp.concatenate([dec] * (dv // LANE), axis=1)
            s_ref[hh] = s * dec + lax.dot_general(kmat, v, TN_DIMS, preferred_element_type=F32)
            if final:
                gg = gg_ref[rows, vc].astype(F32)
                o = _head_norm_gate(o, ob_ref[rows, vc], gg * _sigmoid(gg), nw_ref[:, vc])
            o_ref[rows, vc] = o.astype(o_ref.dtype)


def _gla_scan(proj, lr, wlr, blr, l, *, reverse, dk, dv, col_q, col_k, col_v, col_gate=None,
              other=None, norm_w=None):
    bsz, seq, _ = proj.shape
    lb = _pick(seq, 512, LANE)
    assert lb % (2 * GLA_CHUNK) == 0
    nblk = seq // lb
    final = other is not None
    qk, width = HEADS * dk, HEADS * dv

    def tok(i):
        return (nblk - 1 - i) if reverse else i

    def col_spec(w, col0):
        return pl.BlockSpec((None, lb, w), lambda b, i: (b, tok(i), col0 // w))

    d = 1 if reverse else 0
    in_specs = [col_spec(qk, col_q), col_spec(qk, col_k), col_spec(width, col_v),
                pl.BlockSpec((None, lb, LANE), lambda b, i: (b, tok(i), 0)),
                pl.BlockSpec((None, None, LANE, qk), lambda b, i: (l, d, 0, 0)),
                pl.BlockSpec((None, None, 1, qk), lambda b, i: (l, d, 0, 0))]
    args = [proj, proj, proj, lr, wlr, blr]
    if final:
        in_specs += [col_spec(width, col_gate),
                     pl.BlockSpec((None, lb, width), lambda b, i: (b, tok(i), 0)),
                     pl.BlockSpec((None, 1, width), lambda b, i: (l, 0, 0))]
        args += [proj, other, norm_w]
    return pl.pallas_call(
        functools.partial(_gla_body, reverse=reverse, final=final, lb=lb),
        grid=(bsz, nblk),
        in_specs=in_specs,
        out_specs=pl.BlockSpec((None, lb, width), lambda b, i: (b, tok(i), 0)),
        out_shape=jax.ShapeDtypeStruct((bsz, seq, width), BF16 if final else F32),
        scratch_shapes=[pltpu.VMEM((HEADS, dk, dv), F32)],
        compiler_params=_params(("parallel", "arbitrary")),
        name="gla_fwd" if final else "gla_bwd",
    )(*args)


def _segment_cumsum_lanes(x, seg, reverse):
    n = x.shape[-1]
    pos = lax.broadcasted_iota(jnp.int32, x.shape, x.ndim - 1) % seg
    s = 1
    while s < seg:
        if reverse:
            x = x + jnp.where(pos < seg - s, pltpu.roll(x, n - s, x.ndim - 1), 0.0)
        else:
            x = x + jnp.where(pos >= s, pltpu.roll(x, s, x.ndim - 1), 0.0)
        s *= 2
    return x


def _mlstm_body(q_ref, k_ref, v_ref, gr_ref, br_ref, *rest, reverse, final, lb):
    if final:
        mo_ref, hb_ref, nw_ref, o_ref, c_ref, m_ref = rest
    else:
        o_ref, c_ref, m_ref = rest
    dk = q_ref.shape[-1] // HEADS
    dv = v_ref.shape[-1] // HEADS
    chunk = min(MLSTM_CHUNK, lb)

    @pl.when(pl.program_id(1) == 0)
    def _():
        c_ref[...] = jnp.zeros_like(c_ref)
        m_ref[...] = jnp.full_like(m_ref, NEG_BIG)

    ti, tf = (2, 3) if reverse else (0, 1)
    gr = gr_ref[...] + br_ref[:, 0:1]
    bcum = _segment_cumsum_lanes(_log_sigmoid(gr), chunk, reverse)
    row_id = lax.broadcasted_iota(jnp.int32, gr.shape, 0) % SUBLANE
    rowform = jnp.where(row_id == tf, bcum, gr)
    pad_rows = jnp.zeros((LANE - SUBLANE, chunk), F32)

    mask = _causal_mask(chunk, reverse)
    ones_col = jnp.where(lax.broadcasted_iota(jnp.int32, (chunk, LANE), 1) == 0, 1.0, 0.0).astype(BF16)
    scale = dk ** -0.5
    nch = lb // chunk
    for ch in (range(nch - 1, -1, -1) if reverse else range(nch)):
        rows = slice(ch * chunk, (ch + 1) * chunk)
        for hh in range(HEADS):
            kc = slice(hh * dk, (hh + 1) * dk)
            vc = slice(hh * dv, (hh + 1) * dv)
            rf = rowform[hh * SUBLANE:(hh + 1) * SUBLANE, rows]
            u_row = rf[ti:ti + 1, :] - rf[tf:tf + 1, :]
            colform = jnp.transpose(jnp.concatenate([rf, pad_rows], axis=0))
            bc = colform[:, tf:tf + 1]
            ic = colform[:, ti:ti + 1]
            gtot = bc[0:1, :] if reverse else bc[chunk - 1:chunk, :]
            log_d = jnp.where(mask, bc + u_row, NEG_BIG)
            m_intra = jnp.max(log_d, axis=1, keepdims=True)
            q = q_ref[rows, kc]
            k = k_ref[rows, kc]
            v_aug = jnp.concatenate([v_ref[rows, vc], ones_col], axis=1)
            s = lax.dot_general(q, k, NT_DIMS, preferred_element_type=F32) * jnp.exp(log_d - m_intra)
            intra = _dot(s.astype(BF16), v_aug)
            c = c_ref[hh]
            inter = _dot(q, c.astype(BF16))
            m_prev = m_ref[hh, 0:1, 0:1]
            bm = bc + m_prev
            m_j = jnp.maximum(bm, m_intra)
            comb = (scale * jnp.exp(bm - m_j)) * inter + (scale * jnp.exp(m_intra - m_j)) * intra
            denom = jnp.maximum(jnp.abs(comb[:, dv:dv + 1]), jnp.exp(-m_j))
            h = comb[:, :dv] * (1.0 / denom)
            a_col = gtot - bc + ic
            m_new = jnp.maximum(gtot + m_prev, jnp.max(a_col, axis=0, keepdims=True))
            kw = (k.astype(F32) * jnp.exp(a_col - m_new)).astype(BF16)
            c_ref[hh] = (jnp.exp(gtot + m_prev - m_new) * c
                         + lax.dot_general(kw, v_aug, TN_DIMS, preferred_element_type=F32))
            m_ref[hh] = jnp.broadcast_to(m_new, (SUBLANE, LANE))
            if final:
                h = _head_norm_gate(h, hb_ref[rows, vc], _sigmoid(mo_ref[rows, vc].astype(F32)), nw_ref[:, vc])
            o_ref[rows, vc] = h.astype(o_ref.dtype)


def _mlstm_scan(proj, gates_t, bias_r, l, *, reverse, dk, dv, col_q, col_k, col_v,
                col_gate=None, other=None, norm_w=None):
    bsz, seq, _ = proj.shape
    lb = _pick(seq, 512, LANE)
    nblk = seq // lb
    final = other is not None
    qk, width = HEADS * dk, HEADS * dv

    def tok(i):
        return (nblk - 1 - i) if reverse else i

    def col_spec(w, col0):
        return pl.BlockSpec((None, lb, w), lambda b, i: (b, tok(i), col0 // w))

    in_specs = [col_spec(qk, col_q), col_spec(qk, col_k), col_spec(width, col_v),
                pl.BlockSpec((HEADS * SUBLANE, lb), lambda b, i: (0, b * nblk + tok(i))),
                pl.BlockSpec((None, HEADS * SUBLANE, LANE), lambda b, i: (l, 0, 0))]
    args = [proj, proj, proj, gates_t, bias_r]
    if final:
        in_specs += [col_spec(width, col_gate),
                     pl.BlockSpec((None, lb, width), lambda b, i: (b, tok(i), 0)),
                     pl.BlockSpec((None, 1, width), lambda b, i: (l, 0, 0))]
        args += [proj, other, norm_w]
    return pl.pallas_call(
        functools.partial(_mlstm_body, reverse=reverse, final=final, lb=lb),
        grid=(bsz, nblk),
        in_specs=in_specs,
        out_specs=pl.BlockSpec((None, lb, width), lambda b, i: (b, tok(i), 0)),
        out_shape=jax.ShapeDtypeStruct((bsz, seq, width), BF16 if final else F32),
        scratch_shapes=[pltpu.VMEM((HEADS, dk, dv + LANE), F32), pltpu.VMEM((HEADS, SUBLANE, LANE), F32)],
        compiler_params=_params(("parallel", "arbitrary")),
        name="mlstm_fwd" if final else "mlstm_bwd",
    )(*args)


def _xattn_body(x_ref, nw_ref, wq_ref, kv_ref, wo_ref, nw2_ref, o_ref, n_ref, *, heads):
    x = x_ref[...]
    xn = (x * lax.rsqrt(jnp.mean(x * x, axis=-1, keepdims=True) + EPS) * nw_ref[...]).astype(BF16)
    width = wq_ref.shape[1]
    hd = width // heads
    q = (_dot(xn, wq_ref[...]) * (hd ** -0.5)).astype(BF16)
    outs = []
    for h in range(heads):
        kh = kv_ref[:, h * hd:(h + 1) * hd]
        vh = kv_ref[:, width + h * hd:width + (h + 1) * hd]
        s = lax.dot_general(q[:, h * hd:(h + 1) * hd], kh, NT_DIMS, preferred_element_type=F32)
        p = jnp.exp(s - jnp.max(s, axis=-1, keepdims=True))
        p = p * (1.0 / jnp.sum(p, axis=-1, keepdims=True))
        outs.append(_dot(p.astype(BF16), vh))
    o = jnp.concatenate(outs, axis=1).astype(BF16)
    y = x + _dot(o, wo_ref[...])
    o_ref[...] = y
    n_ref[...] = (y * lax.rsqrt(jnp.mean(y * y, axis=-1, keepdims=True) + EPS) * nw2_ref[...]).astype(BF16)


def _xattn(x, norm_w, wq, kv, wo, next_norm_w, l, tq=256):
    bsz, seq, d = x.shape
    mem = kv.shape[1]
    width = wq.shape[2]
    tq = _pick(seq, tq, 16)
    return pl.pallas_call(
        functools.partial(_xattn_body, heads=XATTN_HEADS),
        grid=(bsz, seq // tq),
        in_specs=[pl.BlockSpec((None, tq, d), lambda b, i: (b, i, 0)),
                  pl.BlockSpec((None, 1, d), lambda b, i: (l, 0, 0)),
                  pl.BlockSpec((None, d, width), lambda b, i: (l, 0, 0)),
                  pl.BlockSpec((None, mem, 2 * width), lambda b, i: (b, 0, 0)),
                  pl.BlockSpec((None, width, d), lambda b, i: (l, 0, 0)),
                  pl.BlockSpec((None, 1, d), lambda b, i: (l, 0, 0))],
        out_specs=[pl.BlockSpec((None, tq, d), lambda b, i: (b, i, 0)),
                   pl.BlockSpec((None, tq, d), lambda b, i: (b, i, 0))],
        out_shape=[jax.ShapeDtypeStruct((bsz, seq, d), F32),
                   jax.ShapeDtypeStruct((bsz, seq, d), BF16)],
        compiler_params=_params(("parallel", "arbitrary")),
        name="mem_xattn",
    )(x, norm_w, wq, kv, wo, next_norm_w)


def _norm_w(w):
    return w.astype(F32).reshape(w.shape[0], 1, w.shape[1])


def _ffn(x, xn, w_gate, w_up, w_down, l, next_gain=None):
    hid = _swiglu_up(xn, w_gate, w_up, l)
    return _matmul_residual(hid, w_down, l, x, 0.5, next_gain=next_gain, name="ffn_down")


def _mixer_weights(d, w_in, gla_w_lr, gla_b_lr, mlstm_gate_b):
    depth = w_in.shape[0]
    half = d // 2
    qk = half // 2
    o_lr = 2 * qk + 2 * half
    o_m = o_lr + 2 * GLA_RANK
    o_mg = o_m + 2 * qk + 2 * half
    w_gla = w_in[:, :, :o_lr].astype(BF16)
    w_ml = w_in[:, :, o_m:o_mg].astype(BF16)
    w_lr = jnp.pad(w_in[:, :, o_lr:o_m], ((0, 0), (0, 0), (0, LANE - 2 * GLA_RANK))).astype(BF16)
    mg = w_in[:, :, o_mg:o_mg + 4 * HEADS].reshape(depth, d, 4, HEADS).transpose(0, 3, 2, 1)
    w_gates_t = jnp.pad(mg, ((0, 0), (0, 0), (0, SUBLANE - 4), (0, 0))).reshape(depth, HEADS * SUBLANE, d)
    w_gates_t = w_gates_t.astype(BF16)
    bias_r = jnp.pad(mlstm_gate_b.astype(F32).transpose(0, 2, 1), ((0, 0), (0, 0), (0, SUBLANE - 4)))
    bias_r = jnp.broadcast_to(bias_r.reshape(depth, HEADS * SUBLANE, 1), (depth, HEADS * SUBLANE, LANE))
    wlr = jnp.stack([jnp.pad(gla_w_lr[:, 0], ((0, 0), (0, LANE - GLA_RANK), (0, 0))),
                     jnp.pad(gla_w_lr[:, 1], ((0, 0), (GLA_RANK, LANE - 2 * GLA_RANK), (0, 0)))],
                    axis=1).astype(F32)
    blr = gla_b_lr.astype(F32).reshape(depth, 2, 1, qk)
    return w_gla, w_ml, w_lr, w_gates_t, bias_r, wlr, blr


def _mixer(x, xn, bsz, seq, l, w_gla, w_ml, w_lr, w_gates_t, bias_r, wlr, blr, gla_norm, mlstm_norm, w_out):
    m, d = x.shape
    half = d // 2
    dv = half // HEADS
    dk = dv // 2
    qk = HEADS * dk
    proj_g = _matmul(xn, w_gla, l, BF16, name="gla_in_proj").reshape(bsz, seq, -1)
    proj_m = _matmul(xn, w_ml, l, BF16, name="mlstm_in_proj").reshape(bsz, seq, -1)
    lr, gates_t = _gate_proj(xn, w_lr, w_gates_t, l)
    lr = lr.reshape(bsz, seq, LANE)

    kw = dict(dk=dk, dv=dv, col_q=0, col_k=qk, col_v=2 * qk)
    g_b = _gla_scan(proj_g, lr, wlr, blr, l, reverse=True, **kw)
    g_out = _gla_scan(proj_g, lr, wlr, blr, l, reverse=False, col_gate=2 * qk + half, other=g_b,
                      norm_w=gla_norm, **kw)
    m_b = _mlstm_scan(proj_m, gates_t, bias_r, l, reverse=True, **kw)
    m_out = _mlstm_scan(proj_m, gates_t, bias_r, l, reverse=False, col_gate=2 * qk + half,
                        other=m_b, norm_w=mlstm_norm, **kw)
    return _out_proj(g_out.reshape(m, half), m_out.reshape(m, half), w_out, l, x)


def kernel(x_prompt, x_sample, mem_prompt, mem_sample, ffn1_norm, ffn1_w_gate, ffn1_w_up, ffn1_w_down, mix_norm, w_in, gla_w_lr, gla_b_lr, gla_out_norm, mlstm_gate_b, mlstm_out_norm, w_out, xattn_norm, mem_norm, xattn_wq, xattn_wk, xattn_wv, xattn_wo, ffn2_norm, ffn2_w_gate, ffn2_w_up, ffn2_w_down, final_norm):
    assert x_prompt.shape[1:] == x_sample.shape[1:] and mem_prompt.shape[1:] == mem_sample.shape[1:]
    n_p, seq, d = x_prompt.shape
    n_s = x_sample.shape[0]
    bsz = n_p + n_s
    m = bsz * seq
    x = (x_prompt.reshape(n_p * seq, d), x_sample.reshape(n_s * seq, d))
    mem = jnp.concatenate([mem_prompt, mem_sample], axis=0)
    n_mem = mem.shape[1]
    mem = mem.reshape(bsz * n_mem, d)
    depth = ffn1_norm.shape[0]

    ffn1 = (ffn1_w_gate.astype(BF16), ffn1_w_up.astype(BF16), ffn1_w_down.astype(BF16))
    ffn2 = (ffn2_w_gate.astype(BF16), ffn2_w_up.astype(BF16), ffn2_w_down.astype(BF16))
    mixer_w = _mixer_weights(d, w_in, gla_w_lr, gla_b_lr, mlstm_gate_b)
    mixer_rest = (_norm_w(gla_out_norm), _norm_w(mlstm_out_norm), w_out.astype(BF16))
    w_kv = jnp.concatenate([xattn_wk, xattn_wv], axis=2).astype(BF16)
    wq = xattn_wq.astype(BF16)
    wo = xattn_wo.astype(BF16)
    ffn1_n, ffn2_n = _norm_w(ffn1_norm), _norm_w(ffn2_norm)
    mix_n, xattn_n, mem_n = _norm_w(mix_norm), _norm_w(xattn_norm), _norm_w(mem_norm)

    xn = _rmsnorm_pair(*x, ffn1_n, 0, BF16)
    for l in range(depth):
        x, xn = _ffn(x, xn, *ffn1, l, next_gain=(mix_n, l))
        x = _mixer(x, xn, bsz, seq, l, *mixer_w, *mixer_rest)
        kv = _matmul(_rmsnorm(mem, mem_n, l, BF16), w_kv, l, BF16, tm=512, name="mem_kv_proj")
        x, xn = _xattn(x.reshape(bsz, seq, d), xattn_n, wq, kv.reshape(bsz, n_mem, -1), wo, ffn2_n, l)
        x, xn = x.reshape(m, d), xn.reshape(m, d)
        if l + 1 < depth:
            x, xn = _ffn(x, xn, *ffn2, l, next_gain=(ffn1_n, l + 1))
        else:
            x = _ffn(x, xn, *ffn2, l)
    fin = _norm_w(final_norm.reshape(1, d))
    y_p = _rmsnorm(x, fin, 0, F32, 0, n_p * seq).reshape(n_p, seq, d)
    y_s = _rmsnorm(x, fin, 0, F32, n_p * seq, n_s * seq).reshape(n_s, seq, d)
    return (y_p, y_s)
```

```python
import functools

import jax
import jax.numpy as jnp
from jax import lax
from jax.experimental import pallas as pl
from jax.experimental.pallas import tpu as pltpu

F32 = jnp.float32
BF16 = jnp.bfloat16

EPS = 1e-6
NEG_BIG = -1e30
HEADS = 4
GLA_RANK = 16
GLA_TAU = 16.0
GLA_MIN_LOG_DECAY = -1.0
GLA_CHUNK = 64
MLSTM_CHUNK = 256
XATTN_HEADS = 4
LANE = 128
SUBLANE = 8
VMEM_LIMIT = 56 * 1024 * 1024

NT_DIMS = (((1,), (1,)), ((), ()))
TN_DIMS = (((0,), (0,)), ((), ()))


def _pick(n, target, mult):
    if n <= target:
        return n
    t = (target // mult) * mult
    while t >= mult:
        if n % t == 0:
            return t
        t -= mult
    return n


def _params(sem):
    return pltpu.CompilerParams(dimension_semantics=sem, vmem_limit_bytes=VMEM_LIMIT)


def _log_sigmoid(x):
    return jnp.minimum(x, 0.0) - jnp.log(1.0 + jnp.exp(-jnp.abs(x)))


def _sigmoid(x):
    return 0.5 * jnp.tanh(0.5 * x) + 0.5


def _split_bf16(x):
    hi = x.astype(BF16)
    return hi, (x - hi.astype(F32)).astype(BF16)


def _dot(a, b):
    return jnp.dot(a, b, preferred_element_type=F32)


def _rmsnorm_body(x_ref, w_ref, o_ref):
    x = x_ref[...]
    ms = jnp.mean(x * x, axis=-1, keepdims=True)
    o_ref[...] = (x * lax.rsqrt(ms + EPS) * w_ref[...]).astype(o_ref.dtype)


def _pair_specs(block, na, col=lambda *g: 0):
    return [pl.BlockSpec(block, lambda *g: (jnp.minimum(g[0], na - 1), col(*g))),
            pl.BlockSpec(block, lambda *g: (jnp.maximum(g[0] - na, 0), col(*g)))]


def _rmsnorm_pair_body(xa_ref, xb_ref, w_ref, o_ref, *, na):
    @pl.when(pl.program_id(0) < na)
    def _():
        _rmsnorm_body(xa_ref, w_ref, o_ref)

    @pl.when(pl.program_id(0) >= na)
    def _():
        _rmsnorm_body(xb_ref, w_ref, o_ref)


def _rmsnorm_pair(xa, xb, w, l, out_dtype):
    d = xa.shape[1]
    tr = _pick(xa.shape[0], 256, SUBLANE * 2)
    assert xa.shape[0] % tr == 0 and xb.shape[0] % tr == 0
    na, nb = xa.shape[0] // tr, xb.shape[0] // tr
    return pl.pallas_call(
        functools.partial(_rmsnorm_pair_body, na=na),
        grid=(na + nb,),
        in_specs=_pair_specs((tr, d), na) + [pl.BlockSpec((None, 1, d), lambda i: (l, 0, 0))],
        out_specs=pl.BlockSpec((tr, d), lambda i: (i, 0)),
        out_shape=jax.ShapeDtypeStruct(((na + nb) * tr, d), out_dtype),
        compiler_params=_params(("parallel",)),
        name="rmsnorm",
    )(xa, xb, w)


def _rmsnorm(x, w, l, out_dtype, row_start=0, rows=None):
    m, d = x.shape
    rows = m if rows is None else rows
    tr = _pick(rows, 256, SUBLANE * 2)
    assert row_start % tr == 0
    off = row_start // tr
    return pl.pallas_call(
        _rmsnorm_body,
        grid=(rows // tr,),
        in_specs=[pl.BlockSpec((tr, d), lambda i: (i + off, 0)),
                  pl.BlockSpec((None, 1, d), lambda i: (l, 0, 0))],
        out_specs=pl.BlockSpec((tr, d), lambda i: (i, 0)),
        out_shape=jax.ShapeDtypeStruct((rows, d), out_dtype),
        compiler_params=_params(("parallel",)),
        name="rmsnorm",
    )(x, w)


def _row_scale(rs_ref, width):
    return jnp.concatenate([rs_ref[...]] * (width // LANE), axis=1)


def _split_act(a):
    return a if isinstance(a, tuple) else (a, None)


def _ss_spec(tm, ss):
    return [] if ss is None else [pl.BlockSpec((tm, LANE), lambda i, *_: (i, 0))]


def _mm_body(a_ref, b_ref, *rest):
    *ss_ref, o_ref = rest
    p = _dot(a_ref[...], b_ref[...])
    if ss_ref:
        p = p * _row_scale(ss_ref[0], p.shape[1])
    o_ref[...] = p.astype(o_ref.dtype)


def _matmul(a, w, l, out_dtype, tm=1024, tn=1024, name="matmul"):
    a, ss = _split_act(a)
    m, k = a.shape
    n = w.shape[2]
    tm = _pick(m, tm, 16)
    tn = _pick(n, tn, LANE)
    return pl.pallas_call(
        _mm_body,
        grid=(m // tm, n // tn),
        in_specs=[pl.BlockSpec((tm, k), lambda i, j: (i, 0)),
                  pl.BlockSpec((None, k, tn), lambda i, j: (l, 0, j))] + _ss_spec(tm, ss),
        out_specs=pl.BlockSpec((tm, tn), lambda i, j: (i, j)),
        out_shape=jax.ShapeDtypeStruct((m, n), out_dtype),
        compiler_params=_params(("parallel", "arbitrary")),
        name=name,
    )(a, w, *([] if ss is None else [ss]))


def _swiglu_body(a_ref, wg_ref, wu_ref, *rest):
    *ss_ref, o_ref = rest
    a = a_ref[...]
    g = _dot(a, wg_ref[...])
    u = _dot(a, wu_ref[...])
    if ss_ref:
        rs = _row_scale(ss_ref[0], g.shape[1])
        g, u = g * rs, u * rs
    o_ref[...] = (g * _sigmoid(g) * u).astype(o_ref.dtype)


def _swiglu_up(a, wg, wu, l, tm=2048, tn=256):
    a, ss = _split_act(a)
    m, k = a.shape
    n = wg.shape[2]
    tm = _pick(m, tm, 16)
    tn = min(tn, n)
    return pl.pallas_call(
        _swiglu_body,
        grid=(m // tm, pl.cdiv(n, tn)),
        in_specs=[pl.BlockSpec((tm, k), lambda i, j: (i, 0)),
                  pl.BlockSpec((None, k, tn), lambda i, j: (l, 0, j)),
                  pl.BlockSpec((None, k, tn), lambda i, j: (l, 0, j))] + _ss_spec(tm, ss),
        out_specs=pl.BlockSpec((tm, tn), lambda i, j: (i, j)),
        out_shape=jax.ShapeDtypeStruct((m, n), BF16),
        compiler_params=_params(("parallel", "arbitrary")),
        name="swiglu_up",
    )(a, wg, wu, *([] if ss is None else [ss]))


def _mm_res_body(a_ref, b_ref, *rest, scale, na, fuse, width):
    n_res = 1 if na is None else 2
    res_refs, rest = rest[:n_res], rest[n_res:]
    if na is None:
        r = res_refs[0][...]
    else:
        r = jnp.where(pl.program_id(0) < na, res_refs[0][...], res_refs[1][...])
    y = r + scale * _dot(a_ref[...], b_ref[...])
    if not fuse:
        rest[0][...] = y
        return
    gain_ref, o_ref, yw_ref, ss_ref = rest
    o_ref[...] = y
    yw_ref[...] = (y * gain_ref[...]).astype(yw_ref.dtype)
    sq = y * y
    part = sq[:, 0:LANE]
    for t in range(1, sq.shape[1] // LANE):
        part = part + sq[:, t * LANE:(t + 1) * LANE]

    @pl.when(pl.program_id(1) == 0)
    def _():
        ss_ref[...] = part

    @pl.when(pl.program_id(1) > 0)
    def _():
        ss_ref[...] += part

    @pl.when(pl.program_id(1) == pl.num_programs(1) - 1)
    def _():
        tot = jnp.sum(ss_ref[...], axis=1, keepdims=True)
        ss_ref[...] = jnp.broadcast_to(lax.rsqrt(tot * (1.0 / width) + EPS), ss_ref.shape)


def _matmul_residual(a, w, l, res, scale, tm=512, tn=512, next_gain=None, name="matmul_res"):
    m, k = a.shape
    n = w.shape[2]
    tm = _pick(m, tm, 16)
    tn = _pick(n, tn, LANE)
    if isinstance(res, tuple):
        assert res[0].shape[0] % tm == 0 and res[1].shape[0] % tm == 0
        na = res[0].shape[0] // tm
        res_specs = _pair_specs((tm, tn), na, col=lambda i, j: j)
    else:
        na, res = None, (res,)
        res_specs = [pl.BlockSpec((tm, tn), lambda i, j: (i, j))]
    in_specs = [pl.BlockSpec((tm, k), lambda i, j: (i, 0)),
                pl.BlockSpec((None, k, tn), lambda i, j: (l, 0, j))] + res_specs
    out_specs = pl.BlockSpec((tm, tn), lambda i, j: (i, j))
    out_shape = jax.ShapeDtypeStruct((m, n), F32)
    args = [a, w, *res]
    fuse = next_gain is not None
    if fuse:
        gains, gl = next_gain
        in_specs.append(pl.BlockSpec((None, 1, tn), lambda i, j: (gl, 0, j)))
        args.append(gains)
        out_specs = [out_specs, pl.BlockSpec((tm, tn), lambda i, j: (i, j)),
                     pl.BlockSpec((tm, LANE), lambda i, j: (i, 0))]
        out_shape = [out_shape, jax.ShapeDtypeStruct((m, n), BF16), jax.ShapeDtypeStruct((m, LANE), F32)]
    out = pl.pallas_call(
        functools.partial(_mm_res_body, scale=scale, na=na, fuse=fuse, width=n),
        grid=(m // tm, n // tn),
        in_specs=in_specs,
        out_specs=out_specs,
        out_shape=out_shape,
        compiler_params=_params(("parallel", "arbitrary")),
        name=name,
    )(*args)
    return (out[0], (out[1], out[2])) if fuse else out


def _mm2_res_body(a1_ref, a2_ref, b1_ref, b2_ref, r_ref, o_ref):
    o_ref[...] = r_ref[...] + (_dot(a1_ref[...], b1_ref[...]) + _dot(a2_ref[...], b2_ref[...]))


def _out_proj(a1, a2, w, l, res, tm=1024, tn=512):
    m, k1 = a1.shape
    k2 = a2.shape[1]
    assert k1 == k2 and w.shape[1] == k1 + k2
    n = w.shape[2]
    tm = _pick(m, tm, 16)
    tn = _pick(n, tn, LANE)
    return pl.pallas_call(
        _mm2_res_body,
        grid=(m // tm, n // tn),
        in_specs=[pl.BlockSpec((tm, k1), lambda i, j: (i, 0)),
                  pl.BlockSpec((tm, k2), lambda i, j: (i, 0)),
                  pl.BlockSpec((None, k1, tn), lambda i, j: (l, 0, j)),
                  pl.BlockSpec((None, k2, tn), lambda i, j: (l, 1, j)),
                  pl.BlockSpec((tm, tn), lambda i, j: (i, j))],
        out_specs=pl.BlockSpec((tm, tn), lambda i, j: (i, j)),
        out_shape=jax.ShapeDtypeStruct((m, n), F32),
        compiler_params=_params(("parallel", "arbitrary")),
        name="mixer_out_proj",
    )(a1, a2, w, w, res)


def _gate_proj_body(a_ref, w_ref, wt_ref, *rest):
    *ss_ref, o_ref, ot_ref = rest
    a = a_ref[...]
    o = _dot(a, w_ref[...])
    ot = lax.dot_general(wt_ref[...], a, NT_DIMS, preferred_element_type=F32)
    if ss_ref:
        o = o * _row_scale(ss_ref[0], o.shape[1])
        ot = ot * jnp.transpose(ss_ref[0][...])[0:1, :]
    o_ref[...] = o
    ot_ref[...] = ot


def _gate_proj(a, w, wt, l, tm=1024):
    a, ss = _split_act(a)
    m, k = a.shape
    n = w.shape[2]
    nt = wt.shape[1]
    tm = _pick(m, tm, LANE)
    return pl.pallas_call(
        _gate_proj_body,
        grid=(m // tm,),
        in_specs=[pl.BlockSpec((tm, k), lambda i: (i, 0)),
                  pl.BlockSpec((None, k, n), lambda i: (l, 0, 0)),
                  pl.BlockSpec((None, nt, k), lambda i: (l, 0, 0))] + _ss_spec(tm, ss),
        out_specs=[pl.BlockSpec((tm, n), lambda i: (i, 0)),
                   pl.BlockSpec((nt, tm), lambda i: (0, i))],
        out_shape=[jax.ShapeDtypeStruct((m, n), F32),
                   jax.ShapeDtypeStruct((nt, m), F32)],
        compiler_params=_params(("parallel",)),
        name="gate_proj",
    )(a, w, wt, *([] if ss is None else [ss]))


def _causal_mask(chunk, reverse):
    r = lax.broadcasted_iota(jnp.int32, (chunk, chunk), 0)
    c = lax.broadcasted_iota(jnp.int32, (chunk, chunk), 1)
    return (c >= r) if reverse else (c <= r)


def _head_norm_gate(h, other, gate, nw):
    h = h + other
    return h * lax.rsqrt(jnp.mean(h * h, axis=-1, keepdims=True) + EPS) * nw * gate


def _gla_body(q_ref, k_ref, v_ref, lr_ref, wlr_ref, blr_ref, *rest, reverse, final, lb, hps):
    if final:
        gg_ref, ob_ref, nw_ref, o_ref, s_ref = rest
    else:
        o_ref, s_ref = rest
    dk = q_ref.shape[-1] // hps
    dv = v_ref.shape[-1] // hps
    chunk = GLA_CHUNK
    nch = lb // chunk
    qk = hps * dk

    @pl.when(pl.program_id(2) == 0)
    def _():
        s_ref[...] = jnp.zeros_like(s_ref)

    lh, ll = _split_bf16(lr_ref[...])
    wh, wl = _split_bf16(wlr_ref[...])
    z = _dot(lh, wh) + _dot(lh, wl) + _dot(ll, wh) + blr_ref[...]
    g = jnp.maximum(_log_sigmoid(z) * (1.0 / GLA_TAU), GLA_MIN_LOG_DECAY)
    mask = _causal_mask(chunk, reverse)
    tri = jnp.where(mask, 1.0, 0.0).astype(BF16)
    gh, gl = _split_bf16(jnp.concatenate([g[c * chunk:(c + 1) * chunk, :] for c in range(nch)], axis=1))
    bcum = _dot(tri, gh) + _dot(tri, gl)

    scale = dk ** -0.5
    zero_blk = jnp.zeros((chunk, chunk), BF16)
    npair = nch // 2
    for pb in (range(npair - 1, -1, -1) if reverse else range(npair)):
        rows = slice(2 * pb * chunk, (2 * pb + 2) * chunk)
        c_first, c_second = (2 * pb + 1, 2 * pb) if reverse else (2 * pb, 2 * pb + 1)
        for hh in range(hps):
            kc = slice(hh * dk, (hh + 1) * dk)
            vc = slice(hh * dv, (hh + 1) * dv)

            def chunk_terms(c):
                b = bcum[:, c * qk + hh * dk:c * qk + (hh + 1) * dk]
                tot = b[0:1, :] if reverse else b[chunk - 1:chunk, :]
                r = slice(c * chunk, (c + 1) * chunk)
                q_dec = q_ref[r, kc].astype(F32) * (jnp.exp(b) * scale)
                k_inv = k_ref[r, kc].astype(F32) * jnp.exp(-b)
                return tot, q_dec, k_inv, k_inv * jnp.exp(tot)

            tot1, qd1, ki1, ke1 = chunk_terms(c_first)
            tot2, qd2, ki2, ke2 = chunk_terms(c_second)
            qd1b, qd2b = qd1.astype(BF16), qd2.astype(BF16)
            a11 = lax.dot_general(qd1b, ki1.astype(BF16), NT_DIMS, preferred_element_type=F32)
            a22 = lax.dot_general(qd2b, ki2.astype(BF16), NT_DIMS, preferred_element_type=F32)
            a21 = lax.dot_general(qd2b, ke1.astype(BF16), NT_DIMS, preferred_element_type=F32)
            a11 = jnp.where(mask, a11, 0.0).astype(BF16)
            a22 = jnp.where(mask, a22, 0.0).astype(BF16)
            a21 = a21.astype(BF16)
            qs2 = (qd2 * jnp.exp(tot1)).astype(BF16)
            kx1 = (ke1 * jnp.exp(tot2)).astype(BF16)
            ke2b = ke2.astype(BF16)
            if reverse:
                amat = jnp.concatenate([jnp.concatenate([a22, a21], axis=1),
                                        jnp.concatenate([zero_blk, a11], axis=1)], axis=0)
                qmat = jnp.concatenate([qs2, qd1b], axis=0)
                kmat = jnp.concatenate([ke2b, kx1], axis=0)
            else:
                amat = jnp.concatenate([jnp.concatenate([a11, zero_blk], axis=1),
                                        jnp.concatenate([a21, a22], axis=1)], axis=0)
                qmat = jnp.concatenate([qd1b, qs2], axis=0)
                kmat = jnp.concatenate([kx1, ke2b], axis=0)
            v = v_ref[rows, vc]
            s = s_ref[hh]
            o = _dot(amat, v) + _dot(qmat, s.astype(BF16))
            dec = jnp.exp(jnp.transpose(jnp.broadcast_to(tot1 + tot2, (LANE, dk))))
            dec = jnp.concatenate([dec] * (dv // LANE), axis=1)
            s_ref[hh] = s * dec + lax.dot_general(kmat, v, TN_DIMS, preferred_element_type=F32)
            if final:
                gg = gg_ref[rows, vc].astype(F32)
                o = _head_norm_gate(o, ob_ref[rows, vc], gg * _sigmoid(gg), nw_ref[:, vc])
            o_ref[rows, vc] = o.astype(o_ref.dtype)


def _gla_scan(proj, lr, wlr, blr, l, *, reverse, dk, dv, col_q, col_k, col_v, col_gate=None,
              other=None, norm_w=None, hps=HEADS):
    bsz, seq, _ = proj.shape
    lb = _pick(seq, 512, LANE)
    assert lb % (2 * GLA_CHUNK) == 0 and HEADS % hps == 0
    nblk = seq // lb
    final = other is not None
    qk, width = hps * dk, hps * dv

    def tok(i):
        return (nblk - 1 - i) if reverse else i

    def col_spec(w, col0):
        return pl.BlockSpec((None, lb, w), lambda b, h, i: (b, tok(i), col0 // w + h))

    d = 1 if reverse else 0
    in_specs = [col_spec(qk, col_q), col_spec(qk, col_k), col_spec(width, col_v),
                pl.BlockSpec((None, lb, LANE), lambda b, h, i: (b, tok(i), 0)),
                pl.BlockSpec((None, None, LANE, qk), lambda b, h, i: (l, d, 0, h)),
                pl.BlockSpec((None, None, 1, qk), lambda b, h, i: (l, d, 0, h))]
    args = [proj, proj, proj, lr, wlr, blr]
    if final:
        in_specs += [col_spec(width, col_gate),
                     pl.BlockSpec((None, lb, width), lambda b, h, i: (b, tok(i), h)),
                     pl.BlockSpec((None, 1, width), lambda b, h, i: (l, 0, h))]
        args += [proj, other, norm_w]
    return pl.pallas_call(
        functools.partial(_gla_body, reverse=reverse, final=final, lb=lb, hps=hps),
        grid=(bsz, HEADS // hps, nblk),
        in_specs=in_specs,
        out_specs=pl.BlockSpec((None, lb, width), lambda b, h, i: (b, tok(i), h)),
        out_shape=jax.ShapeDtypeStruct((bsz, seq, HEADS * dv), BF16 if final else F32),
        scratch_shapes=[pltpu.VMEM((hps, dk, dv), F32)],
        compiler_params=_params(("parallel", "parallel", "arbitrary")),
        name="gla_fwd" if final else "gla_bwd",
    )(*args)


def _segment_cumsum_lanes(x, seg, reverse):
    n = x.shape[-1]
    pos = lax.broadcasted_iota(jnp.int32, x.shape, x.ndim - 1) % seg
    s = 1
    while s < seg:
        if reverse:
            x = x + jnp.where(pos < seg - s, pltpu.roll(x, n - s, x.ndim - 1), 0.0)
        else:
            x = x + jnp.where(pos >= s, pltpu.roll(x, s, x.ndim - 1), 0.0)
        s *= 2
    return x


def _mlstm_body(q_ref, k_ref, v_ref, gr_ref, br_ref, *rest, reverse, final, lb, hps):
    if final:
        mo_ref, hb_ref, nw_ref, o_ref, c_ref, m_ref = rest
    else:
        o_ref, c_ref, m_ref = rest
    dk = q_ref.shape[-1] // hps
    dv = v_ref.shape[-1] // hps
    chunk = min(MLSTM_CHUNK, lb)

    @pl.when(pl.program_id(2) == 0)
    def _():
        c_ref[...] = jnp.zeros_like(c_ref)
        m_ref[...] = jnp.full_like(m_ref, NEG_BIG)

    ti, tf = (2, 3) if reverse else (0, 1)
    gr = gr_ref[...] + br_ref[:, 0:1]
    bcum = _segment_cumsum_lanes(_log_sigmoid(gr), chunk, reverse)
    row_id = lax.broadcasted_iota(jnp.int32, gr.shape, 0) % SUBLANE
    rowform = jnp.where(row_id == tf, bcum, gr)
    pad_rows = jnp.zeros((LANE - SUBLANE, chunk), F32)

    mask = _causal_mask(chunk, reverse)
    ones_col = jnp.where(lax.broadcasted_iota(jnp.int32, (chunk, LANE), 1) == 0, 1.0, 0.0).astype(BF16)
    scale = dk ** -0.5
    nch = lb // chunk
    for ch in (range(nch - 1, -1, -1) if reverse else range(nch)):
        rows = slice(ch * chunk, (ch + 1) * chunk)
        for hh in range(hps):
            kc = slice(hh * dk, (hh + 1) * dk)
            vc = slice(hh * dv, (hh + 1) * dv)
            rf = rowform[hh * SUBLANE:(hh + 1) * SUBLANE, rows]
            u_row = rf[ti:ti + 1, :] - rf[tf:tf + 1, :]
            colform = jnp.transpose(jnp.concatenate([rf, pad_rows], axis=0))
            bc = colform[:, tf:tf + 1]
            ic = colform[:, ti:ti + 1]
            gtot = bc[0:1, :] if reverse else bc[chunk - 1:chunk, :]
            log_d = jnp.where(mask, bc + u_row, NEG_BIG)
            m_intra = jnp.max(log_d, axis=1, keepdims=True)
            q = q_ref[rows, kc]
            k = k_ref[rows, kc]
            v_aug = jnp.concatenate([v_ref[rows, vc], ones_col], axis=1)
            s = lax.dot_general(q, k, NT_DIMS, preferred_element_type=F32) * jnp.exp(log_d - m_intra)
            intra = _dot(s.astype(BF16), v_aug)
            c = c_ref[hh]
            inter = _dot(q, c.astype(BF16))
            m_prev = m_ref[hh, 0:1, 0:1]
            bm = bc + m_prev
            m_j = jnp.maximum(bm, m_intra)
            comb = (scale * jnp.exp(bm - m_j)) * inter + (scale * jnp.exp(m_intra - m_j)) * intra
            denom = jnp.maximum(jnp.abs(comb[:, dv:dv + 1]), jnp.exp(-m_j))
            h = comb[:, :dv] * (1.0 / denom)
            a_col = gtot - bc + ic
            m_new = jnp.maximum(gtot + m_prev, jnp.max(a_col, axis=0, keepdims=True))
            kw = (k.astype(F32) * jnp.exp(a_col - m_new)).astype(BF16)
            c_ref[hh] = (jnp.exp(gtot + m_prev - m_new) * c
                         + lax.dot_general(kw, v_aug, TN_DIMS, preferred_element_type=F32))
            m_ref[hh] = jnp.broadcast_to(m_new, (SUBLANE, LANE))
            if final:
                h = _head_norm_gate(h, hb_ref[rows, vc], _sigmoid(mo_ref[rows, vc].astype(F32)), nw_ref[:, vc])
            o_ref[rows, vc] = h.astype(o_ref.dtype)


def _mlstm_scan(proj, gates_t, bias_r, l, *, reverse, dk, dv, col_q, col_k, col_v,
                col_gate=None, other=None, norm_w=None, hps=HEADS):
    bsz, seq, _ = proj.shape
    lb = _pick(seq, 512, LANE)
    nblk = seq // lb
    final = other is not None
    assert HEADS % hps == 0
    qk, width = hps * dk, hps * dv

    def tok(i):
        return (nblk - 1 - i) if reverse else i

    def col_spec(w, col0):
        return pl.BlockSpec((None, lb, w), lambda b, h, i: (b, tok(i), col0 // w + h))

    in_specs = [col_spec(qk, col_q), col_spec(qk, col_k), col_spec(width, col_v),
                pl.BlockSpec((hps * SUBLANE, lb), lambda b, h, i: (h, b * nblk + tok(i))),
                pl.BlockSpec((None, hps * SUBLANE, LANE), lambda b, h, i: (l, h, 0))]
    args = [proj, proj, proj, gates_t, bias_r]
    if final:
        in_specs += [col_spec(width, col_gate),
                     pl.BlockSpec((None, lb, width), lambda b, h, i: (b, tok(i), h)),
                     pl.BlockSpec((None, 1, width), lambda b, h, i: (l, 0, h))]
        args += [proj, other, norm_w]
    return pl.pallas_call(
        functools.partial(_mlstm_body, reverse=reverse, final=final, lb=lb, hps=hps),
        grid=(bsz, HEADS // hps, nblk),
        in_specs=in_specs,
        out_specs=pl.BlockSpec((None, lb, width), lambda b, h, i: (b, tok(i), h)),
        out_shape=jax.ShapeDtypeStruct((bsz, seq, HEADS * dv), BF16 if final else F32),
        scratch_shapes=[pltpu.VMEM((hps, dk, dv + LANE), F32), pltpu.VMEM((hps, SUBLANE, LANE), F32)],
        compiler_params=_params(("parallel", "parallel", "arbitrary")),
        name="mlstm_fwd" if final else "mlstm_bwd",
    )(*args)


def _xattn_body(x_ref, nw_ref, wq_ref, kv_ref, wo_ref, nw2_ref, o_ref, n_ref, *, heads):
    x = x_ref[...]
    xn = (x * lax.rsqrt(jnp.mean(x * x, axis=-1, keepdims=True) + EPS) * nw_ref[...]).astype(BF16)
    width = wq_ref.shape[1]
    hd = width // heads
    q = (_dot(xn, wq_ref[...]) * (hd ** -0.5)).astype(BF16)
    outs = []
    for h in range(heads):
        kh = kv_ref[:, h * hd:(h + 1) * hd]
        vh = kv_ref[:, width + h * hd:width + (h + 1) * hd]
        s = lax.dot_general(q[:, h * hd:(h + 1) * hd], kh, NT_DIMS, preferred_element_type=F32)
        p = jnp.exp(s - jnp.max(s, axis=-1, keepdims=True))
        p = p * (1.0 / jnp.sum(p, axis=-1, keepdims=True))
        outs.append(_dot(p.astype(BF16), vh))
    o = jnp.concatenate(outs, axis=1).astype(BF16)
    y = x + _dot(o, wo_ref[...])
    o_ref[...] = y
    n_ref[...] = (y * lax.rsqrt(jnp.mean(y * y, axis=-1, keepdims=True) + EPS) * nw2_ref[...]).astype(BF16)


def _xattn(x, norm_w, wq, kv, wo, next_norm_w, l, tq=256):
    bsz, seq, d = x.shape
    mem = kv.shape[1]
    width = wq.shape[2]
    tq = _pick(seq, tq, 16)
    return pl.pallas_call(
        functools.partial(_xattn_body, heads=XATTN_HEADS),
        grid=(bsz, seq // tq),
        in_specs=[pl.BlockSpec((None, tq, d), lambda b, i: (b, i, 0)),
                  pl.BlockSpec((None, 1, d), lambda b, i: (l, 0, 0)),
                  pl.BlockSpec((None, d, width), lambda b, i: (l, 0, 0)),
                  pl.BlockSpec((None, mem, 2 * width), lambda b, i: (b, 0, 0)),
                  pl.BlockSpec((None, width, d), lambda b, i: (l, 0, 0)),
                  pl.BlockSpec((None, 1, d), lambda b, i: (l, 0, 0))],
        out_specs=[pl.BlockSpec((None, tq, d), lambda b, i: (b, i, 0)),
                   pl.BlockSpec((None, tq, d), lambda b, i: (b, i, 0))],
        out_shape=[jax.ShapeDtypeStruct((bsz, seq, d), F32),
                   jax.ShapeDtypeStruct((bsz, seq, d), BF16)],
        compiler_params=_params(("parallel", "arbitrary")),
        name="mem_xattn",
    )(x, norm_w, wq, kv, wo, next_norm_w)


def _norm_w(w):
    return w.astype(F32).reshape(w.shape[0], 1, w.shape[1])


def _ffn(x, xn, w_gate, w_up, w_down, l, next_gain=None, down_tiles=(512, 512)):
    hid = _swiglu_up(xn, w_gate, w_up, l)
    tm, tn = down_tiles
    return _matmul_residual(hid, w_down, l, x, 0.5, tm=tm, tn=tn, next_gain=next_gain, name="ffn_down")


def _mixer_weights(d, w_in, gla_w_lr, gla_b_lr, mlstm_gate_b):
    depth = w_in.shape[0]
    half = d // 2
    qk = half // 2
    o_lr = 2 * qk + 2 * half
    o_m = o_lr + 2 * GLA_RANK
    o_mg = o_m + 2 * qk + 2 * half
    w_gla = w_in[:, :, :o_lr].astype(BF16)
    w_ml = w_in[:, :, o_m:o_mg].astype(BF16)
    w_lr = jnp.pad(w_in[:, :, o_lr:o_m], ((0, 0), (0, 0), (0, LANE - 2 * GLA_RANK))).astype(BF16)
    mg = w_in[:, :, o_mg:o_mg + 4 * HEADS].reshape(depth, d, 4, HEADS).transpose(0, 3, 2, 1)
    w_gates_t = jnp.pad(mg, ((0, 0), (0, 0), (0, SUBLANE - 4), (0, 0))).reshape(depth, HEADS * SUBLANE, d)
    w_gates_t = w_gates_t.astype(BF16)
    bias_r = jnp.pad(mlstm_gate_b.astype(F32).transpose(0, 2, 1), ((0, 0), (0, 0), (0, SUBLANE - 4)))
    bias_r = jnp.broadcast_to(bias_r.reshape(depth, HEADS * SUBLANE, 1), (depth, HEADS * SUBLANE, LANE))
    wlr = jnp.stack([jnp.pad(gla_w_lr[:, 0], ((0, 0), (0, LANE - GLA_RANK), (0, 0))),
                     jnp.pad(gla_w_lr[:, 1], ((0, 0), (GLA_RANK, LANE - 2 * GLA_RANK), (0, 0)))],
                    axis=1).astype(F32)
    blr = gla_b_lr.astype(F32).reshape(depth, 2, 1, qk)
    return w_gla, w_ml, w_lr, w_gates_t, bias_r, wlr, blr


def _mixer(x, xn, bsz, seq, l, w_gla, w_ml, w_lr, w_gates_t, bias_r, wlr, blr, gla_norm, mlstm_norm, w_out,
           hps=HEADS, out_tiles=(1024, 512)):
    m, d = x.shape
    half = d // 2
    dv = half // HEADS
    dk = dv // 2
    qk = HEADS * dk
    proj_g = _matmul(xn, w_gla, l, BF16, name="gla_in_proj").reshape(bsz, seq, -1)
    proj_m = _matmul(xn, w_ml, l, BF16, name="mlstm_in_proj").reshape(bsz, seq, -1)
    lr, gates_t = _gate_proj(xn, w_lr, w_gates_t, l)
    lr = lr.reshape(bsz, seq, LANE)

    kw = dict(dk=dk, dv=dv, col_q=0, col_k=qk, col_v=2 * qk, hps=hps)
    g_b = _gla_scan(proj_g, lr, wlr, blr, l, reverse=True, **kw)
    g_out = _gla_scan(proj_g, lr, wlr, blr, l, reverse=False, col_gate=2 * qk + half, other=g_b,
                      norm_w=gla_norm, **kw)
    m_b = _mlstm_scan(proj_m, gates_t, bias_r, l, reverse=True, **kw)
    m_out = _mlstm_scan(proj_m, gates_t, bias_r, l, reverse=False, col_gate=2 * qk + half,
                        other=m_b, norm_w=mlstm_norm, **kw)
    return _out_proj(g_out.reshape(m, half), m_out.reshape(m, half), w_out, l, x,
                     tm=out_tiles[0], tn=out_tiles[1])


def kernel(x_prompt, x_sample, mem_prompt, mem_sample, ffn1_norm, ffn1_w_gate, ffn1_w_up, ffn1_w_down, mix_norm, w_in, gla_w_lr, gla_b_lr, gla_out_norm, mlstm_gate_b, mlstm_out_norm, w_out, xattn_norm, mem_norm, xattn_wq, xattn_wk, xattn_wv, xattn_wo, ffn2_norm, ffn2_w_gate, ffn2_w_up, ffn2_w_down, final_norm):
    assert x_prompt.shape[1:] == x_sample.shape[1:] and mem_prompt.shape[1:] == mem_sample.shape[1:]
    n_p, seq, d = x_prompt.shape
    n_s = x_sample.shape[0]
    bsz = n_p + n_s
    m = bsz * seq
    x = (x_prompt.reshape(n_p * seq, d), x_sample.reshape(n_s * seq, d))
    mem = jnp.concatenate([mem_prompt, mem_sample], axis=0)
    n_mem = mem.shape[1]
    mem = mem.reshape(bsz * n_mem, d)
    depth = ffn1_norm.shape[0]

    ffn1 = (ffn1_w_gate.astype(BF16), ffn1_w_up.astype(BF16), ffn1_w_down.astype(BF16))
    ffn2 = (ffn2_w_gate.astype(BF16), ffn2_w_up.astype(BF16), ffn2_w_down.astype(BF16))
    mixer_w = _mixer_weights(d, w_in, gla_w_lr, gla_b_lr, mlstm_gate_b)
    mixer_rest = (_norm_w(gla_out_norm), _norm_w(mlstm_out_norm), w_out.astype(BF16))
    w_kv = jnp.concatenate([xattn_wk, xattn_wv], axis=2).astype(BF16)
    wq = xattn_wq.astype(BF16)
    wo = xattn_wo.astype(BF16)
    ffn1_n, ffn2_n = _norm_w(ffn1_norm), _norm_w(ffn2_norm)
    mix_n, xattn_n, mem_n = _norm_w(mix_norm), _norm_w(xattn_norm), _norm_w(mem_norm)

    xn = _rmsnorm_pair(*x, ffn1_n, 0, BF16)
    for l in range(depth):
        hps, down_tiles, out_tiles = (HEADS, (512, 512), (1024, 512)) if l == 0 else (2, (768, 256), (512, 1024))
        x, xn = _ffn(x, xn, *ffn1, l, next_gain=(mix_n, l), down_tiles=down_tiles)
        x = _mixer(x, xn, bsz, seq, l, *mixer_w, *mixer_rest, hps=hps, out_tiles=out_tiles)
        kv = _matmul(_rmsnorm(mem, mem_n, l, BF16), w_kv, l, BF16, tm=512, name="mem_kv_proj")
        x, xn = _xattn(x.reshape(bsz, seq, d), xattn_n, wq, kv.reshape(bsz, n_mem, -1), wo, ffn2_n, l)
        x, xn = x.reshape(m, d), xn.reshape(m, d)
        if l + 1 < depth:
            x, xn = _ffn(x, xn, *ffn2, l, next_gain=(ffn1_n, l + 1), down_tiles=down_tiles)
        else:
            x = _ffn(x, xn, *ffn2, l, down_tiles=down_tiles)
    fin = _norm_w(final_norm.reshape(1, d))
    y_p = _rmsnorm(x, fin, 0, F32, 0, n_p * seq).reshape(n_p, seq, d)
    y_s = _rmsnorm(x, fin, 0, F32, n_p * seq, n_s * seq).reshape(n_s, seq, d)
    return (y_p, y_s)
```

```python
import functools

import jax
import jax.numpy as jnp
from jax import lax
from jax.experimental import pallas as pl
from jax.experimental.pallas import tpu as pltpu

F32 = jnp.float32
BF16 = jnp.bfloat16

EPS = 1e-6
NEG_BIG = -1e30
HEADS = 4
GLA_RANK = 16
GLA_TAU = 16.0
GLA_MIN_LOG_DECAY = -1.0
GLA_CHUNK = 64
MLSTM_CHUNK = 256
XATTN_HEADS = 4
LANE = 128
SUBLANE = 8
VMEM_LIMIT = 56 * 1024 * 1024

NT_DIMS = (((1,), (1,)), ((), ()))
TN_DIMS = (((0,), (0,)), ((), ()))


def _pick(n, target, mult):
    if n <= target:
        return n
    t = (target // mult) * mult
    while t >= mult:
        if n % t == 0:
            return t
        t -= mult
    return n


def _params(sem):
    return pltpu.CompilerParams(dimension_semantics=sem, vmem_limit_bytes=VMEM_LIMIT)


def _log_sigmoid(x):
    return jnp.minimum(x, 0.0) - jnp.log(1.0 + jnp.exp(-jnp.abs(x)))


def _sigmoid(x):
    return 0.5 * jnp.tanh(0.5 * x) + 0.5


def _split_bf16(x):
    hi = x.astype(BF16)
    return hi, (x - hi.astype(F32)).astype(BF16)


def _dot(a, b):
    return jnp.dot(a, b, preferred_element_type=F32)


def _rmsnorm_body(x_ref, w_ref, o_ref):
    x = x_ref[...]
    ms = jnp.mean(x * x, axis=-1, keepdims=True)
    o_ref[...] = (x * lax.rsqrt(ms + EPS) * w_ref[...]).astype(o_ref.dtype)


def _pair_specs(block, na, col=lambda *g: 0):
    return [pl.BlockSpec(block, lambda *g: (jnp.minimum(g[0], na - 1), col(*g))),
            pl.BlockSpec(block, lambda *g: (jnp.maximum(g[0] - na, 0), col(*g)))]


def _rmsnorm_pair_body(xa_ref, xb_ref, w_ref, o_ref, *, na):
    @pl.when(pl.program_id(0) < na)
    def _():
        _rmsnorm_body(xa_ref, w_ref, o_ref)

    @pl.when(pl.program_id(0) >= na)
    def _():
        _rmsnorm_body(xb_ref, w_ref, o_ref)


def _rmsnorm_pair(xa, xb, w, l, out_dtype):
    d = xa.shape[1]
    tr = _pick(xa.shape[0], 256, SUBLANE * 2)
    assert xa.shape[0] % tr == 0 and xb.shape[0] % tr == 0
    na, nb = xa.shape[0] // tr, xb.shape[0] // tr
    return pl.pallas_call(
        functools.partial(_rmsnorm_pair_body, na=na),
        grid=(na + nb,),
        in_specs=_pair_specs((tr, d), na) + [pl.BlockSpec((None, 1, d), lambda i: (l, 0, 0))],
        out_specs=pl.BlockSpec((tr, d), lambda i: (i, 0)),
        out_shape=jax.ShapeDtypeStruct(((na + nb) * tr, d), out_dtype),
        compiler_params=_params(("parallel",)),
        name="rmsnorm",
    )(xa, xb, w)


def _rmsnorm(x, w, l, out_dtype, row_start=0, rows=None):
    m, d = x.shape
    rows = m if rows is None else rows
    tr = _pick(rows, 256, SUBLANE * 2)
    assert row_start % tr == 0
    off = row_start // tr
    return pl.pallas_call(
        _rmsnorm_body,
        grid=(rows // tr,),
        in_specs=[pl.BlockSpec((tr, d), lambda i: (i + off, 0)),
                  pl.BlockSpec((None, 1, d), lambda i: (l, 0, 0))],
        out_specs=pl.BlockSpec((tr, d), lambda i: (i, 0)),
        out_shape=jax.ShapeDtypeStruct((rows, d), out_dtype),
        compiler_params=_params(("parallel",)),
        name="rmsnorm",
    )(x, w)


def _row_scale(rs_ref, width):
    return jnp.concatenate([rs_ref[...]] * (width // LANE), axis=1)


def _split_act(a):
    return a if isinstance(a, tuple) else (a, None)


def _ss_spec(tm, ss):
    return [] if ss is None else [pl.BlockSpec((tm, LANE), lambda i, *_: (i, 0))]


def _mm_body(a_ref, b_ref, *rest):
    *ss_ref, o_ref = rest
    p = _dot(a_ref[...], b_ref[...])
    if ss_ref:
        p = p * _row_scale(ss_ref[0], p.shape[1])
    o_ref[...] = p.astype(o_ref.dtype)


def _matmul(a, w, l, out_dtype, tm=1024, tn=1024, name="matmul"):
    a, ss = _split_act(a)
    m, k = a.shape
    n = w.shape[2]
    tm = _pick(m, tm, 16)
    tn = _pick(n, tn, LANE)
    return pl.pallas_call(
        _mm_body,
        grid=(m // tm, n // tn),
        in_specs=[pl.BlockSpec((tm, k), lambda i, j: (i, 0)),
                  pl.BlockSpec((None, k, tn), lambda i, j: (l, 0, j))] + _ss_spec(tm, ss),
        out_specs=pl.BlockSpec((tm, tn), lambda i, j: (i, j)),
        out_shape=jax.ShapeDtypeStruct((m, n), out_dtype),
        compiler_params=_params(("parallel", "arbitrary")),
        name=name,
    )(a, w, *([] if ss is None else [ss]))


def _swiglu_body(a_ref, wg_ref, wu_ref, *rest, nsplit):
    *ss_ref, o_ref = rest
    rows = a_ref.shape[0] // nsplit
    for t in range(nsplit):
        r = slice(t * rows, (t + 1) * rows)
        a = a_ref[r, :]
        g = _dot(a, wg_ref[...])
        u = _dot(a, wu_ref[...])
        if ss_ref:
            rs = jnp.concatenate([ss_ref[0][r, :]] * (g.shape[1] // LANE), axis=1)
            g, u = g * rs, u * rs
        o_ref[r, :] = (g * _sigmoid(g) * u).astype(o_ref.dtype)


def _swiglu_up(a, wg, wu, l, tm=2048, tn=256, nsplit=1):
    a, ss = _split_act(a)
    m, k = a.shape
    n = wg.shape[2]
    tm = _pick(m, tm, 16)
    tn = min(tn, n)
    return pl.pallas_call(
        functools.partial(_swiglu_body, nsplit=nsplit),
        grid=(m // tm, pl.cdiv(n, tn)),
        in_specs=[pl.BlockSpec((tm, k), lambda i, j: (i, 0)),
                  pl.BlockSpec((None, k, tn), lambda i, j: (l, 0, j)),
                  pl.BlockSpec((None, k, tn), lambda i, j: (l, 0, j))] + _ss_spec(tm, ss),
        out_specs=pl.BlockSpec((tm, tn), lambda i, j: (i, j)),
        out_shape=jax.ShapeDtypeStruct((m, n), BF16),
        compiler_params=_params(("parallel", "arbitrary")),
        name="swiglu_up",
    )(a, wg, wu, *([] if ss is None else [ss]))


def _mm_res_body(a_ref, b_ref, *rest, scale, na, fuse, width):
    n_res = 1 if na is None else 2
    res_refs, rest = rest[:n_res], rest[n_res:]
    if na is None:
        r = res_refs[0][...]
    else:
        r = jnp.where(pl.program_id(0) < na, res_refs[0][...], res_refs[1][...])
    y = r + scale * _dot(a_ref[...], b_ref[...])
    if not fuse:
        rest[0][...] = y
        return
    gain_ref, o_ref, yw_ref, ss_ref = rest
    o_ref[...] = y
    yw_ref[...] = (y * gain_ref[...]).astype(yw_ref.dtype)
    sq = y * y
    part = sq[:, 0:LANE]
    for t in range(1, sq.shape[1] // LANE):
        part = part + sq[:, t * LANE:(t + 1) * LANE]

    @pl.when(pl.program_id(1) == 0)
    def _():
        ss_ref[...] = part

    @pl.when(pl.program_id(1) > 0)
    def _():
        ss_ref[...] += part

    @pl.when(pl.program_id(1) == pl.num_programs(1) - 1)
    def _():
        tot = jnp.sum(ss_ref[...], axis=1, keepdims=True)
        ss_ref[...] = jnp.broadcast_to(lax.rsqrt(tot * (1.0 / width) + EPS), ss_ref.shape)


def _matmul_residual(a, w, l, res, scale, tm=512, tn=512, next_gain=None, name="matmul_res"):
    m, k = a.shape
    n = w.shape[2]
    tm = _pick(m, tm, 16)
    tn = _pick(n, tn, LANE)
    if isinstance(res, tuple):
        assert res[0].shape[0] % tm == 0 and res[1].shape[0] % tm == 0
        na = res[0].shape[0] // tm
        res_specs = _pair_specs((tm, tn), na, col=lambda i, j: j)
    else:
        na, res = None, (res,)
        res_specs = [pl.BlockSpec((tm, tn), lambda i, j: (i, j))]
    in_specs = [pl.BlockSpec((tm, k), lambda i, j: (i, 0)),
                pl.BlockSpec((None, k, tn), lambda i, j: (l, 0, j))] + res_specs
    out_specs = pl.BlockSpec((tm, tn), lambda i, j: (i, j))
    out_shape = jax.ShapeDtypeStruct((m, n), F32)
    args = [a, w, *res]
    fuse = next_gain is not None
    if fuse:
        gains, gl = next_gain
        in_specs.append(pl.BlockSpec((None, 1, tn), lambda i, j: (gl, 0, j)))
        args.append(gains)
        out_specs = [out_specs, pl.BlockSpec((tm, tn), lambda i, j: (i, j)),
                     pl.BlockSpec((tm, LANE), lambda i, j: (i, 0))]
        out_shape = [out_shape, jax.ShapeDtypeStruct((m, n), BF16), jax.ShapeDtypeStruct((m, LANE), F32)]
    out = pl.pallas_call(
        functools.partial(_mm_res_body, scale=scale, na=na, fuse=fuse, width=n),
        grid=(m // tm, n // tn),
        in_specs=in_specs,
        out_specs=out_specs,
        out_shape=out_shape,
        compiler_params=_params(("parallel", "arbitrary")),
        name=name,
    )(*args)
    return (out[0], (out[1], out[2])) if fuse else out


def _mm2_res_body(a1_ref, a2_ref, b1_ref, b2_ref, r_ref, o_ref):
    o_ref[...] = r_ref[...] + (_dot(a1_ref[...], b1_ref[...]) + _dot(a2_ref[...], b2_ref[...]))


def _out_proj(a1, a2, w, l, res, tm=1024, tn=512):
    m, k1 = a1.shape
    k2 = a2.shape[1]
    assert k1 == k2 and w.shape[1] == k1 + k2
    n = w.shape[2]
    tm = _pick(m, tm, 16)
    tn = _pick(n, tn, LANE)
    return pl.pallas_call(
        _mm2_res_body,
        grid=(m // tm, n // tn),
        in_specs=[pl.BlockSpec((tm, k1), lambda i, j: (i, 0)),
                  pl.BlockSpec((tm, k2), lambda i, j: (i, 0)),
                  pl.BlockSpec((None, k1, tn), lambda i, j: (l, 0, j)),
                  pl.BlockSpec((None, k2, tn), lambda i, j: (l, 1, j)),
                  pl.BlockSpec((tm, tn), lambda i, j: (i, j))],
        out_specs=pl.BlockSpec((tm, tn), lambda i, j: (i, j)),
        out_shape=jax.ShapeDtypeStruct((m, n), F32),
        compiler_params=_params(("parallel", "arbitrary")),
        name="mixer_out_proj",
    )(a1, a2, w, w, res)


def _gate_proj_body(a_ref, w_ref, wt_ref, *rest):
    *ss_ref, o_ref, ot_ref = rest
    a = a_ref[...]
    o = _dot(a, w_ref[...])
    ot = lax.dot_general(wt_ref[...], a, NT_DIMS, preferred_element_type=F32)
    if ss_ref:
        o = o * _row_scale(ss_ref[0], o.shape[1])
        ot = ot * jnp.transpose(ss_ref[0][...])[0:1, :]
    o_ref[...] = o
    ot_ref[...] = ot


def _gate_proj(a, w, wt, l, tm=1024):
    a, ss = _split_act(a)
    m, k = a.shape
    n = w.shape[2]
    nt = wt.shape[1]
    tm = _pick(m, tm, LANE)
    return pl.pallas_call(
        _gate_proj_body,
        grid=(m // tm,),
        in_specs=[pl.BlockSpec((tm, k), lambda i: (i, 0)),
                  pl.BlockSpec((None, k, n), lambda i: (l, 0, 0)),
                  pl.BlockSpec((None, nt, k), lambda i: (l, 0, 0))] + _ss_spec(tm, ss),
        out_specs=[pl.BlockSpec((tm, n), lambda i: (i, 0)),
                   pl.BlockSpec((nt, tm), lambda i: (0, i))],
        out_shape=[jax.ShapeDtypeStruct((m, n), F32),
                   jax.ShapeDtypeStruct((nt, m), F32)],
        compiler_params=_params(("parallel",)),
        name="gate_proj",
    )(a, w, wt, *([] if ss is None else [ss]))


def _causal_mask(chunk, reverse):
    r = lax.broadcasted_iota(jnp.int32, (chunk, chunk), 0)
    c = lax.broadcasted_iota(jnp.int32, (chunk, chunk), 1)
    return (c >= r) if reverse else (c <= r)


def _head_norm_gate(h, other, gate, nw):
    h = h + other.astype(F32)
    return h * lax.rsqrt(jnp.mean(h * h, axis=-1, keepdims=True) + EPS) * nw * gate


def _gla_body(q_ref, k_ref, v_ref, lr_ref, wlr_ref, blr_ref, *rest, reverse, final, lb, hps):
    if final:
        gg_ref, ob_ref, nw_ref, o_ref, s_ref = rest
    else:
        o_ref, s_ref = rest
    dk = q_ref.shape[-1] // hps
    dv = v_ref.shape[-1] // hps
    chunk = GLA_CHUNK
    nch = lb // chunk
    qk = hps * dk

    @pl.when(pl.program_id(2) == 0)
    def _():
        s_ref[...] = jnp.zeros_like(s_ref)

    lh, ll = _split_bf16(lr_ref[...])
    wh, wl = _split_bf16(wlr_ref[...])
    z = _dot(lh, wh) + _dot(lh, wl) + _dot(ll, wh) + blr_ref[...]
    g = jnp.maximum(_log_sigmoid(z) * (1.0 / GLA_TAU), GLA_MIN_LOG_DECAY)
    mask = _causal_mask(chunk, reverse)
    tri = jnp.where(mask, 1.0, 0.0).astype(BF16)
    gh, gl = _split_bf16(jnp.concatenate([g[c * chunk:(c + 1) * chunk, :] for c in range(nch)], axis=1))
    bcum = _dot(tri, gh) + _dot(tri, gl)

    scale = dk ** -0.5
    zero_blk = jnp.zeros((chunk, chunk), BF16)
    npair = nch // 2
    for pb in (range(npair - 1, -1, -1) if reverse else range(npair)):
        rows = slice(2 * pb * chunk, (2 * pb + 2) * chunk)
        c_first, c_second = (2 * pb + 1, 2 * pb) if reverse else (2 * pb, 2 * pb + 1)
        for hh in range(hps):
            kc = slice(hh * dk, (hh + 1) * dk)
            vc = slice(hh * dv, (hh + 1) * dv)

            def chunk_terms(c):
                b = bcum[:, c * qk + hh * dk:c * qk + (hh + 1) * dk]
                tot = b[0:1, :] if reverse else b[chunk - 1:chunk, :]
                r = slice(c * chunk, (c + 1) * chunk)
                q_dec = q_ref[r, kc].astype(F32) * (jnp.exp(b) * scale)
                k_inv = k_ref[r, kc].astype(F32) * jnp.exp(-b)
                return tot, q_dec, k_inv, k_inv * jnp.exp(tot)

            tot1, qd1, ki1, ke1 = chunk_terms(c_first)
            tot2, qd2, ki2, ke2 = chunk_terms(c_second)
            qd1b, qd2b = qd1.astype(BF16), qd2.astype(BF16)
            a11 = lax.dot_general(qd1b, ki1.astype(BF16), NT_DIMS, preferred_element_type=F32)
            a22 = lax.dot_general(qd2b, ki2.astype(BF16), NT_DIMS, preferred_element_type=F32)
            a21 = lax.dot_general(qd2b, ke1.astype(BF16), NT_DIMS, preferred_element_type=F32)
            a11 = jnp.where(mask, a11, 0.0).astype(BF16)
            a22 = jnp.where(mask, a22, 0.0).astype(BF16)
            a21 = a21.astype(BF16)
            qs2 = (qd2 * jnp.exp(tot1)).astype(BF16)
            kx1 = (ke1 * jnp.exp(tot2)).astype(BF16)
            ke2b = ke2.astype(BF16)
            if reverse:
                amat = jnp.concatenate([jnp.concatenate([a22, a21], axis=1),
                                        jnp.concatenate([zero_blk, a11], axis=1)], axis=0)
                qmat = jnp.concatenate([qs2, qd1b], axis=0)
                kmat = jnp.concatenate([ke2b, kx1], axis=0)
            else:
                amat = jnp.concatenate([jnp.concatenate([a11, zero_blk], axis=1),
                                        jnp.concatenate([a21, a22], axis=1)], axis=0)
                qmat = jnp.concatenate([qd1b, qs2], axis=0)
                kmat = jnp.concatenate([kx1, ke2b], axis=0)
            v = v_ref[rows, vc]
            s = s_ref[hh]
            o = _dot(amat, v) + _dot(qmat, s.astype(BF16))
            dec = jnp.exp(jnp.transpose(jnp.broadcast_to(tot1 + tot2, (LANE, dk))))
            dec = jnp.concatenate([dec] * (dv // LANE), axis=1)
            s_ref[hh] = s * dec + lax.dot_general(kmat, v, TN_DIMS, preferred_element_type=F32)
            if final:
                gg = gg_ref[rows, vc].astype(F32)
                o = _head_norm_gate(o, ob_ref[rows, vc], gg * _sigmoid(gg), nw_ref[:, vc])
            o_ref[rows, vc] = o.astype(o_ref.dtype)


def _gla_scan(proj, lr, wlr, blr, l, *, reverse, dk, dv, col_q, col_k, col_v, col_gate=None,
              other=None, norm_w=None, hps=HEADS):
    bsz, seq, _ = proj.shape
    lb = _pick(seq, 512, LANE)
    assert lb % (2 * GLA_CHUNK) == 0 and HEADS % hps == 0
    nblk = seq // lb
    final = other is not None
    qk, width = hps * dk, hps * dv

    def tok(i):
        return (nblk - 1 - i) if reverse else i

    def col_spec(w, col0):
        return pl.BlockSpec((None, lb, w), lambda b, h, i: (b, tok(i), col0 // w + h))

    d = 1 if reverse else 0
    in_specs = [col_spec(qk, col_q), col_spec(qk, col_k), col_spec(width, col_v),
                pl.BlockSpec((None, lb, LANE), lambda b, h, i: (b, tok(i), 0)),
                pl.BlockSpec((None, None, LANE, qk), lambda b, h, i: (l, d, 0, h)),
                pl.BlockSpec((None, None, 1, qk), lambda b, h, i: (l, d, 0, h))]
    args = [proj, proj, proj, lr, wlr, blr]
    if final:
        in_specs += [col_spec(width, col_gate),
                     pl.BlockSpec((None, lb, width), lambda b, h, i: (b, tok(i), h)),
                     pl.BlockSpec((None, 1, width), lambda b, h, i: (l, 0, h))]
        args += [proj, other, norm_w]
    return pl.pallas_call(
        functools.partial(_gla_body, reverse=reverse, final=final, lb=lb, hps=hps),
        grid=(bsz, HEADS // hps, nblk),
        in_specs=in_specs,
        out_specs=pl.BlockSpec((None, lb, width), lambda b, h, i: (b, tok(i), h)),
        out_shape=jax.ShapeDtypeStruct((bsz, seq, HEADS * dv), BF16),
        scratch_shapes=[pltpu.VMEM((hps, dk, dv), F32)],
        compiler_params=_params(("parallel", "parallel", "arbitrary")),
        name="gla_fwd" if final else "gla_bwd",
    )(*args)


def _segment_cumsum_lanes(x, seg, reverse):
    n = x.shape[-1]
    pos = lax.broadcasted_iota(jnp.int32, x.shape, x.ndim - 1) % seg
    s = 1
    while s < seg:
        if reverse:
            x = x + jnp.where(pos < seg - s, pltpu.roll(x, n - s, x.ndim - 1), 0.0)
        else:
            x = x + jnp.where(pos >= s, pltpu.roll(x, s, x.ndim - 1), 0.0)
        s *= 2
    return x


def _mlstm_body(q_ref, k_ref, v_ref, gr_ref, br_ref, *rest, reverse, final, lb, hps):
    if final:
        mo_ref, hb_ref, nw_ref, o_ref, c_ref, m_ref = rest
    else:
        o_ref, c_ref, m_ref = rest
    dk = q_ref.shape[-1] // hps
    dv = v_ref.shape[-1] // hps
    chunk = min(MLSTM_CHUNK, lb)

    @pl.when(pl.program_id(2) == 0)
    def _():
        c_ref[...] = jnp.zeros_like(c_ref)
        m_ref[...] = jnp.full_like(m_ref, NEG_BIG)

    ti, tf = (2, 3) if reverse else (0, 1)
    gr = gr_ref[...] + br_ref[:, 0:1]
    bcum = _segment_cumsum_lanes(_log_sigmoid(gr), chunk, reverse)
    row_id = lax.broadcasted_iota(jnp.int32, gr.shape, 0) % SUBLANE
    rowform = jnp.where(row_id == tf, bcum, gr)
    pad_rows = jnp.zeros((LANE - SUBLANE, chunk), F32)

    mask = _causal_mask(chunk, reverse)
    ones_col = jnp.where(lax.broadcasted_iota(jnp.int32, (chunk, LANE), 1) == 0, 1.0, 0.0).astype(BF16)
    scale = dk ** -0.5
    nch = lb // chunk
    for ch in (range(nch - 1, -1, -1) if reverse else range(nch)):
        rows = slice(ch * chunk, (ch + 1) * chunk)
        for hh in range(hps):
            kc = slice(hh * dk, (hh + 1) * dk)
            vc = slice(hh * dv, (hh + 1) * dv)
            rf = rowform[hh * SUBLANE:(hh + 1) * SUBLANE, rows]
            u_row = rf[ti:ti + 1, :] - rf[tf:tf + 1, :]
            colform = jnp.transpose(jnp.concatenate([rf, pad_rows], axis=0))
            bc = colform[:, tf:tf + 1]
            ic = colform[:, ti:ti + 1]
            gtot = bc[0:1, :] if reverse else bc[chunk - 1:chunk, :]
            log_d = jnp.where(mask, bc + u_row, NEG_BIG)
            m_intra = jnp.max(log_d, axis=1, keepdims=True)
            q = q_ref[rows, kc]
            k = k_ref[rows, kc]
            v_aug = jnp.concatenate([v_ref[rows, vc], ones_col], axis=1)
            s = lax.dot_general(q, k, NT_DIMS, preferred_element_type=F32) * jnp.exp(log_d - m_intra)
            intra = _dot(s.astype(BF16), v_aug)
            c = c_ref[hh]
            inter = _dot(q, c.astype(BF16))
            m_prev = m_ref[hh, 0:1, 0:1]
            bm = bc + m_prev
            m_j = jnp.maximum(bm, m_intra)
            comb = (scale * jnp.exp(bm - m_j)) * inter + (scale * jnp.exp(m_intra - m_j)) * intra
            denom = jnp.maximum(jnp.abs(comb[:, dv:dv + 1]), jnp.exp(-m_j))
            h = comb[:, :dv] * (1.0 / denom)
            a_col = gtot - bc + ic
            m_new = jnp.maximum(gtot + m_prev, jnp.max(a_col, axis=0, keepdims=True))
            kw = (k.astype(F32) * jnp.exp(a_col - m_new)).astype(BF16)
            c_ref[hh] = (jnp.exp(gtot + m_prev - m_new) * c
                         + lax.dot_general(kw, v_aug, TN_DIMS, preferred_element_type=F32))
            m_ref[hh] = jnp.broadcast_to(m_new, (SUBLANE, LANE))
            if final:
                h = _head_norm_gate(h, hb_ref[rows, vc], _sigmoid(mo_ref[rows, vc].astype(F32)), nw_ref[:, vc])
            o_ref[rows, vc] = h.astype(o_ref.dtype)


def _mlstm_scan(proj, gates_t, bias_r, l, *, reverse, dk, dv, col_q, col_k, col_v,
                col_gate=None, other=None, norm_w=None, hps=HEADS):
    bsz, seq, _ = proj.shape
    lb = _pick(seq, 512, LANE)
    nblk = seq // lb
    final = other is not None
    assert HEADS % hps == 0
    qk, width = hps * dk, hps * dv

    def tok(i):
        return (nblk - 1 - i) if reverse else i

    def col_spec(w, col0):
        return pl.BlockSpec((None, lb, w), lambda b, h, i: (b, tok(i), col0 // w + h))

    in_specs = [col_spec(qk, col_q), col_spec(qk, col_k), col_spec(width, col_v),
                pl.BlockSpec((hps * SUBLANE, lb), lambda b, h, i: (h, b * nblk + tok(i))),
                pl.BlockSpec((None, hps * SUBLANE, LANE), lambda b, h, i: (l, h, 0))]
    args = [proj, proj, proj, gates_t, bias_r]
    if final:
        in_specs += [col_spec(width, col_gate),
                     pl.BlockSpec((None, lb, width), lambda b, h, i: (b, tok(i), h)),
                     pl.BlockSpec((None, 1, width), lambda b, h, i: (l, 0, h))]
        args += [proj, other, norm_w]
    return pl.pallas_call(
        functools.partial(_mlstm_body, reverse=reverse, final=final, lb=lb, hps=hps),
        grid=(bsz, HEADS // hps, nblk),
        in_specs=in_specs,
        out_specs=pl.BlockSpec((None, lb, width), lambda b, h, i: (b, tok(i), h)),
        out_shape=jax.ShapeDtypeStruct((bsz, seq, HEADS * dv), BF16),
        scratch_shapes=[pltpu.VMEM((hps, dk, dv + LANE), F32), pltpu.VMEM((hps, SUBLANE, LANE), F32)],
        compiler_params=_params(("parallel", "parallel", "arbitrary")),
        name="mlstm_fwd" if final else "mlstm_bwd",
    )(*args)


def _xattn_body(x_ref, nw_ref, wq_ref, kv_ref, wo_ref, nw2_ref, o_ref, n_ref, *, heads):
    x = x_ref[...]
    xn = (x * lax.rsqrt(jnp.mean(x * x, axis=-1, keepdims=True) + EPS) * nw_ref[...]).astype(BF16)
    width = wq_ref.shape[1]
    hd = width // heads
    q = (_dot(xn, wq_ref[...]) * (hd ** -0.5)).astype(BF16)
    outs = []
    for h in range(heads):
        kh = kv_ref[:, h * hd:(h + 1) * hd]
        vh = kv_ref[:, width + h * hd:width + (h + 1) * hd]
        s = lax.dot_general(q[:, h * hd:(h + 1) * hd], kh, NT_DIMS, preferred_element_type=F32)
        p = jnp.exp(s - jnp.max(s, axis=-1, keepdims=True))
        p = p * (1.0 / jnp.sum(p, axis=-1, keepdims=True))
        outs.append(_dot(p.astype(BF16), vh))
    o = jnp.concatenate(outs, axis=1).astype(BF16)
    y = x + _dot(o, wo_ref[...])
    o_ref[...] = y
    n_ref[...] = (y * lax.rsqrt(jnp.mean(y * y, axis=-1, keepdims=True) + EPS) * nw2_ref[...]).astype(BF16)


def _xattn(x, norm_w, wq, kv, wo, next_norm_w, l, tq=512):
    bsz, seq, d = x.shape
    mem = kv.shape[1]
    width = wq.shape[2]
    tq = _pick(seq, tq, 16)
    return pl.pallas_call(
        functools.partial(_xattn_body, heads=XATTN_HEADS),
        grid=(bsz, seq // tq),
        in_specs=[pl.BlockSpec((None, tq, d), lambda b, i: (b, i, 0)),
                  pl.BlockSpec((None, 1, d), lambda b, i: (l, 0, 0)),
                  pl.BlockSpec((None, d, width), lambda b, i: (l, 0, 0), pipeline_mode=pl.Buffered(1)),
                  pl.BlockSpec((None, mem, 2 * width), lambda b, i: (b, 0, 0)),
                  pl.BlockSpec((None, width, d), lambda b, i: (l, 0, 0), pipeline_mode=pl.Buffered(1)),
                  pl.BlockSpec((None, 1, d), lambda b, i: (l, 0, 0))],
        out_specs=[pl.BlockSpec((None, tq, d), lambda b, i: (b, i, 0)),
                   pl.BlockSpec((None, tq, d), lambda b, i: (b, i, 0))],
        out_shape=[jax.ShapeDtypeStruct((bsz, seq, d), F32),
                   jax.ShapeDtypeStruct((bsz, seq, d), BF16)],
        compiler_params=_params(("parallel", "arbitrary")),
        name="mem_xattn",
    )(x, norm_w, wq, kv, wo, next_norm_w)


def _norm_w(w):
    return w.astype(F32).reshape(w.shape[0], 1, w.shape[1])


def _ffn(x, xn, w_gate, w_up, w_down, l, next_gain=None, down_tiles=(512, 512), up_split=1):
    hid = _swiglu_up(xn, w_gate, w_up, l, nsplit=up_split)
    tm, tn = down_tiles
    return _matmul_residual(hid, w_down, l, x, 0.5, tm=tm, tn=tn, next_gain=next_gain, name="ffn_down")


def _split_w_in_body(w_ref, g_ref, m_ref, *, o_m, width):
    w = w_ref[...]
    g_ref[...] = w[:, :width].astype(BF16)
    m_ref[...] = w[:, o_m:o_m + width].astype(BF16)


def _split_w_in(w_in, o_m, width):
    depth, d, n = w_in.shape
    tr = _pick(d, 128, SUBLANE * 2)
    return pl.pallas_call(
        functools.partial(_split_w_in_body, o_m=o_m, width=width),
        grid=(depth, d // tr),
        in_specs=[pl.BlockSpec((None, tr, n), lambda l, i: (l, i, 0))],
        out_specs=[pl.BlockSpec((None, tr, width), lambda l, i: (l, i, 0)),
                   pl.BlockSpec((None, tr, width), lambda l, i: (l, i, 0))],
        out_shape=[jax.ShapeDtypeStruct((depth, d, width), BF16)] * 2,
        compiler_params=_params(("parallel", "parallel")),
        name="split_w_in",
    )(w_in)


def _mixer_weights(d, w_in, gla_w_lr, gla_b_lr, mlstm_gate_b):
    depth = w_in.shape[0]
    half = d // 2
    qk = half // 2
    o_lr = 2 * qk + 2 * half
    o_m = o_lr + 2 * GLA_RANK
    o_mg = o_m + 2 * qk + 2 * half
    w_gla, w_ml = _split_w_in(w_in, o_m, o_lr)
    w_lr = jnp.pad(w_in[:, :, o_lr:o_m], ((0, 0), (0, 0), (0, LANE - 2 * GLA_RANK))).astype(BF16)
    mg = w_in[:, :, o_mg:o_mg + 4 * HEADS].reshape(depth, d, 4, HEADS).transpose(0, 3, 2, 1)
    w_gates_t = jnp.pad(mg, ((0, 0), (0, 0), (0, SUBLANE - 4), (0, 0))).reshape(depth, HEADS * SUBLANE, d)
    w_gates_t = w_gates_t.astype(BF16)
    bias_r = jnp.pad(mlstm_gate_b.astype(F32).transpose(0, 2, 1), ((0, 0), (0, 0), (0, SUBLANE - 4)))
    bias_r = jnp.broadcast_to(bias_r.reshape(depth, HEADS * SUBLANE, 1), (depth, HEADS * SUBLANE, LANE))
    wlr = jnp.stack([jnp.pad(gla_w_lr[:, 0], ((0, 0), (0, LANE - GLA_RANK), (0, 0))),
                     jnp.pad(gla_w_lr[:, 1], ((0, 0), (GLA_RANK, LANE - 2 * GLA_RANK), (0, 0)))],
                    axis=1).astype(F32)
    blr = gla_b_lr.astype(F32).reshape(depth, 2, 1, qk)
    return w_gla, w_ml, w_lr, w_gates_t, bias_r, wlr, blr


def _mixer(x, xn, bsz, seq, l, w_gla, w_ml, w_lr, w_gates_t, bias_r, wlr, blr, gla_norm, mlstm_norm, w_out,
           hps=HEADS, out_tiles=(1024, 512)):
    m, d = x.shape
    half = d // 2
    dv = half // HEADS
    dk = dv // 2
    qk = HEADS * dk
    proj_g = _matmul(xn, w_gla, l, BF16, name="gla_in_proj").reshape(bsz, seq, -1)
    proj_m = _matmul(xn, w_ml, l, BF16, name="mlstm_in_proj").reshape(bsz, seq, -1)
    lr, gates_t = _gate_proj(xn, w_lr, w_gates_t, l)
    lr = lr.reshape(bsz, seq, LANE)

    kw = dict(dk=dk, dv=dv, col_q=0, col_k=qk, col_v=2 * qk, hps=hps)
    g_b = _gla_scan(proj_g, lr, wlr, blr, l, reverse=True, **kw)
    g_out = _gla_scan(proj_g, lr, wlr, blr, l, reverse=False, col_gate=2 * qk + half, other=g_b,
                      norm_w=gla_norm, **kw)
    m_b = _mlstm_scan(proj_m, gates_t, bias_r, l, reverse=True, **kw)
    m_out = _mlstm_scan(proj_m, gates_t, bias_r, l, reverse=False, col_gate=2 * qk + half,
                        other=m_b, norm_w=mlstm_norm, **kw)
    return _out_proj(g_out.reshape(m, half), m_out.reshape(m, half), w_out, l, x,
                     tm=out_tiles[0], tn=out_tiles[1])


def kernel(x_prompt, x_sample, mem_prompt, mem_sample, ffn1_norm, ffn1_w_gate, ffn1_w_up, ffn1_w_down, mix_norm, w_in, gla_w_lr, gla_b_lr, gla_out_norm, mlstm_gate_b, mlstm_out_norm, w_out, xattn_norm, mem_norm, xattn_wq, xattn_wk, xattn_wv, xattn_wo, ffn2_norm, ffn2_w_gate, ffn2_w_up, ffn2_w_down, final_norm):
    assert x_prompt.shape[1:] == x_sample.shape[1:] and mem_prompt.shape[1:] == mem_sample.shape[1:]
    n_p, seq, d = x_prompt.shape
    n_s = x_sample.shape[0]
    bsz = n_p + n_s
    m = bsz * seq
    x = (x_prompt.reshape(n_p * seq, d), x_sample.reshape(n_s * seq, d))
    mem = jnp.concatenate([mem_prompt, mem_sample], axis=0)
    n_mem = mem.shape[1]
    mem = mem.reshape(bsz * n_mem, d)
    depth = ffn1_norm.shape[0]

    ffn1 = (ffn1_w_gate.astype(BF16), ffn1_w_up.astype(BF16), ffn1_w_down.astype(BF16))
    ffn2 = (ffn2_w_gate.astype(BF16), ffn2_w_up.astype(BF16), ffn2_w_down.astype(BF16))
    mixer_w = _mixer_weights(d, w_in, gla_w_lr, gla_b_lr, mlstm_gate_b)
    mixer_rest = (_norm_w(gla_out_norm), _norm_w(mlstm_out_norm), w_out.astype(BF16))
    w_kv = jnp.concatenate([xattn_wk, xattn_wv], axis=2).astype(BF16)
    wq = xattn_wq.astype(BF16)
    wo = xattn_wo.astype(BF16)
    ffn1_n, ffn2_n = _norm_w(ffn1_norm), _norm_w(ffn2_norm)
    mix_n, xattn_n, mem_n = _norm_w(mix_norm), _norm_w(xattn_norm), _norm_w(mem_norm)

    xn = _rmsnorm_pair(*x, ffn1_n, 0, BF16)
    for l in range(depth):
        x, xn = _ffn(x, xn, *ffn1, l, next_gain=(mix_n, l))
        x = _mixer(x, xn, bsz, seq, l, *mixer_w, *mixer_rest)
        kv = _matmul(_rmsnorm(mem, mem_n, l, BF16), w_kv, l, BF16, tm=512, name="mem_kv_proj")
        x, xn = _xattn(x.reshape(bsz, seq, d), xattn_n, wq, kv.reshape(bsz, n_mem, -1), wo, ffn2_n, l)
        x, xn = x.reshape(m, d), xn.reshape(m, d)
        if l + 1 < depth:
            x, xn = _ffn(x, xn, *ffn2, l, next_gain=(ffn1_n, l + 1))
        else:
            x = _ffn(x, xn, *ffn2, l, up_split=4)
    fin = _norm_w(final_norm.reshape(1, d))
    y_p = _rmsnorm(x, fin, 0, F32, 0, n_p * seq).reshape(n_p, seq, d)
    y_s = _rmsnorm(x, fin, 0, F32, n_p * seq, n_s * seq).reshape(n_s, seq, d)
    return (y_p, y_s)
```

```python
import functools

import jax
import jax.numpy as jnp
from jax import lax
from jax.experimental import pallas as pl
from jax.experimental.pallas import tpu as pltpu

F32 = jnp.float32
BF16 = jnp.bfloat16

EPS = 1e-6
NEG_BIG = -1e30
HEADS = 4
GLA_RANK = 16
GLA_TAU = 16.0
GLA_MIN_LOG_DECAY = -1.0
GLA_CHUNK = 64
MLSTM_CHUNK = 256
XATTN_HEADS = 4
LANE = 128
SUBLANE = 8
VMEM_LIMIT = 56 * 1024 * 1024

NT_DIMS = (((1,), (1,)), ((), ()))
TN_DIMS = (((0,), (0,)), ((), ()))


def _pick(n, target, mult):
    if n <= target:
        return n
    t = (target // mult) * mult
    while t >= mult:
        if n % t == 0:
            return t
        t -= mult
    return n


def _params(sem):
    return pltpu.CompilerParams(dimension_semantics=sem, vmem_limit_bytes=VMEM_LIMIT)


def _log_sigmoid(x):
    return jnp.minimum(x, 0.0) - jnp.log(1.0 + jnp.exp(-jnp.abs(x)))


def _sigmoid(x):
    return 0.5 * jnp.tanh(0.5 * x) + 0.5


def _split_bf16(x):
    hi = x.astype(BF16)
    return hi, (x - hi.astype(F32)).astype(BF16)


def _dot(a, b):
    return jnp.dot(a, b, preferred_element_type=F32)


def _rmsnorm_body(x_ref, w_ref, o_ref):
    x = x_ref[...]
    ms = jnp.mean(x * x, axis=-1, keepdims=True)
    o_ref[...] = (x * lax.rsqrt(ms + EPS) * w_ref[...]).astype(o_ref.dtype)


def _pair_specs(block, na, col=lambda *g: 0):
    return [pl.BlockSpec(block, lambda *g: (jnp.minimum(g[0], na - 1), col(*g))),
            pl.BlockSpec(block, lambda *g: (jnp.maximum(g[0] - na, 0), col(*g)))]


def _rmsnorm_pair_body(xa_ref, xb_ref, w_ref, o_ref, *, na):
    @pl.when(pl.program_id(0) < na)
    def _():
        _rmsnorm_body(xa_ref, w_ref, o_ref)

    @pl.when(pl.program_id(0) >= na)
    def _():
        _rmsnorm_body(xb_ref, w_ref, o_ref)


def _rmsnorm_pair(xa, xb, w, l, out_dtype):
    d = xa.shape[1]
    tr = _pick(xa.shape[0], 256, SUBLANE * 2)
    assert xa.shape[0] % tr == 0 and xb.shape[0] % tr == 0
    na, nb = xa.shape[0] // tr, xb.shape[0] // tr
    return pl.pallas_call(
        functools.partial(_rmsnorm_pair_body, na=na),
        grid=(na + nb,),
        in_specs=_pair_specs((tr, d), na) + [pl.BlockSpec((None, 1, d), lambda i: (l, 0, 0))],
        out_specs=pl.BlockSpec((tr, d), lambda i: (i, 0)),
        out_shape=jax.ShapeDtypeStruct(((na + nb) * tr, d), out_dtype),
        compiler_params=_params(("parallel",)),
        name="rmsnorm",
    )(xa, xb, w)


def _rmsnorm(x, w, l, out_dtype, row_start=0, rows=None):
    m, d = x.shape
    rows = m if rows is None else rows
    tr = _pick(rows, 256, SUBLANE * 2)
    assert row_start % tr == 0
    off = row_start // tr
    return pl.pallas_call(
        _rmsnorm_body,
        grid=(rows // tr,),
        in_specs=[pl.BlockSpec((tr, d), lambda i: (i + off, 0)),
                  pl.BlockSpec((None, 1, d), lambda i: (l, 0, 0))],
        out_specs=pl.BlockSpec((tr, d), lambda i: (i, 0)),
        out_shape=jax.ShapeDtypeStruct((rows, d), out_dtype),
        compiler_params=_params(("parallel",)),
        name="rmsnorm",
    )(x, w)


def _row_scale(rs_ref, width):
    return jnp.concatenate([rs_ref[...]] * (width // LANE), axis=1)


def _split_act(a):
    return a if isinstance(a, tuple) else (a, None)


def _ss_spec(tm, ss):
    return [] if ss is None else [pl.BlockSpec((tm, LANE), lambda i, *_: (i, 0))]


def _row_tiles(rows, nsplit):
    step = rows // nsplit
    return [slice(t * step, (t + 1) * step) for t in range(nsplit)]


def _mm_body(a_ref, b_ref, *rest, nsplit):
    *ss_ref, o_ref = rest
    for r in _row_tiles(a_ref.shape[0], nsplit):
        p = _dot(a_ref[r, :], b_ref[...])
        if ss_ref:
            p = p * jnp.concatenate([ss_ref[0][r, :]] * (p.shape[1] // LANE), axis=1)
        o_ref[r, :] = p.astype(o_ref.dtype)


def _matmul(a, w, l, out_dtype, tm=1024, tn=1024, nsplit=1, name="matmul"):
    a, ss = _split_act(a)
    m, k = a.shape
    n = w.shape[2]
    tm = _pick(m, tm, 16)
    tn = _pick(n, tn, LANE)
    return pl.pallas_call(
        functools.partial(_mm_body, nsplit=nsplit),
        grid=(m // tm, n // tn),
        in_specs=[pl.BlockSpec((tm, k), lambda i, j: (i, 0)),
                  pl.BlockSpec((None, k, tn), lambda i, j: (l, 0, j))] + _ss_spec(tm, ss),
        out_specs=pl.BlockSpec((tm, tn), lambda i, j: (i, j)),
        out_shape=jax.ShapeDtypeStruct((m, n), out_dtype),
        compiler_params=_params(("parallel", "arbitrary")),
        name=name,
    )(a, w, *([] if ss is None else [ss]))


def _swiglu_body(a_ref, wg_ref, wu_ref, *rest, nsplit):
    *ss_ref, o_ref = rest
    for r in _row_tiles(a_ref.shape[0], nsplit):
        a = a_ref[r, :]
        g = _dot(a, wg_ref[...])
        u = _dot(a, wu_ref[...])
        if ss_ref:
            rs = jnp.concatenate([ss_ref[0][r, :]] * (g.shape[1] // LANE), axis=1)
            g, u = g * rs, u * rs
        o_ref[r, :] = (g * _sigmoid(g) * u).astype(o_ref.dtype)


def _swiglu_up(a, wg, wu, l, tm=2048, tn=256, nsplit=4):
    a, ss = _split_act(a)
    m, k = a.shape
    n = wg.shape[2]
    tm = _pick(m, tm, 16)
    tn = min(tn, n)
    return pl.pallas_call(
        functools.partial(_swiglu_body, nsplit=nsplit),
        grid=(m // tm, pl.cdiv(n, tn)),
        in_specs=[pl.BlockSpec((tm, k), lambda i, j: (i, 0)),
                  pl.BlockSpec((None, k, tn), lambda i, j: (l, 0, j)),
                  pl.BlockSpec((None, k, tn), lambda i, j: (l, 0, j))] + _ss_spec(tm, ss),
        out_specs=pl.BlockSpec((tm, tn), lambda i, j: (i, j)),
        out_shape=jax.ShapeDtypeStruct((m, n), BF16),
        compiler_params=_params(("parallel", "arbitrary")),
        name="swiglu_up",
    )(a, wg, wu, *([] if ss is None else [ss]))


def _mm_res_body(a_ref, b_ref, *rest, scale, na, fuse, width, nsplit):
    n_res = 1 if na is None else 2
    res_refs, rest = rest[:n_res], rest[n_res:]
    for rt in _row_tiles(a_ref.shape[0], nsplit):
        if na is None:
            r = res_refs[0][rt, :]
        else:
            r = jnp.where(pl.program_id(0) < na, res_refs[0][rt, :], res_refs[1][rt, :])
        y = r + scale * _dot(a_ref[rt, :], b_ref[...])
        if not fuse:
            rest[0][rt, :] = y
            continue
        gain_ref, o_ref, yw_ref, ss_ref = rest
        o_ref[rt, :] = y
        yw_ref[rt, :] = (y * gain_ref[...]).astype(yw_ref.dtype)
        sq = y * y
        part = sq[:, 0:LANE]
        for t in range(1, sq.shape[1] // LANE):
            part = part + sq[:, t * LANE:(t + 1) * LANE]
        @pl.when(pl.program_id(1) == 0)
        def _():
            ss_ref[rt, :] = part

        @pl.when(pl.program_id(1) > 0)
        def _():
            ss_ref[rt, :] += part
    if fuse:
        @pl.when(pl.program_id(1) == pl.num_programs(1) - 1)
        def _():
            tot = jnp.sum(ss_ref[...], axis=1, keepdims=True)
            ss_ref[...] = jnp.broadcast_to(lax.rsqrt(tot * (1.0 / width) + EPS), ss_ref.shape)


def _matmul_residual(a, w, l, res, scale, tm=512, tn=512, next_gain=None, nsplit=1, name="matmul_res"):
    m, k = a.shape
    n = w.shape[2]
    tm = _pick(m, tm, 16)
    tn = _pick(n, tn, LANE)
    if isinstance(res, tuple):
        assert res[0].shape[0] % tm == 0 and res[1].shape[0] % tm == 0
        na = res[0].shape[0] // tm
        res_specs = _pair_specs((tm, tn), na, col=lambda i, j: j)
    else:
        na, res = None, (res,)
        res_specs = [pl.BlockSpec((tm, tn), lambda i, j: (i, j))]
    in_specs = [pl.BlockSpec((tm, k), lambda i, j: (i, 0)),
                pl.BlockSpec((None, k, tn), lambda i, j: (l, 0, j))] + res_specs
    out_specs = pl.BlockSpec((tm, tn), lambda i, j: (i, j))
    out_shape = jax.ShapeDtypeStruct((m, n), F32)
    args = [a, w, *res]
    fuse = next_gain is not None
    if fuse:
        gains, gl = next_gain
        in_specs.append(pl.BlockSpec((None, 1, tn), lambda i, j: (gl, 0, j)))
        args.append(gains)
        out_specs = [out_specs, pl.BlockSpec((tm, tn), lambda i, j: (i, j)),
                     pl.BlockSpec((tm, LANE), lambda i, j: (i, 0))]
        out_shape = [out_shape, jax.ShapeDtypeStruct((m, n), BF16), jax.ShapeDtypeStruct((m, LANE), F32)]
    out = pl.pallas_call(
        functools.partial(_mm_res_body, scale=scale, na=na, fuse=fuse, width=n, nsplit=nsplit),
        grid=(m // tm, n // tn),
        in_specs=in_specs,
        out_specs=out_specs,
        out_shape=out_shape,
        compiler_params=_params(("parallel", "arbitrary")),
        name=name,
    )(*args)
    return (out[0], (out[1], out[2])) if fuse else out


def _mm2_res_body(a1_ref, a2_ref, b1_ref, b2_ref, r_ref, o_ref, *, nsplit):
    for r in _row_tiles(a1_ref.shape[0], nsplit):
        o_ref[r, :] = r_ref[r, :] + (_dot(a1_ref[r, :], b1_ref[...]) + _dot(a2_ref[r, :], b2_ref[...]))


def _out_proj(a1, a2, w, l, res, tm=1024, tn=512, nsplit=1):
    m, k1 = a1.shape
    k2 = a2.shape[1]
    assert k1 == k2 and w.shape[1] == k1 + k2
    n = w.shape[2]
    tm = _pick(m, tm, 16)
    tn = _pick(n, tn, LANE)
    return pl.pallas_call(
        functools.partial(_mm2_res_body, nsplit=nsplit),
        grid=(m // tm, n // tn),
        in_specs=[pl.BlockSpec((tm, k1), lambda i, j: (i, 0)),
                  pl.BlockSpec((tm, k2), lambda i, j: (i, 0)),
                  pl.BlockSpec((None, k1, tn), lambda i, j: (l, 0, j)),
                  pl.BlockSpec((None, k2, tn), lambda i, j: (l, 1, j)),
                  pl.BlockSpec((tm, tn), lambda i, j: (i, j))],
        out_specs=pl.BlockSpec((tm, tn), lambda i, j: (i, j)),
        out_shape=jax.ShapeDtypeStruct((m, n), F32),
        compiler_params=_params(("parallel", "arbitrary")),
        name="mixer_out_proj",
    )(a1, a2, w, w, res)


def _gate_proj_body(a_ref, w_ref, wt_ref, *rest):
    *ss_ref, o_ref, ot_ref = rest
    a = a_ref[...]
    o = _dot(a, w_ref[...])
    ot = lax.dot_general(wt_ref[...], a, NT_DIMS, preferred_element_type=F32)
    if ss_ref:
        o = o * _row_scale(ss_ref[0], o.shape[1])
        ot = ot * jnp.transpose(ss_ref[0][...])[0:1, :]
    o_ref[...] = o
    ot_ref[...] = ot


def _gate_proj(a, w, wt, l, tm=1024):
    a, ss = _split_act(a)
    m, k = a.shape
    n = w.shape[2]
    nt = wt.shape[1]
    tm = _pick(m, tm, LANE)
    return pl.pallas_call(
        _gate_proj_body,
        grid=(m // tm,),
        in_specs=[pl.BlockSpec((tm, k), lambda i: (i, 0)),
                  pl.BlockSpec((None, k, n), lambda i: (l, 0, 0)),
                  pl.BlockSpec((None, nt, k), lambda i: (l, 0, 0))] + _ss_spec(tm, ss),
        out_specs=[pl.BlockSpec((tm, n), lambda i: (i, 0)),
                   pl.BlockSpec((nt, tm), lambda i: (0, i))],
        out_shape=[jax.ShapeDtypeStruct((m, n), F32),
                   jax.ShapeDtypeStruct((nt, m), F32)],
        compiler_params=_params(("parallel",)),
        name="gate_proj",
    )(a, w, wt, *([] if ss is None else [ss]))


def _causal_mask(chunk, reverse):
    r = lax.broadcasted_iota(jnp.int32, (chunk, chunk), 0)
    c = lax.broadcasted_iota(jnp.int32, (chunk, chunk), 1)
    return (c >= r) if reverse else (c <= r)


def _head_norm_gate(h, other, gate, nw):
    h = h + other
    return h * lax.rsqrt(jnp.mean(h * h, axis=-1, keepdims=True) + EPS) * nw * gate


def _gla_body(q_ref, k_ref, v_ref, lr_ref, wlr_ref, blr_ref, *rest, reverse, final, lb, hps):
    if final:
        gg_ref, ob_ref, nw_ref, o_ref, s_ref = rest
    else:
        o_ref, s_ref = rest
    dk = q_ref.shape[-1] // hps
    dv = v_ref.shape[-1] // hps
    chunk = GLA_CHUNK
    nch = lb // chunk
    qk = hps * dk

    @pl.when(pl.program_id(2) == 0)
    def _():
        s_ref[...] = jnp.zeros_like(s_ref)

    lh, ll = _split_bf16(lr_ref[...])
    wh, wl = _split_bf16(wlr_ref[...])
    z = _dot(lh, wh) + _dot(lh, wl) + _dot(ll, wh) + blr_ref[...]
    g = jnp.maximum(_log_sigmoid(z) * (1.0 / GLA_TAU), GLA_MIN_LOG_DECAY)
    mask = _causal_mask(chunk, reverse)
    tri = jnp.where(mask, 1.0, 0.0).astype(BF16)
    gh, gl = _split_bf16(jnp.concatenate([g[c * chunk:(c + 1) * chunk, :] for c in range(nch)], axis=1))
    bcum = _dot(tri, gh) + _dot(tri, gl)

    scale = dk ** -0.5
    zero_blk = jnp.zeros((chunk, chunk), BF16)
    npair = nch // 2
    for pb in (range(npair - 1, -1, -1) if reverse else range(npair)):
        rows = slice(2 * pb * chunk, (2 * pb + 2) * chunk)
        c_first, c_second = (2 * pb + 1, 2 * pb) if reverse else (2 * pb, 2 * pb + 1)
        for hh in range(hps):
            kc = slice(hh * dk, (hh + 1) * dk)
            vc = slice(hh * dv, (hh + 1) * dv)

            def chunk_terms(c):
                b = bcum[:, c * qk + hh * dk:c * qk + (hh + 1) * dk]
                tot = b[0:1, :] if reverse else b[chunk - 1:chunk, :]
                r = slice(c * chunk, (c + 1) * chunk)
                q_dec = q_ref[r, kc].astype(F32) * (jnp.exp(b) * scale)
                k_inv = k_ref[r, kc].astype(F32) * jnp.exp(-b)
                return tot, q_dec, k_inv, k_inv * jnp.exp(tot)

            tot1, qd1, ki1, ke1 = chunk_terms(c_first)
            tot2, qd2, ki2, ke2 = chunk_terms(c_second)
            qd1b, qd2b = qd1.astype(BF16), qd2.astype(BF16)
            a11 = lax.dot_general(qd1b, ki1.astype(BF16), NT_DIMS, preferred_element_type=F32)
            a22 = lax.dot_general(qd2b, ki2.astype(BF16), NT_DIMS, preferred_element_type=F32)
            a21 = lax.dot_general(qd2b, ke1.astype(BF16), NT_DIMS, preferred_element_type=F32)
            a11 = jnp.where(mask, a11, 0.0).astype(BF16)
            a22 = jnp.where(mask, a22, 0.0).astype(BF16)
            a21 = a21.astype(BF16)
            qs2 = (qd2 * jnp.exp(tot1)).astype(BF16)
            kx1 = (ke1 * jnp.exp(tot2)).astype(BF16)
            ke2b = ke2.astype(BF16)
            if reverse:
                amat = jnp.concatenate([jnp.concatenate([a22, a21], axis=1),
                                        jnp.concatenate([zero_blk, a11], axis=1)], axis=0)
                qmat = jnp.concatenate([qs2, qd1b], axis=0)
                kmat = jnp.concatenate([ke2b, kx1], axis=0)
            else:
                amat = jnp.concatenate([jnp.concatenate([a11, zero_blk], axis=1),
                                        jnp.concatenate([a21, a22], axis=1)], axis=0)
                qmat = jnp.concatenate([qd1b, qs2], axis=0)
                kmat = jnp.concatenate([kx1, ke2b], axis=0)
            v = v_ref[rows, vc]
            s = s_ref[hh]
            o = _dot(amat, v) + _dot(qmat, s.astype(BF16))
            dec = jnp.exp(jnp.transpose(jnp.broadcast_to(tot1 + tot2, (LANE, dk))))
            dec = jnp.concatenate([dec] * (dv // LANE), axis=1)
            s_ref[hh] = s * dec + lax.dot_general(kmat, v, TN_DIMS, preferred_element_type=F32)
            if final:
                gg = gg_ref[rows, vc].astype(F32)
                o = _head_norm_gate(o, ob_ref[rows, vc], gg * _sigmoid(gg), nw_ref[:, vc])
            o_ref[rows, vc] = o.astype(o_ref.dtype)


def _gla_scan(proj, lr, wlr, blr, l, *, reverse, dk, dv, col_q, col_k, col_v, col_gate=None,
              other=None, norm_w=None, hps=HEADS):
    bsz, seq, _ = proj.shape
    lb = _pick(seq, 512, LANE)
    assert lb % (2 * GLA_CHUNK) == 0 and HEADS % hps == 0
    nblk = seq // lb
    final = other is not None
    qk, width = hps * dk, hps * dv

    def tok(i):
        return (nblk - 1 - i) if reverse else i

    def col_spec(w, col0):
        return pl.BlockSpec((None, lb, w), lambda b, h, i: (b, tok(i), col0 // w + h))

    d = 1 if reverse else 0
    in_specs = [col_spec(qk, col_q), col_spec(qk, col_k), col_spec(width, col_v),
                pl.BlockSpec((None, lb, LANE), lambda b, h, i: (b, tok(i), 0)),
                pl.BlockSpec((None, None, LANE, qk), lambda b, h, i: (l, d, 0, h)),
                pl.BlockSpec((None, None, 1, qk), lambda b, h, i: (l, d, 0, h))]
    args = [proj, proj, proj, lr, wlr, blr]
    if final:
        in_specs += [col_spec(width, col_gate),
                     pl.BlockSpec((None, lb, width), lambda b, h, i: (b, tok(i), h)),
                     pl.BlockSpec((None, 1, width), lambda b, h, i: (l, 0, h))]
        args += [proj, other, norm_w]
    return pl.pallas_call(
        functools.partial(_gla_body, reverse=reverse, final=final, lb=lb, hps=hps),
        grid=(bsz, HEADS // hps, nblk),
        in_specs=in_specs,
        out_specs=pl.BlockSpec((None, lb, width), lambda b, h, i: (b, tok(i), h)),
        out_shape=jax.ShapeDtypeStruct((bsz, seq, HEADS * dv), BF16 if final else F32),
        scratch_shapes=[pltpu.VMEM((hps, dk, dv), F32)],
        compiler_params=_params(("parallel", "parallel", "arbitrary")),
        name="gla_fwd" if final else "gla_bwd",
    )(*args)


def _segment_cumsum_lanes(x, seg, reverse):
    n = x.shape[-1]
    pos = lax.broadcasted_iota(jnp.int32, x.shape, x.ndim - 1) % seg
    s = 1
    while s < seg:
        if reverse:
            x = x + jnp.where(pos < seg - s, pltpu.roll(x, n - s, x.ndim - 1), 0.0)
        else:
            x = x + jnp.where(pos >= s, pltpu.roll(x, s, x.ndim - 1), 0.0)
        s *= 2
    return x


def _mlstm_body(q_ref, k_ref, v_ref, gr_ref, br_ref, *rest, reverse, final, lb, hps):
    if final:
        mo_ref, hb_ref, nw_ref, o_ref, c_ref, m_ref = rest
    else:
        o_ref, c_ref, m_ref = rest
    dk = q_ref.shape[-1] // hps
    dv = v_ref.shape[-1] // hps
    chunk = min(MLSTM_CHUNK, lb)

    @pl.when(pl.program_id(2) == 0)
    def _():
        c_ref[...] = jnp.zeros_like(c_ref)
        m_ref[...] = jnp.full_like(m_ref, NEG_BIG)

    ti, tf = (2, 3) if reverse else (0, 1)
    gr = gr_ref[...] + br_ref[:, 0:1]
    bcum = _segment_cumsum_lanes(_log_sigmoid(gr), chunk, reverse)
    row_id = lax.broadcasted_iota(jnp.int32, gr.shape, 0) % SUBLANE
    rowform = jnp.where(row_id == tf, bcum, gr)
    pad_rows = jnp.zeros((LANE - SUBLANE, chunk), F32)

    mask = _causal_mask(chunk, reverse)
    ones_col = jnp.where(lax.broadcasted_iota(jnp.int32, (chunk, LANE), 1) == 0, 1.0, 0.0).astype(BF16)
    scale = dk ** -0.5
    nch = lb // chunk
    for ch in (range(nch - 1, -1, -1) if reverse else range(nch)):
        rows = slice(ch * chunk, (ch + 1) * chunk)
        for hh in range(hps):
            kc = slice(hh * dk, (hh + 1) * dk)
            vc = slice(hh * dv, (hh + 1) * dv)
            rf = rowform[hh * SUBLANE:(hh + 1) * SUBLANE, rows]
            u_row = rf[ti:ti + 1, :] - rf[tf:tf + 1, :]
            colform = jnp.transpose(jnp.concatenate([rf, pad_rows], axis=0))
            bc = colform[:, tf:tf + 1]
            ic = colform[:, ti:ti + 1]
            gtot = bc[0:1, :] if reverse else bc[chunk - 1:chunk, :]
            log_d = jnp.where(mask, bc + u_row, NEG_BIG)
            m_intra = jnp.max(log_d, axis=1, keepdims=True)
            q = q_ref[rows, kc]
            k = k_ref[rows, kc]
            v_aug = jnp.concatenate([v_ref[rows, vc], ones_col], axis=1)
            s = lax.dot_general(q, k, NT_DIMS, preferred_element_type=F32) * jnp.exp(log_d - m_intra)
            intra = _dot(s.astype(BF16), v_aug)
            c = c_ref[hh]
            inter = _dot(q, c.astype(BF16))
            m_prev = m_ref[hh, 0:1, 0:1]
            bm = bc + m_prev
            m_j = jnp.maximum(bm, m_intra)
            comb = (scale * jnp.exp(bm - m_j)) * inter + (scale * jnp.exp(m_intra - m_j)) * intra
            denom = jnp.maximum(jnp.abs(comb[:, dv:dv + 1]), jnp.exp(-m_j))
            h = comb[:, :dv] * (1.0 / denom)
            a_col = gtot - bc + ic
            m_new = jnp.maximum(gtot + m_prev, jnp.max(a_col, axis=0, keepdims=True))
            kw = (k.astype(F32) * jnp.exp(a_col - m_new)).astype(BF16)
            c_ref[hh] = (jnp.exp(gtot + m_prev - m_new) * c
                         + lax.dot_general(kw, v_aug, TN_DIMS, preferred_element_type=F32))
            m_ref[hh] = jnp.broadcast_to(m_new, (SUBLANE, LANE))
            if final:
                h = _head_norm_gate(h, hb_ref[rows, vc], _sigmoid(mo_ref[rows, vc].astype(F32)), nw_ref[:, vc])
            o_ref[rows, vc] = h.astype(o_ref.dtype)


def _mlstm_scan(proj, gates_t, bias_r, l, *, reverse, dk, dv, col_q, col_k, col_v,
                col_gate=None, other=None, norm_w=None, hps=HEADS):
    bsz, seq, _ = proj.shape
    lb = _pick(seq, 512, LANE)
    nblk = seq // lb
    final = other is not None
    assert HEADS % hps == 0
    qk, width = hps * dk, hps * dv

    def tok(i):
        return (nblk - 1 - i) if reverse else i

    def col_spec(w, col0):
        return pl.BlockSpec((None, lb, w), lambda b, h, i: (b, tok(i), col0 // w + h))

    in_specs = [col_spec(qk, col_q), col_spec(qk, col_k), col_spec(width, col_v),
                pl.BlockSpec((hps * SUBLANE, lb), lambda b, h, i: (h, b * nblk + tok(i))),
                pl.BlockSpec((None, hps * SUBLANE, LANE), lambda b, h, i: (l, h, 0))]
    args = [proj, proj, proj, gates_t, bias_r]
    if final:
        in_specs += [col_spec(width, col_gate),
                     pl.BlockSpec((None, lb, width), lambda b, h, i: (b, tok(i), h)),
                     pl.BlockSpec((None, 1, width), lambda b, h, i: (l, 0, h))]
        args += [proj, other, norm_w]
    return pl.pallas_call(
        functools.partial(_mlstm_body, reverse=reverse, final=final, lb=lb, hps=hps),
        grid=(bsz, HEADS // hps, nblk),
        in_specs=in_specs,
        out_specs=pl.BlockSpec((None, lb, width), lambda b, h, i: (b, tok(i), h)),
        out_shape=jax.ShapeDtypeStruct((bsz, seq, HEADS * dv), BF16 if final else F32),
        scratch_shapes=[pltpu.VMEM((hps, dk, dv + LANE), F32), pltpu.VMEM((hps, SUBLANE, LANE), F32)],
        compiler_params=_params(("parallel", "parallel", "arbitrary")),
        name="mlstm_fwd" if final else "mlstm_bwd",
    )(*args)


def _xattn_body(x_ref, nw_ref, wq_ref, kv_ref, wo_ref, nw2_ref, o_ref, n_ref, *, heads):
    x = x_ref[...]
    xn = (x * lax.rsqrt(jnp.mean(x * x, axis=-1, keepdims=True) + EPS) * nw_ref[...]).astype(BF16)
    width = wq_ref.shape[1]
    hd = width // heads
    q = (_dot(xn, wq_ref[...]) * (hd ** -0.5)).astype(BF16)
    outs = []
    for h in range(heads):
        kh = kv_ref[:, h * hd:(h + 1) * hd]
        vh = kv_ref[:, width + h * hd:width + (h + 1) * hd]
        s = lax.dot_general(q[:, h * hd:(h + 1) * hd], kh, NT_DIMS, preferred_element_type=F32)
        p = jnp.exp(s - jnp.max(s, axis=-1, keepdims=True))
        p = p * (1.0 / jnp.sum(p, axis=-1, keepdims=True))
        outs.append(_dot(p.astype(BF16), vh))
    o = jnp.concatenate(outs, axis=1).astype(BF16)
    y = x + _dot(o, wo_ref[...])
    o_ref[...] = y
    n_ref[...] = (y * lax.rsqrt(jnp.mean(y * y, axis=-1, keepdims=True) + EPS) * nw2_ref[...]).astype(BF16)


def _xattn(x, norm_w, wq, kv, wo, next_norm_w, l, tq=512):
    bsz, seq, d = x.shape
    mem = kv.shape[1]
    width = wq.shape[2]
    tq = _pick(seq, tq, 16)
    return pl.pallas_call(
        functools.partial(_xattn_body, heads=XATTN_HEADS),
        grid=(bsz, seq // tq),
        in_specs=[pl.BlockSpec((None, tq, d), lambda b, i: (b, i, 0)),
                  pl.BlockSpec((None, 1, d), lambda b, i: (l, 0, 0)),
                  pl.BlockSpec((None, d, width), lambda b, i: (l, 0, 0), pipeline_mode=pl.Buffered(1)),
                  pl.BlockSpec((None, mem, 2 * width), lambda b, i: (b, 0, 0)),
                  pl.BlockSpec((None, width, d), lambda b, i: (l, 0, 0), pipeline_mode=pl.Buffered(1)),
                  pl.BlockSpec((None, 1, d), lambda b, i: (l, 0, 0))],
        out_specs=[pl.BlockSpec((None, tq, d), lambda b, i: (b, i, 0)),
                   pl.BlockSpec((None, tq, d), lambda b, i: (b, i, 0))],
        out_shape=[jax.ShapeDtypeStruct((bsz, seq, d), F32),
                   jax.ShapeDtypeStruct((bsz, seq, d), BF16)],
        compiler_params=_params(("parallel", "arbitrary")),
        name="mem_xattn",
    )(x, norm_w, wq, kv, wo, next_norm_w)


def _norm_w(w):
    return w.astype(F32).reshape(w.shape[0], 1, w.shape[1])


def _ffn(x, xn, w_gate, w_up, w_down, l, next_gain=None, down_split=1):
    hid = _swiglu_up(xn, w_gate, w_up, l)
    return _matmul_residual(hid, w_down, l, x, 0.5, next_gain=next_gain, nsplit=down_split, name="ffn_down")


def _mixer_weights(d, w_in, gla_w_lr, gla_b_lr, mlstm_gate_b):
    depth = w_in.shape[0]
    half = d // 2
    qk = half // 2
    o_lr = 2 * qk + 2 * half
    o_m = o_lr + 2 * GLA_RANK
    o_mg = o_m + 2 * qk + 2 * half
    w_gla = w_in[:, :, :o_lr].astype(BF16)
    w_ml = w_in[:, :, o_m:o_mg].astype(BF16)
    w_lr = jnp.pad(w_in[:, :, o_lr:o_m], ((0, 0), (0, 0), (0, LANE - 2 * GLA_RANK))).astype(BF16)
    mg = w_in[:, :, o_mg:o_mg + 4 * HEADS].reshape(depth, d, 4, HEADS).transpose(0, 3, 2, 1)
    w_gates_t = jnp.pad(mg, ((0, 0), (0, 0), (0, SUBLANE - 4), (0, 0))).reshape(depth, HEADS * SUBLANE, d)
    w_gates_t = w_gates_t.astype(BF16)
    bias_r = jnp.pad(mlstm_gate_b.astype(F32).transpose(0, 2, 1), ((0, 0), (0, 0), (0, SUBLANE - 4)))
    bias_r = jnp.broadcast_to(bias_r.reshape(depth, HEADS * SUBLANE, 1), (depth, HEADS * SUBLANE, LANE))
    wlr = jnp.stack([jnp.pad(gla_w_lr[:, 0], ((0, 0), (0, LANE - GLA_RANK), (0, 0))),
                     jnp.pad(gla_w_lr[:, 1], ((0, 0), (GLA_RANK, LANE - 2 * GLA_RANK), (0, 0)))],
                    axis=1).astype(F32)
    blr = gla_b_lr.astype(F32).reshape(depth, 2, 1, qk)
    return w_gla, w_ml, w_lr, w_gates_t, bias_r, wlr, blr


def _mixer(x, xn, bsz, seq, l, w_gla, w_ml, w_lr, w_gates_t, bias_r, wlr, blr, gla_norm, mlstm_norm, w_out,
           hps=HEADS, proj_split=1):
    m, d = x.shape
    half = d // 2
    dv = half // HEADS
    dk = dv // 2
    qk = HEADS * dk
    proj_g = _matmul(xn, w_gla, l, BF16, nsplit=proj_split, name="gla_in_proj").reshape(bsz, seq, -1)
    proj_m = _matmul(xn, w_ml, l, BF16, nsplit=proj_split, name="mlstm_in_proj").reshape(bsz, seq, -1)
    lr, gates_t = _gate_proj(xn, w_lr, w_gates_t, l)
    lr = lr.reshape(bsz, seq, LANE)

    kw = dict(dk=dk, dv=dv, col_q=0, col_k=qk, col_v=2 * qk, hps=hps)
    g_b = _gla_scan(proj_g, lr, wlr, blr, l, reverse=True, **kw)
    g_out = _gla_scan(proj_g, lr, wlr, blr, l, reverse=False, col_gate=2 * qk + half, other=g_b,
                      norm_w=gla_norm, **kw)
    m_b = _mlstm_scan(proj_m, gates_t, bias_r, l, reverse=True, **kw)
    m_out = _mlstm_scan(proj_m, gates_t, bias_r, l, reverse=False, col_gate=2 * qk + half,
                        other=m_b, norm_w=mlstm_norm, **kw)
    return _out_proj(g_out.reshape(m, half), m_out.reshape(m, half), w_out, l, x, nsplit=proj_split)


def kernel(x_prompt, x_sample, mem_prompt, mem_sample, ffn1_norm, ffn1_w_gate, ffn1_w_up, ffn1_w_down, mix_norm, w_in, gla_w_lr, gla_b_lr, gla_out_norm, mlstm_gate_b, mlstm_out_norm, w_out, xattn_norm, mem_norm, xattn_wq, xattn_wk, xattn_wv, xattn_wo, ffn2_norm, ffn2_w_gate, ffn2_w_up, ffn2_w_down, final_norm):
    assert x_prompt.shape[1:] == x_sample.shape[1:] and mem_prompt.shape[1:] == mem_sample.shape[1:]
    n_p, seq, d = x_prompt.shape
    n_s = x_sample.shape[0]
    bsz = n_p + n_s
    m = bsz * seq
    x = (x_prompt.reshape(n_p * seq, d), x_sample.reshape(n_s * seq, d))
    mem = jnp.concatenate([mem_prompt, mem_sample], axis=0)
    n_mem = mem.shape[1]
    mem = mem.reshape(bsz * n_mem, d)
    depth = ffn1_norm.shape[0]

    ffn1 = (ffn1_w_gate.astype(BF16), ffn1_w_up.astype(BF16), ffn1_w_down.astype(BF16))
    ffn2 = (ffn2_w_gate.astype(BF16), ffn2_w_up.astype(BF16), ffn2_w_down.astype(BF16))
    mixer_w = _mixer_weights(d, w_in, gla_w_lr, gla_b_lr, mlstm_gate_b)
    mixer_rest = (_norm_w(gla_out_norm), _norm_w(mlstm_out_norm), w_out.astype(BF16))
    w_kv = jnp.concatenate([xattn_wk, xattn_wv], axis=2).astype(BF16)
    wq = xattn_wq.astype(BF16)
    wo = xattn_wo.astype(BF16)
    ffn1_n, ffn2_n = _norm_w(ffn1_norm), _norm_w(ffn2_norm)
    mix_n, xattn_n, mem_n = _norm_w(mix_norm), _norm_w(xattn_norm), _norm_w(mem_norm)

    xn = _rmsnorm_pair(*x, ffn1_n, 0, BF16)
    for l in range(depth):
        probe = l == depth - 1
        x, xn = _ffn(x, xn, *ffn1, l, next_gain=(mix_n, l))
        x = _mixer(x, xn, bsz, seq, l, *mixer_w, *mixer_rest, proj_split=4 if probe else 1)
        kv = _matmul(_rmsnorm(mem, mem_n, l, BF16), w_kv, l, BF16, tm=512, name="mem_kv_proj")
        x, xn = _xattn(x.reshape(bsz, seq, d), xattn_n, wq, kv.reshape(bsz, n_mem, -1), wo, ffn2_n, l)
        x, xn = x.reshape(m, d), xn.reshape(m, d)
        if l + 1 < depth:
            x, xn = _ffn(x, xn, *ffn2, l, next_gain=(ffn1_n, l + 1))
        else:
            x = _ffn(x, xn, *ffn2, l, down_split=2)
    fin = _norm_w(final_norm.reshape(1, d))
    y_p = _rmsnorm(x, fin, 0, F32, 0, n_p * seq).reshape(n_p, seq, d)
    y_s = _rmsnorm(x, fin, 0, F32, n_p * seq, n_s * seq).reshape(n_s, seq, d)
    return (y_p, y_s)
```

```python
import functools

import jax
import jax.numpy as jnp
from jax import lax
from jax.experimental import pallas as pl
from jax.experimental.pallas import tpu as pltpu

F32 = jnp.float32
BF16 = jnp.bfloat16

EPS = 1e-6
NEG_BIG = -1e30
HEADS = 4
GLA_RANK = 16
GLA_TAU = 16.0
GLA_MIN_LOG_DECAY = -1.0
GLA_CHUNK = 64
MLSTM_CHUNK = 256
XATTN_HEADS = 4
LANE = 128
SUBLANE = 8
VMEM_LIMIT = 56 * 1024 * 1024

NT_DIMS = (((1,), (1,)), ((), ()))
TN_DIMS = (((0,), (0,)), ((), ()))


def _pick(n, target, mult):
    if n <= target:
        return n
    t = (target // mult) * mult
    while t >= mult:
        if n % t == 0:
            return t
        t -= mult
    return n


def _params(sem):
    return pltpu.CompilerParams(dimension_semantics=sem, vmem_limit_bytes=VMEM_LIMIT)


def _log_sigmoid(x):
    return jnp.minimum(x, 0.0) - jnp.log(1.0 + jnp.exp(-jnp.abs(x)))


def _sigmoid(x):
    return 0.5 * jnp.tanh(0.5 * x) + 0.5


def _split_bf16(x):
    hi = x.astype(BF16)
    return hi, (x - hi.astype(F32)).astype(BF16)


def _dot(a, b):
    return jnp.dot(a, b, preferred_element_type=F32)


def _rmsnorm_body(x_ref, w_ref, o_ref):
    x = x_ref[...]
    ms = jnp.mean(x * x, axis=-1, keepdims=True)
    o_ref[...] = (x * lax.rsqrt(ms + EPS) * w_ref[...]).astype(o_ref.dtype)


def _pair_specs(block, na, col=lambda *g: 0):
    return [pl.BlockSpec(block, lambda *g: (jnp.minimum(g[0], na - 1), col(*g))),
            pl.BlockSpec(block, lambda *g: (jnp.maximum(g[0] - na, 0), col(*g)))]


def _rmsnorm_pair_body(xa_ref, xb_ref, w_ref, o_ref, *, na):
    @pl.when(pl.program_id(0) < na)
    def _():
        _rmsnorm_body(xa_ref, w_ref, o_ref)

    @pl.when(pl.program_id(0) >= na)
    def _():
        _rmsnorm_body(xb_ref, w_ref, o_ref)


def _rmsnorm_pair(xa, xb, w, l, out_dtype):
    d = xa.shape[1]
    tr = _pick(xa.shape[0], 256, SUBLANE * 2)
    assert xa.shape[0] % tr == 0 and xb.shape[0] % tr == 0
    na, nb = xa.shape[0] // tr, xb.shape[0] // tr
    return pl.pallas_call(
        functools.partial(_rmsnorm_pair_body, na=na),
        grid=(na + nb,),
        in_specs=_pair_specs((tr, d), na) + [pl.BlockSpec((None, 1, d), lambda i: (l, 0, 0))],
        out_specs=pl.BlockSpec((tr, d), lambda i: (i, 0)),
        out_shape=jax.ShapeDtypeStruct(((na + nb) * tr, d), out_dtype),
        compiler_params=_params(("parallel",)),
        name="rmsnorm",
    )(xa, xb, w)


def _rmsnorm(x, w, l, out_dtype, row_start=0, rows=None):
    m, d = x.shape
    rows = m if rows is None else rows
    tr = _pick(rows, 256, SUBLANE * 2)
    assert row_start % tr == 0
    off = row_start // tr
    return pl.pallas_call(
        _rmsnorm_body,
        grid=(rows // tr,),
        in_specs=[pl.BlockSpec((tr, d), lambda i: (i + off, 0)),
                  pl.BlockSpec((None, 1, d), lambda i: (l, 0, 0))],
        out_specs=pl.BlockSpec((tr, d), lambda i: (i, 0)),
        out_shape=jax.ShapeDtypeStruct((rows, d), out_dtype),
        compiler_params=_params(("parallel",)),
        name="rmsnorm",
    )(x, w)


def _row_scale(rs_ref, width):
    return jnp.concatenate([rs_ref[...]] * (width // LANE), axis=1)


def _split_act(a):
    return a if isinstance(a, tuple) else (a, None)


def _ss_spec(tm, ss):
    return [] if ss is None else [pl.BlockSpec((tm, LANE), lambda i, *_: (i, 0))]


def _row_tiles(rows, nsplit):
    step = rows // nsplit
    return [slice(t * step, (t + 1) * step) for t in range(nsplit)]


def _mm_body(a_ref, b_ref, *rest, nsplit):
    *ss_ref, o_ref = rest
    for r in _row_tiles(a_ref.shape[0], nsplit):
        p = _dot(a_ref[r, :], b_ref[...])
        if ss_ref:
            p = p * jnp.concatenate([ss_ref[0][r, :]] * (p.shape[1] // LANE), axis=1)
        o_ref[r, :] = p.astype(o_ref.dtype)


def _matmul(a, w, l, out_dtype, tm=1024, tn=1024, nsplit=1, name="matmul"):
    a, ss = _split_act(a)
    m, k = a.shape
    n = w.shape[2]
    tm = _pick(m, tm, 16)
    tn = _pick(n, tn, LANE)
    return pl.pallas_call(
        functools.partial(_mm_body, nsplit=nsplit),
        grid=(m // tm, n // tn),
        in_specs=[pl.BlockSpec((tm, k), lambda i, j: (i, 0)),
                  pl.BlockSpec((None, k, tn), lambda i, j: (l, 0, j))] + _ss_spec(tm, ss),
        out_specs=pl.BlockSpec((tm, tn), lambda i, j: (i, j)),
        out_shape=jax.ShapeDtypeStruct((m, n), out_dtype),
        compiler_params=_params(("parallel", "arbitrary")),
        name=name,
    )(a, w, *([] if ss is None else [ss]))


def _swiglu_body(a_ref, wg_ref, wu_ref, *rest, nsplit):
    *ss_ref, o_ref = rest
    for r in _row_tiles(a_ref.shape[0], nsplit):
        a = a_ref[r, :]
        g = _dot(a, wg_ref[...])
        u = _dot(a, wu_ref[...])
        if ss_ref:
            rs = jnp.concatenate([ss_ref[0][r, :]] * (g.shape[1] // LANE), axis=1)
            g, u = g * rs, u * rs
        o_ref[r, :] = (g * _sigmoid(g) * u).astype(o_ref.dtype)


def _swiglu_up(a, wg, wu, l, tm=2048, tn=256, nsplit=4):
    a, ss = _split_act(a)
    m, k = a.shape
    n = wg.shape[2]
    tm = _pick(m, tm, 16)
    tn = min(tn, n)
    return pl.pallas_call(
        functools.partial(_swiglu_body, nsplit=nsplit),
        grid=(m // tm, pl.cdiv(n, tn)),
        in_specs=[pl.BlockSpec((tm, k), lambda i, j: (i, 0)),
                  pl.BlockSpec((None, k, tn), lambda i, j: (l, 0, j)),
                  pl.BlockSpec((None, k, tn), lambda i, j: (l, 0, j))] + _ss_spec(tm, ss),
        out_specs=pl.BlockSpec((tm, tn), lambda i, j: (i, j)),
        out_shape=jax.ShapeDtypeStruct((m, n), BF16),
        compiler_params=_params(("parallel", "arbitrary")),
        name="swiglu_up",
    )(a, wg, wu, *([] if ss is None else [ss]))


def _mm_res_body(a_ref, b_ref, *rest, scale, na, fuse, width, nsplit):
    n_res = 1 if na is None else 2
    res_refs, rest = rest[:n_res], rest[n_res:]
    for rt in _row_tiles(a_ref.shape[0], nsplit):
        if na is None:
            r = res_refs[0][rt, :]
        else:
            r = jnp.where(pl.program_id(0) < na, res_refs[0][rt, :], res_refs[1][rt, :])
        y = r + scale * _dot(a_ref[rt, :], b_ref[...])
        if not fuse:
            rest[0][rt, :] = y
            continue
        gain_ref, o_ref, yw_ref, ss_ref = rest
        o_ref[rt, :] = y
        yw_ref[rt, :] = (y * gain_ref[...]).astype(yw_ref.dtype)
        sq = y * y
        part = sq[:, 0:LANE]
        for t in range(1, sq.shape[1] // LANE):
            part = part + sq[:, t * LANE:(t + 1) * LANE]
        @pl.when(pl.program_id(1) == 0)
        def _():
            ss_ref[rt, :] = part

        @pl.when(pl.program_id(1) > 0)
        def _():
            ss_ref[rt, :] += part
    if fuse:
        @pl.when(pl.program_id(1) == pl.num_programs(1) - 1)
        def _():
            tot = jnp.sum(ss_ref[...], axis=1, keepdims=True)
            ss_ref[...] = jnp.broadcast_to(lax.rsqrt(tot * (1.0 / width) + EPS), ss_ref.shape)


def _matmul_residual(a, w, l, res, scale, tm=512, tn=512, next_gain=None, nsplit=1, name="matmul_res"):
    m, k = a.shape
    n = w.shape[2]
    tm = _pick(m, tm, 16)
    tn = _pick(n, tn, LANE)
    if isinstance(res, tuple):
        assert res[0].shape[0] % tm == 0 and res[1].shape[0] % tm == 0
        na = res[0].shape[0] // tm
        res_specs = _pair_specs((tm, tn), na, col=lambda i, j: j)
    else:
        na, res = None, (res,)
        res_specs = [pl.BlockSpec((tm, tn), lambda i, j: (i, j))]
    in_specs = [pl.BlockSpec((tm, k), lambda i, j: (i, 0)),
                pl.BlockSpec((None, k, tn), lambda i, j: (l, 0, j))] + res_specs
    out_specs = pl.BlockSpec((tm, tn), lambda i, j: (i, j))
    out_shape = jax.ShapeDtypeStruct((m, n), F32)
    args = [a, w, *res]
    fuse = next_gain is not None
    if fuse:
        gains, gl = next_gain
        in_specs.append(pl.BlockSpec((None, 1, tn), lambda i, j: (gl, 0, j)))
        args.append(gains)
        out_specs = [out_specs, pl.BlockSpec((tm, tn), lambda i, j: (i, j)),
                     pl.BlockSpec((tm, LANE), lambda i, j: (i, 0))]
        out_shape = [out_shape, jax.ShapeDtypeStruct((m, n), BF16), jax.ShapeDtypeStruct((m, LANE), F32)]
    out = pl.pallas_call(
        functools.partial(_mm_res_body, scale=scale, na=na, fuse=fuse, width=n, nsplit=nsplit),
        grid=(m // tm, n // tn),
        in_specs=in_specs,
        out_specs=out_specs,
        out_shape=out_shape,
        compiler_params=_params(("parallel", "arbitrary")),
        name=name,
    )(*args)
    return (out[0], (out[1], out[2])) if fuse else out


def _mm2_res_body(a1_ref, a2_ref, b1_ref, b2_ref, r_ref, o_ref, *, nsplit):
    for r in _row_tiles(a1_ref.shape[0], nsplit):
        o_ref[r, :] = r_ref[r, :] + (_dot(a1_ref[r, :], b1_ref[...]) + _dot(a2_ref[r, :], b2_ref[...]))


def _out_proj(a1, a2, w, l, res, tm=1024, tn=512, nsplit=1):
    m, k1 = a1.shape
    k2 = a2.shape[1]
    assert k1 == k2 and w.shape[1] == k1 + k2
    n = w.shape[2]
    tm = _pick(m, tm, 16)
    tn = _pick(n, tn, LANE)
    return pl.pallas_call(
        functools.partial(_mm2_res_body, nsplit=nsplit),
        grid=(m // tm, n // tn),
        in_specs=[pl.BlockSpec((tm, k1), lambda i, j: (i, 0)),
                  pl.BlockSpec((tm, k2), lambda i, j: (i, 0)),
                  pl.BlockSpec((None, k1, tn), lambda i, j: (l, 0, j)),
                  pl.BlockSpec((None, k2, tn), lambda i, j: (l, 1, j)),
                  pl.BlockSpec((tm, tn), lambda i, j: (i, j))],
        out_specs=pl.BlockSpec((tm, tn), lambda i, j: (i, j)),
        out_shape=jax.ShapeDtypeStruct((m, n), F32),
        compiler_params=_params(("parallel", "arbitrary")),
        name="mixer_out_proj",
    )(a1, a2, w, w, res)


def _gate_proj_body(a_ref, w_ref, wt_ref, *rest):
    *ss_ref, o_ref, ot_ref = rest
    a = a_ref[...]
    o = _dot(a, w_ref[...])
    ot = lax.dot_general(wt_ref[...], a, NT_DIMS, preferred_element_type=F32)
    if ss_ref:
        o = o * _row_scale(ss_ref[0], o.shape[1])
        ot = ot * jnp.transpose(ss_ref[0][...])[0:1, :]
    o_ref[...] = o
    ot_ref[...] = ot


def _gate_proj(a, w, wt, l, tm=1024):
    a, ss = _split_act(a)
    m, k = a.shape
    n = w.shape[2]
    nt = wt.shape[1]
    tm = _pick(m, tm, LANE)
    return pl.pallas_call(
        _gate_proj_body,
        grid=(m // tm,),
        in_specs=[pl.BlockSpec((tm, k), lambda i: (i, 0)),
                  pl.BlockSpec((None, k, n), lambda i: (l, 0, 0)),
                  pl.BlockSpec((None, nt, k), lambda i: (l, 0, 0))] + _ss_spec(tm, ss),
        out_specs=[pl.BlockSpec((tm, n), lambda i: (i, 0)),
                   pl.BlockSpec((nt, tm), lambda i: (0, i))],
        out_shape=[jax.ShapeDtypeStruct((m, n), F32),
                   jax.ShapeDtypeStruct((nt, m), F32)],
        compiler_params=_params(("parallel",)),
        name="gate_proj",
    )(a, w, wt, *([] if ss is None else [ss]))


def _causal_mask(chunk, reverse):
    r = lax.broadcasted_iota(jnp.int32, (chunk, chunk), 0)
    c = lax.broadcasted_iota(jnp.int32, (chunk, chunk), 1)
    return (c >= r) if reverse else (c <= r)


def _head_norm_gate(h, other, gate, nw):
    h = h + other
    return h * lax.rsqrt(jnp.mean(h * h, axis=-1, keepdims=True) + EPS) * nw * gate


def _gla_body(q_ref, k_ref, v_ref, lr_ref, wlr_ref, blr_ref, *rest, reverse, final, lb, hps):
    if final:
        gg_ref, ob_ref, nw_ref, o_ref, s_ref = rest
    else:
        o_ref, s_ref = rest
    dk = q_ref.shape[-1] // hps
    dv = v_ref.shape[-1] // hps
    chunk = GLA_CHUNK
    nch = lb // chunk
    qk = hps * dk

    @pl.when(pl.program_id(2) == 0)
    def _():
        s_ref[...] = jnp.zeros_like(s_ref)

    lh, ll = _split_bf16(lr_ref[...])
    wh, wl = _split_bf16(wlr_ref[...])
    z = _dot(lh, wh) + _dot(lh, wl) + _dot(ll, wh) + blr_ref[...]
    g = jnp.maximum(_log_sigmoid(z) * (1.0 / GLA_TAU), GLA_MIN_LOG_DECAY)
    mask = _causal_mask(chunk, reverse)
    tri = jnp.where(mask, 1.0, 0.0).astype(BF16)
    gh, gl = _split_bf16(jnp.concatenate([g[c * chunk:(c + 1) * chunk, :] for c in range(nch)], axis=1))
    bcum = _dot(tri, gh) + _dot(tri, gl)

    scale = dk ** -0.5
    zero_blk = jnp.zeros((chunk, chunk), BF16)
    npair = nch // 2
    for pb in (range(npair - 1, -1, -1) if reverse else range(npair)):
        rows = slice(2 * pb * chunk, (2 * pb + 2) * chunk)
        c_first, c_second = (2 * pb + 1, 2 * pb) if reverse else (2 * pb, 2 * pb + 1)
        for hh in range(hps):
            kc = slice(hh * dk, (hh + 1) * dk)
            vc = slice(hh * dv, (hh + 1) * dv)

            def chunk_terms(c):
                b = bcum[:, c * qk + hh * dk:c * qk + (hh + 1) * dk]
                tot = b[0:1, :] if reverse else b[chunk - 1:chunk, :]
                r = slice(c * chunk, (c + 1) * chunk)
                q_dec = q_ref[r, kc].astype(F32) * (jnp.exp(b) * scale)
                k_inv = k_ref[r, kc].astype(F32) * jnp.exp(-b)
                return tot, q_dec, k_inv, k_inv * jnp.exp(tot)

            tot1, qd1, ki1, ke1 = chunk_terms(c_first)
            tot2, qd2, ki2, ke2 = chunk_terms(c_second)
            qd1b, qd2b = qd1.astype(BF16), qd2.astype(BF16)
            a11 = lax.dot_general(qd1b, ki1.astype(BF16), NT_DIMS, preferred_element_type=F32)
            a22 = lax.dot_general(qd2b, ki2.astype(BF16), NT_DIMS, preferred_element_type=F32)
            a21 = lax.dot_general(qd2b, ke1.astype(BF16), NT_DIMS, preferred_element_type=F32)
            a11 = jnp.where(mask, a11, 0.0).astype(BF16)
            a22 = jnp.where(mask, a22, 0.0).astype(BF16)
            a21 = a21.astype(BF16)
            qs2 = (qd2 * jnp.exp(tot1)).astype(BF16)
            kx1 = (ke1 * jnp.exp(tot2)).astype(BF16)
            ke2b = ke2.astype(BF16)
            if reverse:
                amat = jnp.concatenate([jnp.concatenate([a22, a21], axis=1),
                                        jnp.concatenate([zero_blk, a11], axis=1)], axis=0)
                qmat = jnp.concatenate([qs2, qd1b], axis=0)
                kmat = jnp.concatenate([ke2b, kx1], axis=0)
            else:
                amat = jnp.concatenate([jnp.concatenate([a11, zero_blk], axis=1),
                                        jnp.concatenate([a21, a22], axis=1)], axis=0)
                qmat = jnp.concatenate([qd1b, qs2], axis=0)
                kmat = jnp.concatenate([kx1, ke2b], axis=0)
            v = v_ref[rows, vc]
            s = s_ref[hh]
            o = _dot(amat, v) + _dot(qmat, s.astype(BF16))
            dec = jnp.exp(jnp.transpose(jnp.broadcast_to(tot1 + tot2, (LANE, dk))))
            dec = jnp.concatenate([dec] * (dv // LANE), axis=1)
            s_ref[hh] = s * dec + lax.dot_general(kmat, v, TN_DIMS, preferred_element_type=F32)
            if final:
                gg = gg_ref[rows, vc].astype(F32)
                o = _head_norm_gate(o, ob_ref[rows, vc], gg * _sigmoid(gg), nw_ref[:, vc])
            o_ref[rows, vc] = o.astype(o_ref.dtype)


def _gla_scan(proj, lr, wlr, blr, l, *, reverse, dk, dv, col_q, col_k, col_v, col_gate=None,
              other=None, norm_w=None, hps=HEADS):
    bsz, seq, _ = proj.shape
    lb = _pick(seq, 512, LANE)
    assert lb % (2 * GLA_CHUNK) == 0 and HEADS % hps == 0
    nblk = seq // lb
    final = other is not None
    qk, width = hps * dk, hps * dv

    def tok(i):
        return (nblk - 1 - i) if reverse else i

    def col_spec(w, col0):
        return pl.BlockSpec((None, lb, w), lambda b, h, i: (b, tok(i), col0 // w + h))

    d = 1 if reverse else 0
    in_specs = [col_spec(qk, col_q), col_spec(qk, col_k), col_spec(width, col_v),
                pl.BlockSpec((None, lb, LANE), lambda b, h, i: (b, tok(i), 0)),
                pl.BlockSpec((None, None, LANE, qk), lambda b, h, i: (l, d, 0, h)),
                pl.BlockSpec((None, None, 1, qk), lambda b, h, i: (l, d, 0, h))]
    args = [proj, proj, proj, lr, wlr, blr]
    if final:
        in_specs += [col_spec(width, col_gate),
                     pl.BlockSpec((None, lb, width), lambda b, h, i: (b, tok(i), h)),
                     pl.BlockSpec((None, 1, width), lambda b, h, i: (l, 0, h))]
        args += [proj, other, norm_w]
    return pl.pallas_call(
        functools.partial(_gla_body, reverse=reverse, final=final, lb=lb, hps=hps),
        grid=(bsz, HEADS // hps, nblk),
        in_specs=in_specs,
        out_specs=pl.BlockSpec((None, lb, width), lambda b, h, i: (b, tok(i), h)),
        out_shape=jax.ShapeDtypeStruct((bsz, seq, HEADS * dv), BF16 if final else F32),
        scratch_shapes=[pltpu.VMEM((hps, dk, dv), F32)],
        compiler_params=_params(("parallel", "parallel", "arbitrary")),
        name="gla_fwd" if final else "gla_bwd",
    )(*args)


def _segment_cumsum_lanes(x, seg, reverse):
    n = x.shape[-1]
    pos = lax.broadcasted_iota(jnp.int32, x.shape, x.ndim - 1) % seg
    s = 1
    while s < seg:
        if reverse:
            x = x + jnp.where(pos < seg - s, pltpu.roll(x, n - s, x.ndim - 1), 0.0)
        else:
            x = x + jnp.where(pos >= s, pltpu.roll(x, s, x.ndim - 1), 0.0)
        s *= 2
    return x


def _mlstm_body(q_ref, k_ref, v_ref, gr_ref, br_ref, *rest, reverse, final, lb, hps):
    if final:
        mo_ref, hb_ref, nw_ref, o_ref, c_ref, m_ref = rest
    else:
        o_ref, c_ref, m_ref = rest
    dk = q_ref.shape[-1] // hps
    dv = v_ref.shape[-1] // hps
    chunk = min(MLSTM_CHUNK, lb)

    @pl.when(pl.program_id(2) == 0)
    def _():
        c_ref[...] = jnp.zeros_like(c_ref)
        m_ref[...] = jnp.full_like(m_ref, NEG_BIG)

    ti, tf = (2, 3) if reverse else (0, 1)
    gr = gr_ref[...] + br_ref[:, 0:1]
    bcum = _segment_cumsum_lanes(_log_sigmoid(gr), chunk, reverse)
    row_id = lax.broadcasted_iota(jnp.int32, gr.shape, 0) % SUBLANE
    rowform = jnp.where(row_id == tf, bcum, gr)
    pad_rows = jnp.zeros((LANE - SUBLANE, chunk), F32)

    mask = _causal_mask(chunk, reverse)
    ones_col = jnp.where(lax.broadcasted_iota(jnp.int32, (chunk, LANE), 1) == 0, 1.0, 0.0).astype(BF16)
    scale = dk ** -0.5
    nch = lb // chunk
    for ch in (range(nch - 1, -1, -1) if reverse else range(nch)):
        rows = slice(ch * chunk, (ch + 1) * chunk)
        for hh in range(hps):
            kc = slice(hh * dk, (hh + 1) * dk)
            vc = slice(hh * dv, (hh + 1) * dv)
            rf = rowform[hh * SUBLANE:(hh + 1) * SUBLANE, rows]
            u_row = rf[ti:ti + 1, :] - rf[tf:tf + 1, :]
            colform = jnp.transpose(jnp.concatenate([rf, pad_rows], axis=0))
            bc = colform[:, tf:tf + 1]
            ic = colform[:, ti:ti + 1]
            gtot = bc[0:1, :] if reverse else bc[chunk - 1:chunk, :]
            log_d = jnp.where(mask, bc + u_row, NEG_BIG)
            m_intra = jnp.max(log_d, axis=1, keepdims=True)
            q = q_ref[rows, kc]
            k = k_ref[rows, kc]
            v_aug = jnp.concatenate([v_ref[rows, vc], ones_col], axis=1)
            s = lax.dot_general(q, k, NT_DIMS, preferred_element_type=F32) * jnp.exp(log_d - m_intra)
            intra = _dot(s.astype(BF16), v_aug)
            c = c_ref[hh]
            inter = _dot(q, c.astype(BF16))
            m_prev = m_ref[hh, 0:1, 0:1]
            bm = bc + m_prev
            m_j = jnp.maximum(bm, m_intra)
            comb = (scale * jnp.exp(bm - m_j)) * inter + (scale * jnp.exp(m_intra - m_j)) * intra
            denom = jnp.maximum(jnp.abs(comb[:, dv:dv + 1]), jnp.exp(-m_j))
            h = comb[:, :dv] * (1.0 / denom)
            a_col = gtot - bc + ic
            m_new = jnp.maximum(gtot + m_prev, jnp.max(a_col, axis=0, keepdims=True))
            kw = (k.astype(F32) * jnp.exp(a_col - m_new)).astype(BF16)
            c_ref[hh] = (jnp.exp(gtot + m_prev - m_new) * c
                         + lax.dot_general(kw, v_aug, TN_DIMS, preferred_element_type=F32))
            m_ref[hh] = jnp.broadcast_to(m_new, (SUBLANE, LANE))
            if final:
                h = _head_norm_gate(h, hb_ref[rows, vc], _sigmoid(mo_ref[rows, vc].astype(F32)), nw_ref[:, vc])
            o_ref[rows, vc] = h.astype(o_ref.dtype)


def _mlstm_scan(proj, gates_t, bias_r, l, *, reverse, dk, dv, col_q, col_k, col_v,
                col_gate=None, other=None, norm_w=None, hps=HEADS):
    bsz, seq, _ = proj.shape
    lb = _pick(seq, 512, LANE)
    nblk = seq // lb
    final = other is not None
    assert HEADS % hps == 0
    qk, width = hps * dk, hps * dv

    def tok(i):
        return (nblk - 1 - i) if reverse else i

    def col_spec(w, col0):
        return pl.BlockSpec((None, lb, w), lambda b, h, i: (b, tok(i), col0 // w + h))

    in_specs = [col_spec(qk, col_q), col_spec(qk, col_k), col_spec(width, col_v),
                pl.BlockSpec((hps * SUBLANE, lb), lambda b, h, i: (h, b * nblk + tok(i))),
                pl.BlockSpec((None, hps * SUBLANE, LANE), lambda b, h, i: (l, h, 0))]
    args = [proj, proj, proj, gates_t, bias_r]
    if final:
        in_specs += [col_spec(width, col_gate),
                     pl.BlockSpec((None, lb, width), lambda b, h, i: (b, tok(i), h)),
                     pl.BlockSpec((None, 1, width), lambda b, h, i: (l, 0, h))]
        args += [proj, other, norm_w]
    return pl.pallas_call(
        functools.partial(_mlstm_body, reverse=reverse, final=final, lb=lb, hps=hps),
        grid=(bsz, HEADS // hps, nblk),
        in_specs=in_specs,
        out_specs=pl.BlockSpec((None, lb, width), lambda b, h, i: (b, tok(i), h)),
        out_shape=jax.ShapeDtypeStruct((bsz, seq, HEADS * dv), BF16 if final else F32),
        scratch_shapes=[pltpu.VMEM((hps, dk, dv + LANE), F32), pltpu.VMEM((hps, SUBLANE, LANE), F32)],
        compiler_params=_params(("parallel", "parallel", "arbitrary")),
        name="mlstm_fwd" if final else "mlstm_bwd",
    )(*args)


def _xattn_body(x_ref, nw_ref, wq_ref, kv_ref, wo_ref, nw2_ref, o_ref, n_ref, *, heads):
    x = x_ref[...]
    xn = (x * lax.rsqrt(jnp.mean(x * x, axis=-1, keepdims=True) + EPS) * nw_ref[...]).astype(BF16)
    width = wq_ref.shape[1]
    hd = width // heads
    q = (_dot(xn, wq_ref[...]) * (hd ** -0.5)).astype(BF16)
    outs = []
    for h in range(heads):
        kh = kv_ref[:, h * hd:(h + 1) * hd]
        vh = kv_ref[:, width + h * hd:width + (h + 1) * hd]
        s = lax.dot_general(q[:, h * hd:(h + 1) * hd], kh, NT_DIMS, preferred_element_type=F32)
        p = jnp.exp(s - jnp.max(s, axis=-1, keepdims=True))
        p = p * (1.0 / jnp.sum(p, axis=-1, keepdims=True))
        outs.append(_dot(p.astype(BF16), vh))
    o = jnp.concatenate(outs, axis=1).astype(BF16)
    y = x + _dot(o, wo_ref[...])
    o_ref[...] = y
    n_ref[...] = (y * lax.rsqrt(jnp.mean(y * y, axis=-1, keepdims=True) + EPS) * nw2_ref[...]).astype(BF16)


def _xattn(x, norm_w, wq, kv, wo, next_norm_w, l, tq=512):
    bsz, seq, d = x.shape
    mem = kv.shape[1]
    width = wq.shape[2]
    tq = _pick(seq, tq, 16)
    return pl.pallas_call(
        functools.partial(_xattn_body, heads=XATTN_HEADS),
        grid=(bsz, seq // tq),
        in_specs=[pl.BlockSpec((None, tq, d), lambda b, i: (b, i, 0)),
                  pl.BlockSpec((None, 1, d), lambda b, i: (l, 0, 0)),
                  pl.BlockSpec((None, d, width), lambda b, i: (l, 0, 0), pipeline_mode=pl.Buffered(1)),
                  pl.BlockSpec((None, mem, 2 * width), lambda b, i: (b, 0, 0)),
                  pl.BlockSpec((None, width, d), lambda b, i: (l, 0, 0), pipeline_mode=pl.Buffered(1)),
                  pl.BlockSpec((None, 1, d), lambda b, i: (l, 0, 0))],
        out_specs=[pl.BlockSpec((None, tq, d), lambda b, i: (b, i, 0)),
                   pl.BlockSpec((None, tq, d), lambda b, i: (b, i, 0))],
        out_shape=[jax.ShapeDtypeStruct((bsz, seq, d), F32),
                   jax.ShapeDtypeStruct((bsz, seq, d), BF16)],
        compiler_params=_params(("parallel", "arbitrary")),
        name="mem_xattn",
    )(x, norm_w, wq, kv, wo, next_norm_w)


def _norm_w(w):
    return w.astype(F32).reshape(w.shape[0], 1, w.shape[1])


def _ffn(x, xn, w_gate, w_up, w_down, l, next_gain=None, up_split=4):
    hid = _swiglu_up(xn, w_gate, w_up, l, nsplit=up_split)
    return _matmul_residual(hid, w_down, l, x, 0.5, next_gain=next_gain, name="ffn_down")


def _mixer_weights(d, w_in, gla_w_lr, gla_b_lr, mlstm_gate_b):
    depth = w_in.shape[0]
    half = d // 2
    qk = half // 2
    o_lr = 2 * qk + 2 * half
    o_m = o_lr + 2 * GLA_RANK
    o_mg = o_m + 2 * qk + 2 * half
    w_gla = w_in[:, :, :o_lr].astype(BF16)
    w_ml = w_in[:, :, o_m:o_mg].astype(BF16)
    w_lr = jnp.pad(w_in[:, :, o_lr:o_m], ((0, 0), (0, 0), (0, LANE - 2 * GLA_RANK))).astype(BF16)
    mg = w_in[:, :, o_mg:o_mg + 4 * HEADS].reshape(depth, d, 4, HEADS).transpose(0, 3, 2, 1)
    w_gates_t = jnp.pad(mg, ((0, 0), (0, 0), (0, SUBLANE - 4), (0, 0))).reshape(depth, HEADS * SUBLANE, d)
    w_gates_t = w_gates_t.astype(BF16)
    bias_r = jnp.pad(mlstm_gate_b.astype(F32).transpose(0, 2, 1), ((0, 0), (0, 0), (0, SUBLANE - 4)))
    bias_r = jnp.broadcast_to(bias_r.reshape(depth, HEADS * SUBLANE, 1), (depth, HEADS * SUBLANE, LANE))
    wlr = jnp.stack([jnp.pad(gla_w_lr[:, 0], ((0, 0), (0, LANE - GLA_RANK), (0, 0))),
                     jnp.pad(gla_w_lr[:, 1], ((0, 0), (GLA_RANK, LANE - 2 * GLA_RANK), (0, 0)))],
                    axis=1).astype(F32)
    blr = gla_b_lr.astype(F32).reshape(depth, 2, 1, qk)
    return w_gla, w_ml, w_lr, w_gates_t, bias_r, wlr, blr


def _mixer(x, xn, bsz, seq, l, w_gla, w_ml, w_lr, w_gates_t, bias_r, wlr, blr, gla_norm, mlstm_norm, w_out,
           mlstm_fwd_hps=HEADS):
    m, d = x.shape
    half = d // 2
    dv = half // HEADS
    dk = dv // 2
    qk = HEADS * dk
    proj_g = _matmul(xn, w_gla, l, BF16, name="gla_in_proj").reshape(bsz, seq, -1)
    proj_m = _matmul(xn, w_ml, l, BF16, name="mlstm_in_proj").reshape(bsz, seq, -1)
    lr, gates_t = _gate_proj(xn, w_lr, w_gates_t, l)
    lr = lr.reshape(bsz, seq, LANE)

    kw = dict(dk=dk, dv=dv, col_q=0, col_k=qk, col_v=2 * qk)
    g_b = _gla_scan(proj_g, lr, wlr, blr, l, reverse=True, **kw)
    g_out = _gla_scan(proj_g, lr, wlr, blr, l, reverse=False, col_gate=2 * qk + half, other=g_b,
                      norm_w=gla_norm, **kw)
    m_b = _mlstm_scan(proj_m, gates_t, bias_r, l, reverse=True, **kw)
    m_out = _mlstm_scan(proj_m, gates_t, bias_r, l, reverse=False, col_gate=2 * qk + half,
                        other=m_b, norm_w=mlstm_norm, hps=mlstm_fwd_hps, **kw)
    return _out_proj(g_out.reshape(m, half), m_out.reshape(m, half), w_out, l, x)


def kernel(x_prompt, x_sample, mem_prompt, mem_sample, ffn1_norm, ffn1_w_gate, ffn1_w_up, ffn1_w_down, mix_norm, w_in, gla_w_lr, gla_b_lr, gla_out_norm, mlstm_gate_b, mlstm_out_norm, w_out, xattn_norm, mem_norm, xattn_wq, xattn_wk, xattn_wv, xattn_wo, ffn2_norm, ffn2_w_gate, ffn2_w_up, ffn2_w_down, final_norm):
    assert x_prompt.shape[1:] == x_sample.shape[1:] and mem_prompt.shape[1:] == mem_sample.shape[1:]
    n_p, seq, d = x_prompt.shape
    n_s = x_sample.shape[0]
    bsz = n_p + n_s
    m = bsz * seq
    x = (x_prompt.reshape(n_p * seq, d), x_sample.reshape(n_s * seq, d))
    mem = jnp.concatenate([mem_prompt, mem_sample], axis=0)
    n_mem = mem.shape[1]
    mem = mem.reshape(bsz * n_mem, d)
    depth = ffn1_norm.shape[0]

    ffn1 = (ffn1_w_gate.astype(BF16), ffn1_w_up.astype(BF16), ffn1_w_down.astype(BF16))
    ffn2 = (ffn2_w_gate.astype(BF16), ffn2_w_up.astype(BF16), ffn2_w_down.astype(BF16))
    mixer_w = _mixer_weights(d, w_in, gla_w_lr, gla_b_lr, mlstm_gate_b)
    mixer_rest = (_norm_w(gla_out_norm), _norm_w(mlstm_out_norm), w_out.astype(BF16))
    w_kv = jnp.concatenate([xattn_wk, xattn_wv], axis=2).astype(BF16)
    wq = xattn_wq.astype(BF16)
    wo = xattn_wo.astype(BF16)
    ffn1_n, ffn2_n = _norm_w(ffn1_norm), _norm_w(ffn2_norm)
    mix_n, xattn_n, mem_n = _norm_w(mix_norm), _norm_w(xattn_norm), _norm_w(mem_norm)

    xn = _rmsnorm_pair(*x, ffn1_n, 0, BF16)
    for l in range(depth):
        up_split, fwd_hps = (8, 2) if l == depth - 1 else (4, HEADS)
        x, xn = _ffn(x, xn, *ffn1, l, next_gain=(mix_n, l), up_split=up_split)
        x = _mixer(x, xn, bsz, seq, l, *mixer_w, *mixer_rest, mlstm_fwd_hps=fwd_hps)
        kv = _matmul(_rmsnorm(mem, mem_n, l, BF16), w_kv, l, BF16, tm=512, name="mem_kv_proj")
        x, xn = _xattn(x.reshape(bsz, seq, d), xattn_n, wq, kv.reshape(bsz, n_mem, -1), wo, ffn2_n, l)
        x, xn = x.reshape(m, d), xn.reshape(m, d)
        if l + 1 < depth:
            x, xn = _ffn(x, xn, *ffn2, l, next_gain=(ffn1_n, l + 1), up_split=up_split)
        else:
            x = _ffn(x, xn, *ffn2, l, up_split=up_split)
    fin = _norm_w(final_norm.reshape(1, d))
    y_p = _rmsnorm(x, fin, 0, F32, 0, n_p * seq).reshape(n_p, seq, d)
    y_s = _rmsnorm(x, fin, 0, F32, n_p * seq, n_s * seq).reshape(n_s, seq, d)
    return (y_p, y_s)
```

```python
import functools

import jax
import jax.numpy as jnp
from jax import lax
from jax.experimental import pallas as pl
from jax.experimental.pallas import tpu as pltpu

F32 = jnp.float32
BF16 = jnp.bfloat16

EPS = 1e-6
NEG_BIG = -1e30
HEADS = 4
GLA_RANK = 16
GLA_TAU = 16.0
GLA_MIN_LOG_DECAY = -1.0
GLA_CHUNK = 64
MLSTM_CHUNK = 256
MLSTM_FWD_HEADS_PER_STEP = 2
XATTN_HEADS = 4
LANE = 128
SUBLANE = 8
VMEM_LIMIT = 56 * 1024 * 1024

NT_DIMS = (((1,), (1,)), ((), ()))
TN_DIMS = (((0,), (0,)), ((), ()))


def _pick(n, target, mult):
    if n <= target:
        return n
    t = (target // mult) * mult
    while t >= mult:
        if n % t == 0:
            return t
        t -= mult
    return n


def _params(sem):
    return pltpu.CompilerParams(dimension_semantics=sem, vmem_limit_bytes=VMEM_LIMIT)


def _log_sigmoid(x):
    return jnp.minimum(x, 0.0) - jnp.log(1.0 + jnp.exp(-jnp.abs(x)))


def _sigmoid(x):
    return 0.5 * jnp.tanh(0.5 * x) + 0.5


def _split_bf16(x):
    hi = x.astype(BF16)
    return hi, (x - hi.astype(F32)).astype(BF16)


def _dot(a, b):
    return jnp.dot(a, b, preferred_element_type=F32)


def _rmsnorm_body(x_ref, w_ref, o_ref):
    x = x_ref[...]
    ms = jnp.mean(x * x, axis=-1, keepdims=True)
    o_ref[...] = (x * lax.rsqrt(ms + EPS) * w_ref[...]).astype(o_ref.dtype)


def _pair_specs(block, na, col=lambda *g: 0):
    return [pl.BlockSpec(block, lambda *g: (jnp.minimum(g[0], na - 1), col(*g))),
            pl.BlockSpec(block, lambda *g: (jnp.maximum(g[0] - na, 0), col(*g)))]


def _rmsnorm_pair_body(xa_ref, xb_ref, w_ref, o_ref, *, na):
    @pl.when(pl.program_id(0) < na)
    def _():
        _rmsnorm_body(xa_ref, w_ref, o_ref)

    @pl.when(pl.program_id(0) >= na)
    def _():
        _rmsnorm_body(xb_ref, w_ref, o_ref)


def _rmsnorm_pair(xa, xb, w, l, out_dtype):
    d = xa.shape[1]
    tr = _pick(xa.shape[0], 256, SUBLANE * 2)
    assert xa.shape[0] % tr == 0 and xb.shape[0] % tr == 0
    na, nb = xa.shape[0] // tr, xb.shape[0] // tr
    return pl.pallas_call(
        functools.partial(_rmsnorm_pair_body, na=na),
        grid=(na + nb,),
        in_specs=_pair_specs((tr, d), na) + [pl.BlockSpec((None, 1, d), lambda i: (l, 0, 0))],
        out_specs=pl.BlockSpec((tr, d), lambda i: (i, 0)),
        out_shape=jax.ShapeDtypeStruct(((na + nb) * tr, d), out_dtype),
        compiler_params=_params(("parallel",)),
        name="rmsnorm",
    )(xa, xb, w)


def _rmsnorm(x, w, l, out_dtype, row_start=0, rows=None):
    m, d = x.shape
    rows = m if rows is None else rows
    tr = _pick(rows, 256, SUBLANE * 2)
    assert row_start % tr == 0
    off = row_start // tr
    return pl.pallas_call(
        _rmsnorm_body,
        grid=(rows // tr,),
        in_specs=[pl.BlockSpec((tr, d), lambda i: (i + off, 0)),
                  pl.BlockSpec((None, 1, d), lambda i: (l, 0, 0))],
        out_specs=pl.BlockSpec((tr, d), lambda i: (i, 0)),
        out_shape=jax.ShapeDtypeStruct((rows, d), out_dtype),
        compiler_params=_params(("parallel",)),
        name="rmsnorm",
    )(x, w)


def _row_scale(rs_ref, width):
    return jnp.concatenate([rs_ref[...]] * (width // LANE), axis=1)


def _split_act(a):
    return a if isinstance(a, tuple) else (a, None)


def _ss_spec(tm, ss):
    return [] if ss is None else [pl.BlockSpec((tm, LANE), lambda i, *_: (i, 0))]


def _row_tiles(rows, nsplit):
    step = rows // nsplit
    return [slice(t * step, (t + 1) * step) for t in range(nsplit)]


def _mm_body(a_ref, b_ref, *rest, nsplit):
    *ss_ref, o_ref = rest
    for r in _row_tiles(a_ref.shape[0], nsplit):
        p = _dot(a_ref[r, :], b_ref[...])
        if ss_ref:
            p = p * jnp.concatenate([ss_ref[0][r, :]] * (p.shape[1] // LANE), axis=1)
        o_ref[r, :] = p.astype(o_ref.dtype)


def _matmul(a, w, l, out_dtype, tm=1024, tn=1024, nsplit=1, name="matmul"):
    a, ss = _split_act(a)
    m, k = a.shape
    n = w.shape[2]
    tm = _pick(m, tm, 16)
    tn = _pick(n, tn, LANE)
    return pl.pallas_call(
        functools.partial(_mm_body, nsplit=nsplit),
        grid=(m // tm, n // tn),
        in_specs=[pl.BlockSpec((tm, k), lambda i, j: (i, 0)),
                  pl.BlockSpec((None, k, tn), lambda i, j: (l, 0, j))] + _ss_spec(tm, ss),
        out_specs=pl.BlockSpec((tm, tn), lambda i, j: (i, j)),
        out_shape=jax.ShapeDtypeStruct((m, n), out_dtype),
        compiler_params=_params(("parallel", "arbitrary")),
        name=name,
    )(a, w, *([] if ss is None else [ss]))


def _swiglu_body(a_ref, wg_ref, wu_ref, *rest, nsplit):
    *ss_ref, o_ref = rest
    for r in _row_tiles(a_ref.shape[0], nsplit):
        a = a_ref[r, :]
        g = _dot(a, wg_ref[...])
        u = _dot(a, wu_ref[...])
        if ss_ref:
            rs = jnp.concatenate([ss_ref[0][r, :]] * (g.shape[1] // LANE), axis=1)
            g, u = g * rs, u * rs
        o_ref[r, :] = (g * _sigmoid(g) * u).astype(o_ref.dtype)


def _swiglu_up(a, wg, wu, l, tm=2048, tn=256, nsplit=4):
    a, ss = _split_act(a)
    m, k = a.shape
    n = wg.shape[2]
    tm = _pick(m, tm, 16)
    tn = min(tn, n)
    return pl.pallas_call(
        functools.partial(_swiglu_body, nsplit=nsplit),
        grid=(m // tm, pl.cdiv(n, tn)),
        in_specs=[pl.BlockSpec((tm, k), lambda i, j: (i, 0)),
                  pl.BlockSpec((None, k, tn), lambda i, j: (l, 0, j)),
                  pl.BlockSpec((None, k, tn), lambda i, j: (l, 0, j))] + _ss_spec(tm, ss),
        out_specs=pl.BlockSpec((tm, tn), lambda i, j: (i, j)),
        out_shape=jax.ShapeDtypeStruct((m, n), BF16),
        compiler_params=_params(("parallel", "arbitrary")),
        name="swiglu_up",
    )(a, wg, wu, *([] if ss is None else [ss]))


def _mm_res_body(a_ref, b_ref, *rest, scale, na, fuse, width, nsplit):
    n_res = 1 if na is None else 2
    res_refs, rest = rest[:n_res], rest[n_res:]
    for rt in _row_tiles(a_ref.shape[0], nsplit):
        if na is None:
            r = res_refs[0][rt, :]
        else:
            r = jnp.where(pl.program_id(0) < na, res_refs[0][rt, :], res_refs[1][rt, :])
        y = r + scale * _dot(a_ref[rt, :], b_ref[...])
        if not fuse:
            rest[0][rt, :] = y
            continue
        gain_ref, o_ref, yw_ref, ss_ref = rest
        o_ref[rt, :] = y
        yw_ref[rt, :] = (y * gain_ref[...]).astype(yw_ref.dtype)
        sq = y * y
        part = sq[:, 0:LANE]
        for t in range(1, sq.shape[1] // LANE):
            part = part + sq[:, t * LANE:(t + 1) * LANE]
        @pl.when(pl.program_id(1) == 0)
        def _():
            ss_ref[rt, :] = part

        @pl.when(pl.program_id(1) > 0)
        def _():
            ss_ref[rt, :] += part
    if fuse:
        @pl.when(pl.program_id(1) == pl.num_programs(1) - 1)
        def _():
            tot = jnp.sum(ss_ref[...], axis=1, keepdims=True)
            ss_ref[...] = jnp.broadcast_to(lax.rsqrt(tot * (1.0 / width) + EPS), ss_ref.shape)


def _matmul_residual(a, w, l, res, scale, tm=512, tn=512, next_gain=None, nsplit=1, name="matmul_res"):
    m, k = a.shape
    n = w.shape[2]
    tm = _pick(m, tm, 16)
    tn = _pick(n, tn, LANE)
    if isinstance(res, tuple):
        assert res[0].shape[0] % tm == 0 and res[1].shape[0] % tm == 0
        na = res[0].shape[0] // tm
        res_specs = _pair_specs((tm, tn), na, col=lambda i, j: j)
    else:
        na, res = None, (res,)
        res_specs = [pl.BlockSpec((tm, tn), lambda i, j: (i, j))]
    in_specs = [pl.BlockSpec((tm, k), lambda i, j: (i, 0)),
                pl.BlockSpec((None, k, tn), lambda i, j: (l, 0, j))] + res_specs
    out_specs = pl.BlockSpec((tm, tn), lambda i, j: (i, j))
    out_shape = jax.ShapeDtypeStruct((m, n), F32)
    args = [a, w, *res]
    fuse = next_gain is not None
    if fuse:
        gains, gl = next_gain
        in_specs.append(pl.BlockSpec((None, 1, tn), lambda i, j: (gl, 0, j)))
        args.append(gains)
        out_specs = [out_specs, pl.BlockSpec((tm, tn), lambda i, j: (i, j)),
                     pl.BlockSpec((tm, LANE), lambda i, j: (i, 0))]
        out_shape = [out_shape, jax.ShapeDtypeStruct((m, n), BF16), jax.ShapeDtypeStruct((m, LANE), F32)]
    out = pl.pallas_call(
        functools.partial(_mm_res_body, scale=scale, na=na, fuse=fuse, width=n, nsplit=nsplit),
        grid=(m // tm, n // tn),
        in_specs=in_specs,
        out_specs=out_specs,
        out_shape=out_shape,
        compiler_params=_params(("parallel", "arbitrary")),
        name=name,
    )(*args)
    return (out[0], (out[1], out[2])) if fuse else out


def _mm2_res_body(a1_ref, a2_ref, b1_ref, b2_ref, r_ref, o_ref, *, nsplit):
    for r in _row_tiles(a1_ref.shape[0], nsplit):
        o_ref[r, :] = r_ref[r, :] + (_dot(a1_ref[r, :], b1_ref[...]) + _dot(a2_ref[r, :], b2_ref[...]))


def _out_proj(a1, a2, w, l, res, tm=1024, tn=512, nsplit=1):
    m, k1 = a1.shape
    k2 = a2.shape[1]
    assert k1 == k2 and w.shape[1] == k1 + k2
    n = w.shape[2]
    tm = _pick(m, tm, 16)
    tn = _pick(n, tn, LANE)
    return pl.pallas_call(
        functools.partial(_mm2_res_body, nsplit=nsplit),
        grid=(m // tm, n // tn),
        in_specs=[pl.BlockSpec((tm, k1), lambda i, j: (i, 0)),
                  pl.BlockSpec((tm, k2), lambda i, j: (i, 0)),
                  pl.BlockSpec((None, k1, tn), lambda i, j: (l, 0, j)),
                  pl.BlockSpec((None, k2, tn), lambda i, j: (l, 1, j)),
                  pl.BlockSpec((tm, tn), lambda i, j: (i, j))],
        out_specs=pl.BlockSpec((tm, tn), lambda i, j: (i, j)),
        out_shape=jax.ShapeDtypeStruct((m, n), F32),
        compiler_params=_params(("parallel", "arbitrary")),
        name="mixer_out_proj",
    )(a1, a2, w, w, res)


def _gate_proj_body(a_ref, w_ref, wt_ref, *rest):
    *ss_ref, o_ref, ot_ref = rest
    a = a_ref[...]
    o = _dot(a, w_ref[...])
    ot = lax.dot_general(wt_ref[...], a, NT_DIMS, preferred_element_type=F32)
    if ss_ref:
        o = o * _row_scale(ss_ref[0], o.shape[1])
        ot = ot * jnp.transpose(ss_ref[0][...])[0:1, :]
    o_ref[...] = o
    ot_ref[...] = ot


def _gate_proj(a, w, wt, l, tm=1024):
    a, ss = _split_act(a)
    m, k = a.shape
    n = w.shape[2]
    nt = wt.shape[1]
    tm = _pick(m, tm, LANE)
    return pl.pallas_call(
        _gate_proj_body,
        grid=(m // tm,),
        in_specs=[pl.BlockSpec((tm, k), lambda i: (i, 0)),
                  pl.BlockSpec((None, k, n), lambda i: (l, 0, 0)),
                  pl.BlockSpec((None, nt, k), lambda i: (l, 0, 0))] + _ss_spec(tm, ss),
        out_specs=[pl.BlockSpec((tm, n), lambda i: (i, 0)),
                   pl.BlockSpec((nt, tm), lambda i: (0, i))],
        out_shape=[jax.ShapeDtypeStruct((m, n), F32),
                   jax.ShapeDtypeStruct((nt, m), F32)],
        compiler_params=_params(("parallel",)),
        name="gate_proj",
    )(a, w, wt, *([] if ss is None else [ss]))


def _causal_mask(chunk, reverse):
    r = lax.broadcasted_iota(jnp.int32, (chunk, chunk), 0)
    c = lax.broadcasted_iota(jnp.int32, (chunk, chunk), 1)
    return (c >= r) if reverse else (c <= r)


def _head_norm_gate(h, other, gate, nw):
    h = h + other
    return h * lax.rsqrt(jnp.mean(h * h, axis=-1, keepdims=True) + EPS) * nw * gate


def _gla_body(q_ref, k_ref, v_ref, lr_ref, wlr_ref, blr_ref, *rest, reverse, final, lb, hps):
    if final:
        gg_ref, ob_ref, nw_ref, o_ref, s_ref = rest
    else:
        o_ref, s_ref = rest
    dk = q_ref.shape[-1] // hps
    dv = v_ref.shape[-1] // hps
    chunk = GLA_CHUNK
    nch = lb // chunk
    qk = hps * dk

    @pl.when(pl.program_id(2) == 0)
    def _():
        s_ref[...] = jnp.zeros_like(s_ref)

    lh, ll = _split_bf16(lr_ref[...])
    wh, wl = _split_bf16(wlr_ref[...])
    z = _dot(lh, wh) + _dot(lh, wl) + _dot(ll, wh) + blr_ref[...]
    g = jnp.maximum(_log_sigmoid(z) * (1.0 / GLA_TAU), GLA_MIN_LOG_DECAY)
    mask = _causal_mask(chunk, reverse)
    tri = jnp.where(mask, 1.0, 0.0).astype(BF16)
    gh, gl = _split_bf16(jnp.concatenate([g[c * chunk:(c + 1) * chunk, :] for c in range(nch)], axis=1))
    bcum = _dot(tri, gh) + _dot(tri, gl)

    scale = dk ** -0.5
    zero_blk = jnp.zeros((chunk, chunk), BF16)
    npair = nch // 2
    for pb in (range(npair - 1, -1, -1) if reverse else range(npair)):
        rows = slice(2 * pb * chunk, (2 * pb + 2) * chunk)
        c_first, c_second = (2 * pb + 1, 2 * pb) if reverse else (2 * pb, 2 * pb + 1)
        for hh in range(hps):
            kc = slice(hh * dk, (hh + 1) * dk)
            vc = slice(hh * dv, (hh + 1) * dv)

            def chunk_terms(c):
                b = bcum[:, c * qk + hh * dk:c * qk + (hh + 1) * dk]
                tot = b[0:1, :] if reverse else b[chunk - 1:chunk, :]
                r = slice(c * chunk, (c + 1) * chunk)
                q_dec = q_ref[r, kc].astype(F32) * (jnp.exp(b) * scale)
                k_inv = k_ref[r, kc].astype(F32) * jnp.exp(-b)
                return tot, q_dec, k_inv, k_inv * jnp.exp(tot)

            tot1, qd1, ki1, ke1 = chunk_terms(c_first)
            tot2, qd2, ki2, ke2 = chunk_terms(c_second)
            qd1b, qd2b = qd1.astype(BF16), qd2.astype(BF16)
            a11 = lax.dot_general(qd1b, ki1.astype(BF16), NT_DIMS, preferred_element_type=F32)
            a22 = lax.dot_general(qd2b, ki2.astype(BF16), NT_DIMS, preferred_element_type=F32)
            a21 = lax.dot_general(qd2b, ke1.astype(BF16), NT_DIMS, preferred_element_type=F32)
            a11 = jnp.where(mask, a11, 0.0).astype(BF16)
            a22 = jnp.where(mask, a22, 0.0).astype(BF16)
            a21 = a21.astype(BF16)
            qs2 = (qd2 * jnp.exp(tot1)).astype(BF16)
            kx1 = (ke1 * jnp.exp(tot2)).astype(BF16)
            ke2b = ke2.astype(BF16)
            if reverse:
                amat = jnp.concatenate([jnp.concatenate([a22, a21], axis=1),
                                        jnp.concatenate([zero_blk, a11], axis=1)], axis=0)
                qmat = jnp.concatenate([qs2, qd1b], axis=0)
                kmat = jnp.concatenate([ke2b, kx1], axis=0)
            else:
                amat = jnp.concatenate([jnp.concatenate([a11, zero_blk], axis=1),
                                        jnp.concatenate([a21, a22], axis=1)], axis=0)
                qmat = jnp.concatenate([qd1b, qs2], axis=0)
                kmat = jnp.concatenate([kx1, ke2b], axis=0)
            v = v_ref[rows, vc]
            s = s_ref[hh]
            o = _dot(amat, v) + _dot(qmat, s.astype(BF16))
            dec = jnp.exp(jnp.transpose(jnp.broadcast_to(tot1 + tot2, (LANE, dk))))
            dec = jnp.concatenate([dec] * (dv // LANE), axis=1)
            s_ref[hh] = s * dec + lax.dot_general(kmat, v, TN_DIMS, preferred_element_type=F32)
            if final:
                gg = gg_ref[rows, vc].astype(F32)
                o = _head_norm_gate(o, ob_ref[rows, vc], gg * _sigmoid(gg), nw_ref[:, vc])
            o_ref[rows, vc] = o.astype(o_ref.dtype)


def _gla_scan(proj, lr, wlr, blr, l, *, reverse, dk, dv, col_q, col_k, col_v, col_gate=None,
              other=None, norm_w=None, hps=HEADS):
    bsz, seq, _ = proj.shape
    lb = _pick(seq, 512, LANE)
    assert lb % (2 * GLA_CHUNK) == 0 and HEADS % hps == 0
    nblk = seq // lb
    final = other is not None
    qk, width = hps * dk, hps * dv

    def tok(i):
        return (nblk - 1 - i) if reverse else i

    def col_spec(w, col0):
        return pl.BlockSpec((None, lb, w), lambda b, h, i: (b, tok(i), col0 // w + h))

    d = 1 if reverse else 0
    in_specs = [col_spec(qk, col_q), col_spec(qk, col_k), col_spec(width, col_v),
                pl.BlockSpec((None, lb, LANE), lambda b, h, i: (b, tok(i), 0)),
                pl.BlockSpec((None, None, LANE, qk), lambda b, h, i: (l, d, 0, h)),
                pl.BlockSpec((None, None, 1, qk), lambda b, h, i: (l, d, 0, h))]
    args = [proj, proj, proj, lr, wlr, blr]
    if final:
        in_specs += [col_spec(width, col_gate),
                     pl.BlockSpec((None, lb, width), lambda b, h, i: (b, tok(i), h)),
                     pl.BlockSpec((None, 1, width), lambda b, h, i: (l, 0, h))]
        args += [proj, other, norm_w]
    return pl.pallas_call(
        functools.partial(_gla_body, reverse=reverse, final=final, lb=lb, hps=hps),
        grid=(bsz, HEADS // hps, nblk),
        in_specs=in_specs,
        out_specs=pl.BlockSpec((None, lb, width), lambda b, h, i: (b, tok(i), h)),
        out_shape=jax.ShapeDtypeStruct((bsz, seq, HEADS * dv), BF16 if final else F32),
        scratch_shapes=[pltpu.VMEM((hps, dk, dv), F32)],
        compiler_params=_params(("parallel", "parallel", "arbitrary")),
        name="gla_fwd" if final else "gla_bwd",
    )(*args)


def _segment_cumsum_lanes(x, seg, reverse):
    n = x.shape[-1]
    pos = lax.broadcasted_iota(jnp.int32, x.shape, x.ndim - 1) % seg
    s = 1
    while s < seg:
        if reverse:
            x = x + jnp.where(pos < seg - s, pltpu.roll(x, n - s, x.ndim - 1), 0.0)
        else:
            x = x + jnp.where(pos >= s, pltpu.roll(x, s, x.ndim - 1), 0.0)
        s *= 2
    return x


def _mlstm_body(q_ref, k_ref, v_ref, gr_ref, br_ref, *rest, reverse, final, lb, hps):
    if final:
        mo_ref, hb_ref, nw_ref, o_ref, c_ref, m_ref = rest
    else:
        o_ref, c_ref, m_ref = rest
    dk = q_ref.shape[-1] // hps
    dv = v_ref.shape[-1] // hps
    chunk = min(MLSTM_CHUNK, lb)

    @pl.when(pl.program_id(2) == 0)
    def _():
        c_ref[...] = jnp.zeros_like(c_ref)
        m_ref[...] = jnp.full_like(m_ref, NEG_BIG)

    ti, tf = (2, 3) if reverse else (0, 1)
    gr = gr_ref[...] + br_ref[:, 0:1]
    bcum = _segment_cumsum_lanes(_log_sigmoid(gr), chunk, reverse)
    row_id = lax.broadcasted_iota(jnp.int32, gr.shape, 0) % SUBLANE
    rowform = jnp.where(row_id == tf, bcum, gr)
    pad_rows = jnp.zeros((LANE - SUBLANE, chunk), F32)

    mask = _causal_mask(chunk, reverse)
    ones_col = jnp.where(lax.broadcasted_iota(jnp.int32, (chunk, LANE), 1) == 0, 1.0, 0.0).astype(BF16)
    scale = dk ** -0.5
    nch = lb // chunk
    for ch in (range(nch - 1, -1, -1) if reverse else range(nch)):
        rows = slice(ch * chunk, (ch + 1) * chunk)
        for hh in range(hps):
            kc = slice(hh * dk, (hh + 1) * dk)
            vc = slice(hh * dv, (hh + 1) * dv)
            rf = rowform[hh * SUBLANE:(hh + 1) * SUBLANE, rows]
            u_row = rf[ti:ti + 1, :] - rf[tf:tf + 1, :]
            colform = jnp.transpose(jnp.concatenate([rf, pad_rows], axis=0))
            bc = colform[:, tf:tf + 1]
            ic = colform[:, ti:ti + 1]
            gtot = bc[0:1, :] if reverse else bc[chunk - 1:chunk, :]
            log_d = jnp.where(mask, bc + u_row, NEG_BIG)
            m_intra = jnp.max(log_d, axis=1, keepdims=True)
            q = q_ref[rows, kc]
            k = k_ref[rows, kc]
            v_aug = jnp.concatenate([v_ref[rows, vc], ones_col], axis=1)
            s = lax.dot_general(q, k, NT_DIMS, preferred_element_type=F32) * jnp.exp(log_d - m_intra)
            intra = _dot(s.astype(BF16), v_aug)
            c = c_ref[hh]
            inter = _dot(q, c.astype(BF16))
            m_prev = m_ref[hh, 0:1, 0:1]
            bm = bc + m_prev
            m_j = jnp.maximum(bm, m_intra)
            comb = (scale * jnp.exp(bm - m_j)) * inter + (scale * jnp.exp(m_intra - m_j)) * intra
            denom = jnp.maximum(jnp.abs(comb[:, dv:dv + 1]), jnp.exp(-m_j))
            h = comb[:, :dv] * (1.0 / denom)
            a_col = gtot - bc + ic
            m_new = jnp.maximum(gtot + m_prev, jnp.max(a_col, axis=0, keepdims=True))
            kw = (k.astype(F32) * jnp.exp(a_col - m_new)).astype(BF16)
            c_ref[hh] = (jnp.exp(gtot + m_prev - m_new) * c
                         + lax.dot_general(kw, v_aug, TN_DIMS, preferred_element_type=F32))
            m_ref[hh] = jnp.broadcast_to(m_new, (SUBLANE, LANE))
            if final:
                h = _head_norm_gate(h, hb_ref[rows, vc], _sigmoid(mo_ref[rows, vc].astype(F32)), nw_ref[:, vc])
            o_ref[rows, vc] = h.astype(o_ref.dtype)


def _mlstm_scan(proj, gates_t, bias_r, l, *, reverse, dk, dv, col_q, col_k, col_v,
                col_gate=None, other=None, norm_w=None, hps=HEADS):
    bsz, seq, _ = proj.shape
    lb = _pick(seq, 512, LANE)
    nblk = seq // lb
    final = other is not None
    assert HEADS % hps == 0
    qk, width = hps * dk, hps * dv

    def tok(i):
        return (nblk - 1 - i) if reverse else i

    def col_spec(w, col0):
        return pl.BlockSpec((None, lb, w), lambda b, h, i: (b, tok(i), col0 // w + h))

    in_specs = [col_spec(qk, col_q), col_spec(qk, col_k), col_spec(width, col_v),
                pl.BlockSpec((hps * SUBLANE, lb), lambda b, h, i: (h, b * nblk + tok(i))),
                pl.BlockSpec((None, hps * SUBLANE, LANE), lambda b, h, i: (l, h, 0))]
    args = [proj, proj, proj, gates_t, bias_r]
    if final:
        in_specs += [col_spec(width, col_gate),
                     pl.BlockSpec((None, lb, width), lambda b, h, i: (b, tok(i), h)),
                     pl.BlockSpec((None, 1, width), lambda b, h, i: (l, 0, h))]
        args += [proj, other, norm_w]
    return pl.pallas_call(
        functools.partial(_mlstm_body, reverse=reverse, final=final, lb=lb, hps=hps),
        grid=(bsz, HEADS // hps, nblk),
        in_specs=in_specs,
        out_specs=pl.BlockSpec((None, lb, width), lambda b, h, i: (b, tok(i), h)),
        out_shape=jax.ShapeDtypeStruct((bsz, seq, HEADS * dv), BF16 if final else F32),
        scratch_shapes=[pltpu.VMEM((hps, dk, dv + LANE), F32), pltpu.VMEM((hps, SUBLANE, LANE), F32)],
        compiler_params=_params(("parallel", "parallel", "arbitrary")),
        name="mlstm_fwd" if final else "mlstm_bwd",
    )(*args)


def _xattn_body(x_ref, nw_ref, wq_ref, kv_ref, wo_ref, nw2_ref, o_ref, n_ref, *, heads):
    x = x_ref[...]
    xn = (x * lax.rsqrt(jnp.mean(x * x, axis=-1, keepdims=True) + EPS) * nw_ref[...]).astype(BF16)
    width = wq_ref.shape[1]
    hd = width // heads
    q = (_dot(xn, wq_ref[...]) * (hd ** -0.5)).astype(BF16)
    outs = []
    for h in range(heads):
        kh = kv_ref[:, h * hd:(h + 1) * hd]
        vh = kv_ref[:, width + h * hd:width + (h + 1) * hd]
        s = lax.dot_general(q[:, h * hd:(h + 1) * hd], kh, NT_DIMS, preferred_element_type=F32)
        p = jnp.exp(s - jnp.max(s, axis=-1, keepdims=True))
        p = p * (1.0 / jnp.sum(p, axis=-1, keepdims=True))
        outs.append(_dot(p.astype(BF16), vh))
    o = jnp.concatenate(outs, axis=1).astype(BF16)
    y = x + _dot(o, wo_ref[...])
    o_ref[...] = y
    n_ref[...] = (y * lax.rsqrt(jnp.mean(y * y, axis=-1, keepdims=True) + EPS) * nw2_ref[...]).astype(BF16)


def _xattn(x, norm_w, wq, kv, wo, next_norm_w, l, tq=512):
    bsz, seq, d = x.shape
    mem = kv.shape[1]
    width = wq.shape[2]
    tq = _pick(seq, tq, 16)
    return pl.pallas_call(
        functools.partial(_xattn_body, heads=XATTN_HEADS),
        grid=(bsz, seq // tq),
        in_specs=[pl.BlockSpec((None, tq, d), lambda b, i: (b, i, 0)),
                  pl.BlockSpec((None, 1, d), lambda b, i: (l, 0, 0)),
                  pl.BlockSpec((None, d, width), lambda b, i: (l, 0, 0), pipeline_mode=pl.Buffered(1)),
                  pl.BlockSpec((None, mem, 2 * width), lambda b, i: (b, 0, 0)),
                  pl.BlockSpec((None, width, d), lambda b, i: (l, 0, 0), pipeline_mode=pl.Buffered(1)),
                  pl.BlockSpec((None, 1, d), lambda b, i: (l, 0, 0))],
        out_specs=[pl.BlockSpec((None, tq, d), lambda b, i: (b, i, 0)),
                   pl.BlockSpec((None, tq, d), lambda b, i: (b, i, 0))],
        out_shape=[jax.ShapeDtypeStruct((bsz, seq, d), F32),
                   jax.ShapeDtypeStruct((bsz, seq, d), BF16)],
        compiler_params=_params(("parallel", "arbitrary")),
        name="mem_xattn",
    )(x, norm_w, wq, kv, wo, next_norm_w)


def _norm_w(w):
    return w.astype(F32).reshape(w.shape[0], 1, w.shape[1])


def _ffn(x, xn, w_gate, w_up, w_down, l, next_gain=None, up_split=4):
    hid = _swiglu_up(xn, w_gate, w_up, l, nsplit=up_split)
    return _matmul_residual(hid, w_down, l, x, 0.5, next_gain=next_gain, name="ffn_down")


def _mixer_weights(d, w_in, gla_w_lr, gla_b_lr, mlstm_gate_b):
    depth = w_in.shape[0]
    half = d // 2
    qk = half // 2
    o_lr = 2 * qk + 2 * half
    o_m = o_lr + 2 * GLA_RANK
    o_mg = o_m + 2 * qk + 2 * half
    w_gla = w_in[:, :, :o_lr].astype(BF16)
    w_ml = w_in[:, :, o_m:o_mg].astype(BF16)
    w_lr = jnp.pad(w_in[:, :, o_lr:o_m], ((0, 0), (0, 0), (0, LANE - 2 * GLA_RANK))).astype(BF16)
    mg = w_in[:, :, o_mg:o_mg + 4 * HEADS].reshape(depth, d, 4, HEADS).transpose(0, 3, 2, 1)
    w_gates_t = jnp.pad(mg, ((0, 0), (0, 0), (0, SUBLANE - 4), (0, 0))).reshape(depth, HEADS * SUBLANE, d)
    w_gates_t = w_gates_t.astype(BF16)
    bias_r = jnp.pad(mlstm_gate_b.astype(F32).transpose(0, 2, 1), ((0, 0), (0, 0), (0, SUBLANE - 4)))
    bias_r = jnp.broadcast_to(bias_r.reshape(depth, HEADS * SUBLANE, 1), (depth, HEADS * SUBLANE, LANE))
    wlr = jnp.stack([jnp.pad(gla_w_lr[:, 0], ((0, 0), (0, LANE - GLA_RANK), (0, 0))),
                     jnp.pad(gla_w_lr[:, 1], ((0, 0), (GLA_RANK, LANE - 2 * GLA_RANK), (0, 0)))],
                    axis=1).astype(F32)
    blr = gla_b_lr.astype(F32).reshape(depth, 2, 1, qk)
    return w_gla, w_ml, w_lr, w_gates_t, bias_r, wlr, blr


def _mixer(x, xn, bsz, seq, l, w_gla, w_ml, w_lr, w_gates_t, bias_r, wlr, blr, gla_norm, mlstm_norm, w_out,
           in_tiles=(1024, 1024), out_tiles=(1024, 512)):
    m, d = x.shape
    half = d // 2
    dv = half // HEADS
    dk = dv // 2
    qk = HEADS * dk
    tm, tn = in_tiles
    proj_g = _matmul(xn, w_gla, l, BF16, tm=tm, tn=tn, name="gla_in_proj").reshape(bsz, seq, -1)
    proj_m = _matmul(xn, w_ml, l, BF16, tm=tm, tn=tn, name="mlstm_in_proj").reshape(bsz, seq, -1)
    lr, gates_t = _gate_proj(xn, w_lr, w_gates_t, l)
    lr = lr.reshape(bsz, seq, LANE)

    kw = dict(dk=dk, dv=dv, col_q=0, col_k=qk, col_v=2 * qk)
    g_b = _gla_scan(proj_g, lr, wlr, blr, l, reverse=True, **kw)
    g_out = _gla_scan(proj_g, lr, wlr, blr, l, reverse=False, col_gate=2 * qk + half, other=g_b,
                      norm_w=gla_norm, **kw)
    m_b = _mlstm_scan(proj_m, gates_t, bias_r, l, reverse=True, **kw)
    m_out = _mlstm_scan(proj_m, gates_t, bias_r, l, reverse=False, col_gate=2 * qk + half,
                        other=m_b, norm_w=mlstm_norm, hps=MLSTM_FWD_HEADS_PER_STEP, **kw)
    return _out_proj(g_out.reshape(m, half), m_out.reshape(m, half), w_out, l, x,
                     tm=out_tiles[0], tn=out_tiles[1])


def kernel(x_prompt, x_sample, mem_prompt, mem_sample, ffn1_norm, ffn1_w_gate, ffn1_w_up, ffn1_w_down, mix_norm, w_in, gla_w_lr, gla_b_lr, gla_out_norm, mlstm_gate_b, mlstm_out_norm, w_out, xattn_norm, mem_norm, xattn_wq, xattn_wk, xattn_wv, xattn_wo, ffn2_norm, ffn2_w_gate, ffn2_w_up, ffn2_w_down, final_norm):
    assert x_prompt.shape[1:] == x_sample.shape[1:] and mem_prompt.shape[1:] == mem_sample.shape[1:]
    n_p, seq, d = x_prompt.shape
    n_s = x_sample.shape[0]
    bsz = n_p + n_s
    m = bsz * seq
    x = (x_prompt.reshape(n_p * seq, d), x_sample.reshape(n_s * seq, d))
    mem = jnp.concatenate([mem_prompt, mem_sample], axis=0)
    n_mem = mem.shape[1]
    mem = mem.reshape(bsz * n_mem, d)
    depth = ffn1_norm.shape[0]

    ffn1 = (ffn1_w_gate.astype(BF16), ffn1_w_up.astype(BF16), ffn1_w_down.astype(BF16))
    ffn2 = (ffn2_w_gate.astype(BF16), ffn2_w_up.astype(BF16), ffn2_w_down.astype(BF16))
    mixer_w = _mixer_weights(d, w_in, gla_w_lr, gla_b_lr, mlstm_gate_b)
    mixer_rest = (_norm_w(gla_out_norm), _norm_w(mlstm_out_norm), w_out.astype(BF16))
    w_kv = jnp.concatenate([xattn_wk, xattn_wv], axis=2).astype(BF16)
    wq = xattn_wq.astype(BF16)
    wo = xattn_wo.astype(BF16)
    ffn1_n, ffn2_n = _norm_w(ffn1_norm), _norm_w(ffn2_norm)
    mix_n, xattn_n, mem_n = _norm_w(mix_norm), _norm_w(xattn_norm), _norm_w(mem_norm)

    xn = _rmsnorm_pair(*x, ffn1_n, 0, BF16)
    for l in range(depth):
        tiles = dict(in_tiles=(2048, 512), out_tiles=(2048, 256)) if l == depth - 1 else {}
        x, xn = _ffn(x, xn, *ffn1, l, next_gain=(mix_n, l))
        x = _mixer(x, xn, bsz, seq, l, *mixer_w, *mixer_rest, **tiles)
        kv = _matmul(_rmsnorm(mem, mem_n, l, BF16), w_kv, l, BF16, tm=512, name="mem_kv_proj")
        x, xn = _xattn(x.reshape(bsz, seq, d), xattn_n, wq, kv.reshape(bsz, n_mem, -1), wo, ffn2_n, l)
        x, xn = x.reshape(m, d), xn.reshape(m, d)
        if l + 1 < depth:
            x, xn = _ffn(x, xn, *ffn2, l, next_gain=(ffn1_n, l + 1))
        else:
            x = _ffn(x, xn, *ffn2, l)
    fin = _norm_w(final_norm.reshape(1, d))
    y_p = _rmsnorm(x, fin, 0, F32, 0, n_p * seq).reshape(n_p, seq, d)
    y_s = _rmsnorm(x, fin, 0, F32, n_p * seq, n_s * seq).reshape(n_s, seq, d)
    return (y_p, y_s)
```

```python
import functools

import jax
import jax.numpy as jnp
from jax import lax
from jax.experimental import pallas as pl
from jax.experimental.pallas import tpu as pltpu

F32 = jnp.float32
BF16 = jnp.bfloat16

EPS = 1e-6
NEG_BIG = -1e30
HEADS = 4
GLA_RANK = 16
GLA_TAU = 16.0
GLA_MIN_LOG_DECAY = -1.0
GLA_CHUNK = 64
MLSTM_CHUNK = 256
MLSTM_FWD_HEADS_PER_STEP = 2
XATTN_HEADS = 4
LANE = 128
SUBLANE = 8
VMEM_LIMIT = 56 * 1024 * 1024

NT_DIMS = (((1,), (1,)), ((), ()))
TN_DIMS = (((0,), (0,)), ((), ()))


def _pick(n, target, mult):
    if n <= target:
        return n
    t = (target // mult) * mult
    while t >= mult:
        if n % t == 0:
            return t
        t -= mult
    return n


def _params(sem):
    return pltpu.CompilerParams(dimension_semantics=sem, vmem_limit_bytes=VMEM_LIMIT)


def _log_sigmoid(x):
    return jnp.minimum(x, 0.0) - jnp.log(1.0 + jnp.exp(-jnp.abs(x)))


def _sigmoid(x):
    return 0.5 * jnp.tanh(0.5 * x) + 0.5


def _split_bf16(x):
    hi = x.astype(BF16)
    return hi, (x - hi.astype(F32)).astype(BF16)


def _dot(a, b):
    return jnp.dot(a, b, preferred_element_type=F32)


def _rmsnorm_body(x_ref, w_ref, o_ref):
    x = x_ref[...]
    ms = jnp.mean(x * x, axis=-1, keepdims=True)
    o_ref[...] = (x * lax.rsqrt(ms + EPS) * w_ref[...]).astype(o_ref.dtype)


def _pair_specs(block, na, col=lambda *g: 0):
    return [pl.BlockSpec(block, lambda *g: (jnp.minimum(g[0], na - 1), col(*g))),
            pl.BlockSpec(block, lambda *g: (jnp.maximum(g[0] - na, 0), col(*g)))]


def _rmsnorm_pair_body(xa_ref, xb_ref, w_ref, o_ref, *, na):
    @pl.when(pl.program_id(0) < na)
    def _():
        _rmsnorm_body(xa_ref, w_ref, o_ref)

    @pl.when(pl.program_id(0) >= na)
    def _():
        _rmsnorm_body(xb_ref, w_ref, o_ref)


def _rmsnorm_pair(xa, xb, w, l, out_dtype):
    d = xa.shape[1]
    tr = _pick(xa.shape[0], 256, SUBLANE * 2)
    assert xa.shape[0] % tr == 0 and xb.shape[0] % tr == 0
    na, nb = xa.shape[0] // tr, xb.shape[0] // tr
    return pl.pallas_call(
        functools.partial(_rmsnorm_pair_body, na=na),
        grid=(na + nb,),
        in_specs=_pair_specs((tr, d), na) + [pl.BlockSpec((None, 1, d), lambda i: (l, 0, 0))],
        out_specs=pl.BlockSpec((tr, d), lambda i: (i, 0)),
        out_shape=jax.ShapeDtypeStruct(((na + nb) * tr, d), out_dtype),
        compiler_params=_params(("parallel",)),
        name="rmsnorm",
    )(xa, xb, w)


def _rmsnorm(x, w, l, out_dtype, row_start=0, rows=None):
    m, d = x.shape
    rows = m if rows is None else rows
    tr = _pick(rows, 256, SUBLANE * 2)
    assert row_start % tr == 0
    off = row_start // tr
    return pl.pallas_call(
        _rmsnorm_body,
        grid=(rows // tr,),
        in_specs=[pl.BlockSpec((tr, d), lambda i: (i + off, 0)),
                  pl.BlockSpec((None, 1, d), lambda i: (l, 0, 0))],
        out_specs=pl.BlockSpec((tr, d), lambda i: (i, 0)),
        out_shape=jax.ShapeDtypeStruct((rows, d), out_dtype),
        compiler_params=_params(("parallel",)),
        name="rmsnorm",
    )(x, w)


def _row_scale(rs_ref, width):
    return jnp.concatenate([rs_ref[...]] * (width // LANE), axis=1)


def _split_act(a):
    return a if isinstance(a, tuple) else (a, None)


def _ss_spec(tm, ss):
    return [] if ss is None else [pl.BlockSpec((tm, LANE), lambda i, *_: (i, 0))]


def _mm_body(a_ref, b_ref, *rest):
    *ss_ref, o_ref = rest
    p = _dot(a_ref[...], b_ref[...])
    if ss_ref:
        p = p * _row_scale(ss_ref[0], p.shape[1])
    o_ref[...] = p.astype(o_ref.dtype)


def _matmul(a, w, l, out_dtype, tm=1024, tn=1024, name="matmul"):
    a, ss = _split_act(a)
    m, k = a.shape
    n = w.shape[2]
    tm = _pick(m, tm, 16)
    tn = _pick(n, tn, LANE)
    return pl.pallas_call(
        _mm_body,
        grid=(m // tm, n // tn),
        in_specs=[pl.BlockSpec((tm, k), lambda i, j: (i, 0)),
                  pl.BlockSpec((None, k, tn), lambda i, j: (l, 0, j))] + _ss_spec(tm, ss),
        out_specs=pl.BlockSpec((tm, tn), lambda i, j: (i, j)),
        out_shape=jax.ShapeDtypeStruct((m, n), out_dtype),
        compiler_params=_params(("parallel", "arbitrary")),
        name=name,
    )(a, w, *([] if ss is None else [ss]))


SWIGLU_ROW_TILES = 4


def _swiglu_body(a_ref, wg_ref, wu_ref, *rest):
    *ss_ref, o_ref = rest
    rows = a_ref.shape[0] // SWIGLU_ROW_TILES
    for t in range(SWIGLU_ROW_TILES):
        r = slice(t * rows, (t + 1) * rows)
        a = a_ref[r, :]
        g = _dot(a, wg_ref[...])
        u = _dot(a, wu_ref[...])
        if ss_ref:
            rs = jnp.concatenate([ss_ref[0][r, :]] * (g.shape[1] // LANE), axis=1)
            g, u = g * rs, u * rs
        o_ref[r, :] = (g * _sigmoid(g) * u).astype(o_ref.dtype)


def _swiglu_up(a, wg, wu, l, tm=2048, tn=256):
    a, ss = _split_act(a)
    m, k = a.shape
    n = wg.shape[2]
    tm = _pick(m, tm, 16 * SWIGLU_ROW_TILES)
    tn = min(tn, n)
    return pl.pallas_call(
        _swiglu_body,
        grid=(m // tm, pl.cdiv(n, tn)),
        in_specs=[pl.BlockSpec((tm, k), lambda i, j: (i, 0)),
                  pl.BlockSpec((None, k, tn), lambda i, j: (l, 0, j)),
                  pl.BlockSpec((None, k, tn), lambda i, j: (l, 0, j))] + _ss_spec(tm, ss),
        out_specs=pl.BlockSpec((tm, tn), lambda i, j: (i, j)),
        out_shape=jax.ShapeDtypeStruct((m, n), BF16),
        compiler_params=_params(("parallel", "arbitrary")),
        name="swiglu_up",
    )(a, wg, wu, *([] if ss is None else [ss]))


def _mm_res_body(a_ref, b_ref, *rest, scale, na, fuse, width):
    n_res = 1 if na is None else 2
    res_refs, rest = rest[:n_res], rest[n_res:]
    if na is None:
        r = res_refs[0][...]
    else:
        r = jnp.where(pl.program_id(0) < na, res_refs[0][...], res_refs[1][...])
    y = r + scale * _dot(a_ref[...], b_ref[...])
    if not fuse:
        rest[0][...] = y
        return
    gain_ref, o_ref, yw_ref, ss_ref = rest
    o_ref[...] = y
    yw_ref[...] = (y * gain_ref[...]).astype(yw_ref.dtype)
    sq = y * y
    part = sq[:, 0:LANE]
    for t in range(1, sq.shape[1] // LANE):
        part = part + sq[:, t * LANE:(t + 1) * LANE]

    @pl.when(pl.program_id(1) == 0)
    def _():
        ss_ref[...] = part

    @pl.when(pl.program_id(1) > 0)
    def _():
        ss_ref[...] += part

    @pl.when(pl.program_id(1) == pl.num_programs(1) - 1)
    def _():
        tot = jnp.sum(ss_ref[...], axis=1, keepdims=True)
        ss_ref[...] = jnp.broadcast_to(lax.rsqrt(tot * (1.0 / width) + EPS), ss_ref.shape)


def _matmul_residual(a, w, l, res, scale, tm=512, tn=512, next_gain=None, name="matmul_res"):
    m, k = a.shape
    n = w.shape[2]
    tm = _pick(m, tm, 16)
    tn = _pick(n, tn, LANE)
    if isinstance(res, tuple):
        assert res[0].shape[0] % tm == 0 and res[1].shape[0] % tm == 0
        na = res[0].shape[0] // tm
        res_specs = _pair_specs((tm, tn), na, col=lambda i, j: j)
    else:
        na, res = None, (res,)
        res_specs = [pl.BlockSpec((tm, tn), lambda i, j: (i, j))]
    in_specs = [pl.BlockSpec((tm, k), lambda i, j: (i, 0)),
                pl.BlockSpec((None, k, tn), lambda i, j: (l, 0, j))] + res_specs
    out_specs = pl.BlockSpec((tm, tn), lambda i, j: (i, j))
    out_shape = jax.ShapeDtypeStruct((m, n), F32)
    args = [a, w, *res]
    fuse = next_gain is not None
    if fuse:
        gains, gl = next_gain
        in_specs.append(pl.BlockSpec((None, 1, tn), lambda i, j: (gl, 0, j)))
        args.append(gains)
        out_specs = [out_specs, pl.BlockSpec((tm, tn), lambda i, j: (i, j)),
                     pl.BlockSpec((tm, LANE), lambda i, j: (i, 0))]
        out_shape = [out_shape, jax.ShapeDtypeStruct((m, n), BF16), jax.ShapeDtypeStruct((m, LANE), F32)]
    out = pl.pallas_call(
        functools.partial(_mm_res_body, scale=scale, na=na, fuse=fuse, width=n),
        grid=(m // tm, n // tn),
        in_specs=in_specs,
        out_specs=out_specs,
        out_shape=out_shape,
        compiler_params=_params(("parallel", "arbitrary")),
        name=name,
    )(*args)
    return (out[0], (out[1], out[2])) if fuse else out


def _mm2_res_body(a1_ref, a2_ref, b1_ref, b2_ref, r_ref, o_ref):
    o_ref[...] = r_ref[...] + (_dot(a1_ref[...], b1_ref[...]) + _dot(a2_ref[...], b2_ref[...]))


def _out_proj(a1, a2, w, l, res, tm=1024, tn=512):
    m, k1 = a1.shape
    k2 = a2.shape[1]
    assert k1 == k2 and w.shape[1] == k1 + k2
    n = w.shape[2]
    tm = _pick(m, tm, 16)
    tn = _pick(n, tn, LANE)
    return pl.pallas_call(
        _mm2_res_body,
        grid=(m // tm, n // tn),
        in_specs=[pl.BlockSpec((tm, k1), lambda i, j: (i, 0)),
                  pl.BlockSpec((tm, k2), lambda i, j: (i, 0)),
                  pl.BlockSpec((None, k1, tn), lambda i, j: (l, 0, j)),
                  pl.BlockSpec((None, k2, tn), lambda i, j: (l, 1, j)),
                  pl.BlockSpec((tm, tn), lambda i, j: (i, j))],
        out_specs=pl.BlockSpec((tm, tn), lambda i, j: (i, j)),
        out_shape=jax.ShapeDtypeStruct((m, n), F32),
        compiler_params=_params(("parallel", "arbitrary")),
        name="mixer_out_proj",
    )(a1, a2, w, w, res)


def _gate_proj_body(a_ref, w_ref, wt_ref, *rest):
    *ss_ref, o_ref, ot_ref = rest
    a = a_ref[...]
    o = _dot(a, w_ref[...])
    ot = lax.dot_general(wt_ref[...], a, NT_DIMS, preferred_element_type=F32)
    if ss_ref:
        o = o * _row_scale(ss_ref[0], o.shape[1])
        ot = ot * jnp.transpose(ss_ref[0][...])[0:1, :]
    o_ref[...] = o
    ot_ref[...] = ot


def _gate_proj(a, w, wt, l, tm=1024):
    a, ss = _split_act(a)
    m, k = a.shape
    n = w.shape[2]
    nt = wt.shape[1]
    tm = _pick(m, tm, LANE)
    return pl.pallas_call(
        _gate_proj_body,
        grid=(m // tm,),
        in_specs=[pl.BlockSpec((tm, k), lambda i: (i, 0)),
                  pl.BlockSpec((None, k, n), lambda i: (l, 0, 0)),
                  pl.BlockSpec((None, nt, k), lambda i: (l, 0, 0))] + _ss_spec(tm, ss),
        out_specs=[pl.BlockSpec((tm, n), lambda i: (i, 0)),
                   pl.BlockSpec((nt, tm), lambda i: (0, i))],
        out_shape=[jax.ShapeDtypeStruct((m, n), F32),
                   jax.ShapeDtypeStruct((nt, m), F32)],
        compiler_params=_params(("parallel",)),
        name="gate_proj",
    )(a, w, wt, *([] if ss is None else [ss]))


def _causal_mask(chunk, reverse):
    r = lax.broadcasted_iota(jnp.int32, (chunk, chunk), 0)
    c = lax.broadcasted_iota(jnp.int32, (chunk, chunk), 1)
    return (c >= r) if reverse else (c <= r)


def _head_norm_gate(h, other, gate, nw):
    h = h + other
    return h * lax.rsqrt(jnp.mean(h * h, axis=-1, keepdims=True) + EPS) * nw * gate


def _gla_body(q_ref, k_ref, v_ref, lr_ref, wlr_ref, blr_ref, *rest, reverse, final, lb, hps):
    if final:
        gg_ref, ob_ref, nw_ref, o_ref, s_ref = rest
    else:
        o_ref, s_ref = rest
    dk = q_ref.shape[-1] // hps
    dv = v_ref.shape[-1] // hps
    chunk = GLA_CHUNK
    nch = lb // chunk
    qk = hps * dk

    @pl.when(pl.program_id(2) == 0)
    def _():
        s_ref[...] = jnp.zeros_like(s_ref)

    lh, ll = _split_bf16(lr_ref[...])
    wh, wl = _split_bf16(wlr_ref[...])
    z = _dot(lh, wh) + _dot(lh, wl) + _dot(ll, wh) + blr_ref[...]
    g = jnp.maximum(_log_sigmoid(z) * (1.0 / GLA_TAU), GLA_MIN_LOG_DECAY)
    mask = _causal_mask(chunk, reverse)
    tri = jnp.where(mask, 1.0, 0.0).astype(BF16)
    gh, gl = _split_bf16(jnp.concatenate([g[c * chunk:(c + 1) * chunk, :] for c in range(nch)], axis=1))
    bcum = _dot(tri, gh) + _dot(tri, gl)

    scale = dk ** -0.5
    zero_blk = jnp.zeros((chunk, chunk), BF16)
    npair = nch // 2
    for pb in (range(npair - 1, -1, -1) if reverse else range(npair)):
        rows = slice(2 * pb * chunk, (2 * pb + 2) * chunk)
        c_first, c_second = (2 * pb + 1, 2 * pb) if reverse else (2 * pb, 2 * pb + 1)
        for hh in range(hps):
            kc = slice(hh * dk, (hh + 1) * dk)
            vc = slice(hh * dv, (hh + 1) * dv)

            def chunk_terms(c):
                b = bcum[:, c * qk + hh * dk:c * qk + (hh + 1) * dk]
                tot = b[0:1, :] if reverse else b[chunk - 1:chunk, :]
                r = slice(c * chunk, (c + 1) * chunk)
                q_dec = q_ref[r, kc].astype(F32) * (jnp.exp(b) * scale)
                k_inv = k_ref[r, kc].astype(F32) * jnp.exp(-b)
                return tot, q_dec, k_inv, k_inv * jnp.exp(tot)

            tot1, qd1, ki1, ke1 = chunk_terms(c_first)
            tot2, qd2, ki2, ke2 = chunk_terms(c_second)
            qd1b, qd2b = qd1.astype(BF16), qd2.astype(BF16)
            a11 = lax.dot_general(qd1b, ki1.astype(BF16), NT_DIMS, preferred_element_type=F32)
            a22 = lax.dot_general(qd2b, ki2.astype(BF16), NT_DIMS, preferred_element_type=F32)
            a21 = lax.dot_general(qd2b, ke1.astype(BF16), NT_DIMS, preferred_element_type=F32)
            a11 = jnp.where(mask, a11, 0.0).astype(BF16)
            a22 = jnp.where(mask, a22, 0.0).astype(BF16)
            a21 = a21.astype(BF16)
            qs2 = (qd2 * jnp.exp(tot1)).astype(BF16)
            kx1 = (ke1 * jnp.exp(tot2)).astype(BF16)
            ke2b = ke2.astype(BF16)
            if reverse:
                amat = jnp.concatenate([jnp.concatenate([a22, a21], axis=1),
                                        jnp.concatenate([zero_blk, a11], axis=1)], axis=0)
                qmat = jnp.concatenate([qs2, qd1b], axis=0)
                kmat = jnp.concatenate([ke2b, kx1], axis=0)
            else:
                amat = jnp.concatenate([jnp.concatenate([a11, zero_blk], axis=1),
                                        jnp.concatenate([a21, a22], axis=1)], axis=0)
                qmat = jnp.concatenate([qd1b, qs2], axis=0)
                kmat = jnp.concatenate([kx1, ke2b], axis=0)
            v = v_ref[rows, vc]
            s = s_ref[hh]
            o = _dot(amat, v) + _dot(qmat, s.astype(BF16))
            dec = jnp.exp(jnp.transpose(jnp.broadcast_to(tot1 + tot2, (LANE, dk))))
            dec = jnp.concatenate([dec] * (dv // LANE), axis=1)
            s_ref[hh] = s * dec + lax.dot_general(kmat, v, TN_DIMS, preferred_element_type=F32)
            if final:
                gg = gg_ref[rows, vc].astype(F32)
                o = _head_norm_gate(o, ob_ref[rows, vc], gg * _sigmoid(gg), nw_ref[:, vc])
            o_ref[rows, vc] = o.astype(o_ref.dtype)


def _gla_scan(proj, lr, wlr, blr, l, *, reverse, dk, dv, col_q, col_k, col_v, col_gate=None,
              other=None, norm_w=None):
    bsz, seq, _ = proj.shape
    lb = _pick(seq, 512, LANE)
    assert lb % (2 * GLA_CHUNK) == 0
    hps = HEADS
    nblk = seq // lb
    final = other is not None
    qk, width = hps * dk, hps * dv

    def tok(i):
        return (nblk - 1 - i) if reverse else i

    def col_spec(w, col0):
        return pl.BlockSpec((None, lb, w), lambda b, h, i: (b, tok(i), col0 // w + h))

    d = 1 if reverse else 0
    in_specs = [col_spec(qk, col_q), col_spec(qk, col_k), col_spec(width, col_v),
                pl.BlockSpec((None, lb, LANE), lambda b, h, i: (b, tok(i), 0)),
                pl.BlockSpec((None, None, LANE, qk), lambda b, h, i: (l, d, 0, h)),
                pl.BlockSpec((None, None, 1, qk), lambda b, h, i: (l, d, 0, h))]
    args = [proj, proj, proj, lr, wlr, blr]
    if final:
        in_specs += [col_spec(width, col_gate),
                     pl.BlockSpec((None, lb, width), lambda b, h, i: (b, tok(i), h)),
                     pl.BlockSpec((None, 1, width), lambda b, h, i: (l, 0, h))]
        args += [proj, other, norm_w]
    return pl.pallas_call(
        functools.partial(_gla_body, reverse=reverse, final=final, lb=lb, hps=hps),
        grid=(bsz, HEADS // hps, nblk),
        in_specs=in_specs,
        out_specs=pl.BlockSpec((None, lb, width), lambda b, h, i: (b, tok(i), h)),
        out_shape=jax.ShapeDtypeStruct((bsz, seq, HEADS * dv), BF16 if final else F32),
        scratch_shapes=[pltpu.VMEM((hps, dk, dv), F32)],
        compiler_params=_params(("parallel", "parallel", "arbitrary")),
        name="gla_fwd" if final else "gla_bwd",
    )(*args)


def _segment_cumsum_lanes(x, seg, reverse):
    n = x.shape[-1]
    pos = lax.broadcasted_iota(jnp.int32, x.shape, x.ndim - 1) % seg
    s = 1
    while s < seg:
        if reverse:
            x = x + jnp.where(pos < seg - s, pltpu.roll(x, n - s, x.ndim - 1), 0.0)
        else:
            x = x + jnp.where(pos >= s, pltpu.roll(x, s, x.ndim - 1), 0.0)
        s *= 2
    return x


def _mlstm_body(q_ref, k_ref, v_ref, gr_ref, br_ref, *rest, reverse, final, lb, hps):
    if final:
        mo_ref, hb_ref, nw_ref, o_ref, c_ref, m_ref = rest
    else:
        o_ref, c_ref, m_ref = rest
    dk = q_ref.shape[-1] // hps
    dv = v_ref.shape[-1] // hps
    chunk = min(MLSTM_CHUNK, lb)

    @pl.when(pl.program_id(2) == 0)
    def _():
        c_ref[...] = jnp.zeros_like(c_ref)
        m_ref[...] = jnp.full_like(m_ref, NEG_BIG)

    ti, tf = (2, 3) if reverse else (0, 1)
    gr = gr_ref[...] + br_ref[:, 0:1]
    bcum = _segment_cumsum_lanes(_log_sigmoid(gr), chunk, reverse)
    row_id = lax.broadcasted_iota(jnp.int32, gr.shape, 0) % SUBLANE
    rowform = jnp.where(row_id == tf, bcum, gr)
    pad_rows = jnp.zeros((LANE - SUBLANE, chunk), F32)

    mask = _causal_mask(chunk, reverse)
    ones_col = jnp.where(lax.broadcasted_iota(jnp.int32, (chunk, LANE), 1) == 0, 1.0, 0.0).astype(BF16)
    scale = dk ** -0.5
    nch = lb // chunk
    for ch in (range(nch - 1, -1, -1) if reverse else range(nch)):
        rows = slice(ch * chunk, (ch + 1) * chunk)
        for hh in range(hps):
            kc = slice(hh * dk, (hh + 1) * dk)
            vc = slice(hh * dv, (hh + 1) * dv)
            rf = rowform[hh * SUBLANE:(hh + 1) * SUBLANE, rows]
            u_row = rf[ti:ti + 1, :] - rf[tf:tf + 1, :]
            colform = jnp.transpose(jnp.concatenate([rf, pad_rows], axis=0))
            bc = colform[:, tf:tf + 1]
            ic = colform[:, ti:ti + 1]
            gtot = bc[0:1, :] if reverse else bc[chunk - 1:chunk, :]
            log_d = jnp.where(mask, bc + u_row, NEG_BIG)
            m_intra = jnp.max(log_d, axis=1, keepdims=True)
            q = q_ref[rows, kc]
            k = k_ref[rows, kc]
            v_aug = jnp.concatenate([v_ref[rows, vc], ones_col], axis=1)
            s = lax.dot_general(q, k, NT_DIMS, preferred_element_type=F32) * jnp.exp(log_d - m_intra)
            intra = _dot(s.astype(BF16), v_aug)
            c = c_ref[hh]
            inter = _dot(q, c.astype(BF16))
            m_prev = m_ref[hh, 0:1, 0:1]
            bm = bc + m_prev
            m_j = jnp.maximum(bm, m_intra)
            comb = (scale * jnp.exp(bm - m_j)) * inter + (scale * jnp.exp(m_intra - m_j)) * intra
            denom = jnp.maximum(jnp.abs(comb[:, dv:dv + 1]), jnp.exp(-m_j))
            h = comb[:, :dv] * (1.0 / denom)
            a_col = gtot - bc + ic
            m_new = jnp.maximum(gtot + m_prev, jnp.max(a_col, axis=0, keepdims=True))
            kw = (k.astype(F32) * jnp.exp(a_col - m_new)).astype(BF16)
            c_ref[hh] = (jnp.exp(gtot + m_prev - m_new) * c
                         + lax.dot_general(kw, v_aug, TN_DIMS, preferred_element_type=F32))
            m_ref[hh] = jnp.broadcast_to(m_new, (SUBLANE, LANE))
            if final:
                h = _head_norm_gate(h, hb_ref[rows, vc], _sigmoid(mo_ref[rows, vc].astype(F32)), nw_ref[:, vc])
            o_ref[rows, vc] = h.astype(o_ref.dtype)


def _mlstm_scan(proj, gates_t, bias_r, l, *, reverse, dk, dv, col_q, col_k, col_v,
                col_gate=None, other=None, norm_w=None, hps=HEADS):
    bsz, seq, _ = proj.shape
    lb = _pick(seq, 512, LANE)
    nblk = seq // lb
    final = other is not None
    assert HEADS % hps == 0
    qk, width = hps * dk, hps * dv

    def tok(i):
        return (nblk - 1 - i) if reverse else i

    def col_spec(w, col0):
        return pl.BlockSpec((None, lb, w), lambda b, h, i: (b, tok(i), col0 // w + h))

    in_specs = [col_spec(qk, col_q), col_spec(qk, col_k), col_spec(width, col_v),
                pl.BlockSpec((hps * SUBLANE, lb), lambda b, h, i: (h, b * nblk + tok(i))),
                pl.BlockSpec((None, hps * SUBLANE, LANE), lambda b, h, i: (l, h, 0))]
    args = [proj, proj, proj, gates_t, bias_r]
    if final:
        in_specs += [col_spec(width, col_gate),
                     pl.BlockSpec((None, lb, width), lambda b, h, i: (b, tok(i), h)),
                     pl.BlockSpec((None, 1, width), lambda b, h, i: (l, 0, h))]
        args += [proj, other, norm_w]
    return pl.pallas_call(
        functools.partial(_mlstm_body, reverse=reverse, final=final, lb=lb, hps=hps),
        grid=(bsz, HEADS // hps, nblk),
        in_specs=in_specs,
        out_specs=pl.BlockSpec((None, lb, width), lambda b, h, i: (b, tok(i), h)),
        out_shape=jax.ShapeDtypeStruct((bsz, seq, HEADS * dv), BF16 if final else F32),
        scratch_shapes=[pltpu.VMEM((hps, dk, dv + LANE), F32), pltpu.VMEM((hps, SUBLANE, LANE), F32)],
        compiler_params=_params(("parallel", "parallel", "arbitrary")),
        name="mlstm_fwd" if final else "mlstm_bwd",
    )(*args)


def _xattn_body(x_ref, nw_ref, wq_ref, kv_ref, wo_ref, nw2_ref, o_ref, n_ref, *, heads):
    x = x_ref[...]
    xn = (x * lax.rsqrt(jnp.mean(x * x, axis=-1, keepdims=True) + EPS) * nw_ref[...]).astype(BF16)
    width = wq_ref.shape[1]
    hd = width // heads
    q = (_dot(xn, wq_ref[...]) * (hd ** -0.5)).astype(BF16)
    outs = []
    for h in range(heads):
        kh = kv_ref[:, h * hd:(h + 1) * hd]
        vh = kv_ref[:, width + h * hd:width + (h + 1) * hd]
        s = lax.dot_general(q[:, h * hd:(h + 1) * hd], kh, NT_DIMS, preferred_element_type=F32)
        p = jnp.exp(s - jnp.max(s, axis=-1, keepdims=True))
        p = p * (1.0 / jnp.sum(p, axis=-1, keepdims=True))
        outs.append(_dot(p.astype(BF16), vh))
    o = jnp.concatenate(outs, axis=1).astype(BF16)
    y = x + _dot(o, wo_ref[...])
    o_ref[...] = y
    n_ref[...] = (y * lax.rsqrt(jnp.mean(y * y, axis=-1, keepdims=True) + EPS) * nw2_ref[...]).astype(BF16)


def _xattn(x, norm_w, wq, kv, wo, next_norm_w, l, tq=512):
    bsz, seq, d = x.shape
    mem = kv.shape[1]
    width = wq.shape[2]
    tq = _pick(seq, tq, 16)
    return pl.pallas_call(
        functools.partial(_xattn_body, heads=XATTN_HEADS),
        grid=(bsz, seq // tq),
        in_specs=[pl.BlockSpec((None, tq, d), lambda b, i: (b, i, 0)),
                  pl.BlockSpec((None, 1, d), lambda b, i: (l, 0, 0)),
                  pl.BlockSpec((None, d, width), lambda b, i: (l, 0, 0), pipeline_mode=pl.Buffered(1)),
                  pl.BlockSpec((None, mem, 2 * width), lambda b, i: (b, 0, 0)),
                  pl.BlockSpec((None, width, d), lambda b, i: (l, 0, 0), pipeline_mode=pl.Buffered(1)),
                  pl.BlockSpec((None, 1, d), lambda b, i: (l, 0, 0))],
        out_specs=[pl.BlockSpec((None, tq, d), lambda b, i: (b, i, 0)),
                   pl.BlockSpec((None, tq, d), lambda b, i: (b, i, 0))],
        out_shape=[jax.ShapeDtypeStruct((bsz, seq, d), F32),
                   jax.ShapeDtypeStruct((bsz, seq, d), BF16)],
        compiler_params=_params(("parallel", "arbitrary")),
        name="mem_xattn",
    )(x, norm_w, wq, kv, wo, next_norm_w)


def _norm_w(w):
    return w.astype(F32).reshape(w.shape[0], 1, w.shape[1])


def _ffn(x, xn, w_gate, w_up, w_down, l, next_gain=None):
    hid = _swiglu_up(xn, w_gate, w_up, l)
    return _matmul_residual(hid, w_down, l, x, 0.5, next_gain=next_gain, name="ffn_down")


def _mixer_weights(d, w_in, gla_w_lr, gla_b_lr, mlstm_gate_b):
    depth = w_in.shape[0]
    half = d // 2
    qk = half // 2
    o_lr = 2 * qk + 2 * half
    o_m = o_lr + 2 * GLA_RANK
    o_mg = o_m + 2 * qk + 2 * half
    w_gla = w_in[:, :, :o_lr].astype(BF16)
    w_ml = w_in[:, :, o_m:o_mg].astype(BF16)
    w_lr = jnp.pad(w_in[:, :, o_lr:o_m], ((0, 0), (0, 0), (0, LANE - 2 * GLA_RANK))).astype(BF16)
    mg = w_in[:, :, o_mg:o_mg + 4 * HEADS].reshape(depth, d, 4, HEADS).transpose(0, 3, 2, 1)
    w_gates_t = jnp.pad(mg, ((0, 0), (0, 0), (0, SUBLANE - 4), (0, 0))).reshape(depth, HEADS * SUBLANE, d)
    w_gates_t = w_gates_t.astype(BF16)
    bias_r = jnp.pad(mlstm_gate_b.astype(F32).transpose(0, 2, 1), ((0, 0), (0, 0), (0, SUBLANE - 4)))
    bias_r = jnp.broadcast_to(bias_r.reshape(depth, HEADS * SUBLANE, 1), (depth, HEADS * SUBLANE, LANE))
    wlr = jnp.stack([jnp.pad(gla_w_lr[:, 0], ((0, 0), (0, LANE - GLA_RANK), (0, 0))),
                     jnp.pad(gla_w_lr[:, 1], ((0, 0), (GLA_RANK, LANE - 2 * GLA_RANK), (0, 0)))],
                    axis=1).astype(F32)
    blr = gla_b_lr.astype(F32).reshape(depth, 2, 1, qk)
    return w_gla, w_ml, w_lr, w_gates_t, bias_r, wlr, blr


def _mixer(x, xn, bsz, seq, l, w_gla, w_ml, w_lr, w_gates_t, bias_r, wlr, blr, gla_norm, mlstm_norm, w_out):
    m, d = x.shape
    half = d // 2
    dv = half // HEADS
    dk = dv // 2
    qk = HEADS * dk
    proj_g = _matmul(xn, w_gla, l, BF16, name="gla_in_proj").reshape(bsz, seq, -1)
    proj_m = _matmul(xn, w_ml, l, BF16, name="mlstm_in_proj").reshape(bsz, seq, -1)
    lr, gates_t = _gate_proj(xn, w_lr, w_gates_t, l)
    lr = lr.reshape(bsz, seq, LANE)

    kw = dict(dk=dk, dv=dv, col_q=0, col_k=qk, col_v=2 * qk)
    g_b = _gla_scan(proj_g, lr, wlr, blr, l, reverse=True, **kw)
    g_out = _gla_scan(proj_g, lr, wlr, blr, l, reverse=False, col_gate=2 * qk + half, other=g_b,
                      norm_w=gla_norm, **kw)
    m_b = _mlstm_scan(proj_m, gates_t, bias_r, l, reverse=True, **kw)
    m_out = _mlstm_scan(proj_m, gates_t, bias_r, l, reverse=False, col_gate=2 * qk + half,
                        other=m_b, norm_w=mlstm_norm, hps=MLSTM_FWD_HEADS_PER_STEP, **kw)
    return _out_proj(g_out.reshape(m, half), m_out.reshape(m, half), w_out, l, x)


def kernel(x_prompt, x_sample, mem_prompt, mem_sample, ffn1_norm, ffn1_w_gate, ffn1_w_up, ffn1_w_down, mix_norm, w_in, gla_w_lr, gla_b_lr, gla_out_norm, mlstm_gate_b, mlstm_out_norm, w_out, xattn_norm, mem_norm, xattn_wq, xattn_wk, xattn_wv, xattn_wo, ffn2_norm, ffn2_w_gate, ffn2_w_up, ffn2_w_down, final_norm):
    assert x_prompt.shape[1:] == x_sample.shape[1:] and mem_prompt.shape[1:] == mem_sample.shape[1:]
    n_p, seq, d = x_prompt.shape
    n_s = x_sample.shape[0]
    bsz = n_p + n_s
    m = bsz * seq
    x = (x_prompt.reshape(n_p * seq, d), x_sample.reshape(n_s * seq, d))
    mem = jnp.concatenate([mem_prompt, mem_sample], axis=0)
    n_mem = mem.shape[1]
    mem = mem.reshape(bsz * n_mem, d)
    depth = ffn1_norm.shape[0]

    ffn1 = (ffn1_w_gate.astype(BF16), ffn1_w_up.astype(BF16), ffn1_w_down.astype(BF16))
    ffn2 = (ffn2_w_gate.astype(BF16), ffn2_w_up.astype(BF16), ffn2_w_down.astype(BF16))
    mixer_w = _mixer_weights(d, w_in, gla_w_lr, gla_b_lr, mlstm_gate_b)
    mixer_rest = (_norm_w(gla_out_norm), _norm_w(mlstm_out_norm), w_out.astype(BF16))
    w_kv = jnp.concatenate([xattn_wk, xattn_wv], axis=2).astype(BF16)
    wq = xattn_wq.astype(BF16)
    wo = xattn_wo.astype(BF16)
    ffn1_n, ffn2_n = _norm_w(ffn1_norm), _norm_w(ffn2_norm)
    mix_n, xattn_n, mem_n = _norm_w(mix_norm), _norm_w(xattn_norm), _norm_w(mem_norm)

    xn = _rmsnorm_pair(*x, ffn1_n, 0, BF16)
    for l in range(depth):
        x, xn = _ffn(x, xn, *ffn1, l, next_gain=(mix_n, l))
        x = _mixer(x, xn, bsz, seq, l, *mixer_w, *mixer_rest)
        kv = _matmul(_rmsnorm(mem, mem_n, l, BF16), w_kv, l, BF16, tm=512, name="mem_kv_proj")
        x, xn = _xattn(x.reshape(bsz, seq, d), xattn_n, wq, kv.reshape(bsz, n_mem, -1), wo, ffn2_n, l)
        x, xn = x.reshape(m, d), xn.reshape(m, d)
        if l + 1 < depth:
            x, xn = _ffn(x, xn, *ffn2, l, next_gain=(ffn1_n, l + 1))
        else:
            x = _ffn(x, xn, *ffn2, l)
    fin = _norm_w(final_norm.reshape(1, d))
    y_p = _rmsnorm(x, fin, 0, F32, 0, n_p * seq).reshape(n_p, seq, d)
    y_s = _rmsnorm(x, fin, 0, F32, n_p * seq, n_s * seq).reshape(n_s, seq, d)
    return (y_p, y_s)
```

```python
import functools

import jax
import jax.numpy as jnp
from jax import lax
from jax.experimental import pallas as pl
from jax.experimental.pallas import tpu as pltpu

F32 = jnp.float32
BF16 = jnp.bfloat16

EPS = 1e-6
NEG_BIG = -1e30
HEADS = 4
GLA_RANK = 16
GLA_TAU = 16.0
GLA_MIN_LOG_DECAY = -1.0
GLA_CHUNK = 64
MLSTM_BWD_CHUNK = 512
MLSTM_FWD_CHUNK = 256
MLSTM_FWD_HEADS_PER_STEP = 2
XATTN_HEADS = 4
LANE = 128
SUBLANE = 8
VMEM_LIMIT = 56 * 1024 * 1024

NT_DIMS = (((1,), (1,)), ((), ()))
TN_DIMS = (((0,), (0,)), ((), ()))


def _pick(n, target, mult):
    if n <= target:
        return n
    t = (target // mult) * mult
    while t >= mult:
        if n % t == 0:
            return t
        t -= mult
    return n


def _params(sem):
    return pltpu.CompilerParams(dimension_semantics=sem, vmem_limit_bytes=VMEM_LIMIT)


def _log_sigmoid(x):
    return jnp.minimum(x, 0.0) - jnp.log(1.0 + jnp.exp(-jnp.abs(x)))


def _sigmoid(x):
    return 0.5 * jnp.tanh(0.5 * x) + 0.5


def _split_bf16(x):
    hi = x.astype(BF16)
    return hi, (x - hi.astype(F32)).astype(BF16)


def _dot(a, b):
    return jnp.dot(a, b, preferred_element_type=F32)


def _rmsnorm_body(x_ref, w_ref, o_ref):
    x = x_ref[...]
    ms = jnp.mean(x * x, axis=-1, keepdims=True)
    o_ref[...] = (x * lax.rsqrt(ms + EPS) * w_ref[...]).astype(o_ref.dtype)


def _pair_specs(block, na, col=lambda *g: 0):
    return [pl.BlockSpec(block, lambda *g: (jnp.minimum(g[0], na - 1), col(*g))),
            pl.BlockSpec(block, lambda *g: (jnp.maximum(g[0] - na, 0), col(*g)))]


def _rmsnorm_pair_body(xa_ref, xb_ref, w_ref, o_ref, *, na):
    @pl.when(pl.program_id(0) < na)
    def _():
        _rmsnorm_body(xa_ref, w_ref, o_ref)

    @pl.when(pl.program_id(0) >= na)
    def _():
        _rmsnorm_body(xb_ref, w_ref, o_ref)


def _rmsnorm_pair(xa, xb, w, l, out_dtype):
    d = xa.shape[1]
    tr = _pick(xa.shape[0], 256, SUBLANE * 2)
    assert xa.shape[0] % tr == 0 and xb.shape[0] % tr == 0
    na, nb = xa.shape[0] // tr, xb.shape[0] // tr
    return pl.pallas_call(
        functools.partial(_rmsnorm_pair_body, na=na),
        grid=(na + nb,),
        in_specs=_pair_specs((tr, d), na) + [pl.BlockSpec((None, 1, d), lambda i: (l, 0, 0))],
        out_specs=pl.BlockSpec((tr, d), lambda i: (i, 0)),
        out_shape=jax.ShapeDtypeStruct(((na + nb) * tr, d), out_dtype),
        compiler_params=_params(("parallel",)),
        name="rmsnorm",
    )(xa, xb, w)


def _rmsnorm(x, w, l, out_dtype, row_start=0, rows=None):
    m, d = x.shape
    rows = m if rows is None else rows
    tr = _pick(rows, 256, SUBLANE * 2)
    assert row_start % tr == 0
    off = row_start // tr
    return pl.pallas_call(
        _rmsnorm_body,
        grid=(rows // tr,),
        in_specs=[pl.BlockSpec((tr, d), lambda i: (i + off, 0)),
                  pl.BlockSpec((None, 1, d), lambda i: (l, 0, 0))],
        out_specs=pl.BlockSpec((tr, d), lambda i: (i, 0)),
        out_shape=jax.ShapeDtypeStruct((rows, d), out_dtype),
        compiler_params=_params(("parallel",)),
        name="rmsnorm",
    )(x, w)


def _row_scale(rs_ref, width):
    return jnp.concatenate([rs_ref[...]] * (width // LANE), axis=1)


def _split_act(a):
    return a if isinstance(a, tuple) else (a, None)


def _ss_spec(tm, ss):
    return [] if ss is None else [pl.BlockSpec((tm, LANE), lambda i, *_: (i, 0))]


def _mm_body(a_ref, b_ref, *rest):
    *ss_ref, o_ref = rest
    p = _dot(a_ref[...], b_ref[...])
    if ss_ref:
        p = p * _row_scale(ss_ref[0], p.shape[1])
    o_ref[...] = p.astype(o_ref.dtype)


def _matmul(a, w, l, out_dtype, tm=1024, tn=1024, name="matmul"):
    a, ss = _split_act(a)
    m, k = a.shape
    n = w.shape[2]
    tm = _pick(m, tm, 16)
    tn = _pick(n, tn, LANE)
    return pl.pallas_call(
        _mm_body,
        grid=(m // tm, n // tn),
        in_specs=[pl.BlockSpec((tm, k), lambda i, j: (i, 0)),
                  pl.BlockSpec((None, k, tn), lambda i, j: (l, 0, j))] + _ss_spec(tm, ss),
        out_specs=pl.BlockSpec((tm, tn), lambda i, j: (i, j)),
        out_shape=jax.ShapeDtypeStruct((m, n), out_dtype),
        compiler_params=_params(("parallel", "arbitrary")),
        name=name,
    )(a, w, *([] if ss is None else [ss]))


SWIGLU_ROW_TILES = 4


def _swiglu_body(a_ref, wg_ref, wu_ref, *rest):
    *ss_ref, o_ref = rest
    rows = a_ref.shape[0] // SWIGLU_ROW_TILES
    for t in range(SWIGLU_ROW_TILES):
        r = slice(t * rows, (t + 1) * rows)
        a = a_ref[r, :]
        g = _dot(a, wg_ref[...])
        u = _dot(a, wu_ref[...])
        if ss_ref:
            rs = jnp.concatenate([ss_ref[0][r, :]] * (g.shape[1] // LANE), axis=1)
            g, u = g * rs, u * rs
        o_ref[r, :] = (g * _sigmoid(g) * u).astype(o_ref.dtype)


def _swiglu_up(a, wg, wu, l, tm=2048, tn=256):
    a, ss = _split_act(a)
    m, k = a.shape
    n = wg.shape[2]
    tm = _pick(m, tm, 16 * SWIGLU_ROW_TILES)
    tn = min(tn, n)
    return pl.pallas_call(
        _swiglu_body,
        grid=(m // tm, pl.cdiv(n, tn)),
        in_specs=[pl.BlockSpec((tm, k), lambda i, j: (i, 0)),
                  pl.BlockSpec((None, k, tn), lambda i, j: (l, 0, j)),
                  pl.BlockSpec((None, k, tn), lambda i, j: (l, 0, j))] + _ss_spec(tm, ss),
        out_specs=pl.BlockSpec((tm, tn), lambda i, j: (i, j)),
        out_shape=jax.ShapeDtypeStruct((m, n), BF16),
        compiler_params=_params(("parallel", "arbitrary")),
        name="swiglu_up",
    )(a, wg, wu, *([] if ss is None else [ss]))


def _mm_res_body(a_ref, b_ref, *rest, scale, na, fuse, width):
    n_res = 1 if na is None else 2
    res_refs, rest = rest[:n_res], rest[n_res:]
    if na is None:
        r = res_refs[0][...]
    else:
        r = jnp.where(pl.program_id(0) < na, res_refs[0][...], res_refs[1][...])
    y = r + scale * _dot(a_ref[...], b_ref[...])
    if not fuse:
        rest[0][...] = y
        return
    gain_ref, o_ref, yw_ref, ss_ref = rest
    o_ref[...] = y
    yw_ref[...] = (y * gain_ref[...]).astype(yw_ref.dtype)
    sq = y * y
    part = sq[:, 0:LANE]
    for t in range(1, sq.shape[1] // LANE):
        part = part + sq[:, t * LANE:(t + 1) * LANE]

    @pl.when(pl.program_id(1) == 0)
    def _():
        ss_ref[...] = part

    @pl.when(pl.program_id(1) > 0)
    def _():
        ss_ref[...] += part

    @pl.when(pl.program_id(1) == pl.num_programs(1) - 1)
    def _():
        tot = jnp.sum(ss_ref[...], axis=1, keepdims=True)
        ss_ref[...] = jnp.broadcast_to(lax.rsqrt(tot * (1.0 / width) + EPS), ss_ref.shape)


def _matmul_residual(a, w, l, res, scale, tm=512, tn=512, next_gain=None, name="matmul_res"):
    m, k = a.shape
    n = w.shape[2]
    tm = _pick(m, tm, 16)
    tn = _pick(n, tn, LANE)
    if isinstance(res, tuple):
        assert res[0].shape[0] % tm == 0 and res[1].shape[0] % tm == 0
        na = res[0].shape[0] // tm
        res_specs = _pair_specs((tm, tn), na, col=lambda i, j: j)
    else:
        na, res = None, (res,)
        res_specs = [pl.BlockSpec((tm, tn), lambda i, j: (i, j))]
    in_specs = [pl.BlockSpec((tm, k), lambda i, j: (i, 0)),
                pl.BlockSpec((None, k, tn), lambda i, j: (l, 0, j))] + res_specs
    out_specs = pl.BlockSpec((tm, tn), lambda i, j: (i, j))
    out_shape = jax.ShapeDtypeStruct((m, n), F32)
    args = [a, w, *res]
    fuse = next_gain is not None
    if fuse:
        gains, gl = next_gain
        in_specs.append(pl.BlockSpec((None, 1, tn), lambda i, j: (gl, 0, j)))
        args.append(gains)
        out_specs = [out_specs, pl.BlockSpec((tm, tn), lambda i, j: (i, j)),
                     pl.BlockSpec((tm, LANE), lambda i, j: (i, 0))]
        out_shape = [out_shape, jax.ShapeDtypeStruct((m, n), BF16), jax.ShapeDtypeStruct((m, LANE), F32)]
    out = pl.pallas_call(
        functools.partial(_mm_res_body, scale=scale, na=na, fuse=fuse, width=n),
        grid=(m // tm, n // tn),
        in_specs=in_specs,
        out_specs=out_specs,
        out_shape=out_shape,
        compiler_params=_params(("parallel", "arbitrary")),
        name=name,
    )(*args)
    return (out[0], (out[1], out[2])) if fuse else out


def _mm2_res_body(a1_ref, a2_ref, b1_ref, b2_ref, r_ref, o_ref):
    o_ref[...] = r_ref[...] + (_dot(a1_ref[...], b1_ref[...]) + _dot(a2_ref[...], b2_ref[...]))


def _out_proj(a1, a2, w, l, res, tm=1024, tn=512):
    m, k1 = a1.shape
    k2 = a2.shape[1]
    assert k1 == k2 and w.shape[1] == k1 + k2
    n = w.shape[2]
    tm = _pick(m, tm, 16)
    tn = _pick(n, tn, LANE)
    return pl.pallas_call(
        _mm2_res_body,
        grid=(m // tm, n // tn),
        in_specs=[pl.BlockSpec((tm, k1), lambda i, j: (i, 0)),
                  pl.BlockSpec((tm, k2), lambda i, j: (i, 0)),
                  pl.BlockSpec((None, k1, tn), lambda i, j: (l, 0, j)),
                  pl.BlockSpec((None, k2, tn), lambda i, j: (l, 1, j)),
                  pl.BlockSpec((tm, tn), lambda i, j: (i, j))],
        out_specs=pl.BlockSpec((tm, tn), lambda i, j: (i, j)),
        out_shape=jax.ShapeDtypeStruct((m, n), F32),
        compiler_params=_params(("parallel", "arbitrary")),
        name="mixer_out_proj",
    )(a1, a2, w, w, res)


def _gate_proj_body(a_ref, w_ref, wt_ref, *rest):
    *ss_ref, o_ref, ot_ref = rest
    a = a_ref[...]
    o = _dot(a, w_ref[...])
    ot = lax.dot_general(wt_ref[...], a, NT_DIMS, preferred_element_type=F32)
    if ss_ref:
        o = o * _row_scale(ss_ref[0], o.shape[1])
        ot = ot * jnp.transpose(ss_ref[0][...])[0:1, :]
    o_ref[...] = o
    ot_ref[...] = ot


def _gate_proj(a, w, wt, l, tm=1024):
    a, ss = _split_act(a)
    m, k = a.shape
    n = w.shape[2]
    nt = wt.shape[1]
    tm = _pick(m, tm, LANE)
    return pl.pallas_call(
        _gate_proj_body,
        grid=(m // tm,),
        in_specs=[pl.BlockSpec((tm, k), lambda i: (i, 0)),
                  pl.BlockSpec((None, k, n), lambda i: (l, 0, 0)),
                  pl.BlockSpec((None, nt, k), lambda i: (l, 0, 0))] + _ss_spec(tm, ss),
        out_specs=[pl.BlockSpec((tm, n), lambda i: (i, 0)),
                   pl.BlockSpec((nt, tm), lambda i: (0, i))],
        out_shape=[jax.ShapeDtypeStruct((m, n), F32),
                   jax.ShapeDtypeStruct((nt, m), F32)],
        compiler_params=_params(("parallel",)),
        name="gate_proj",
    )(a, w, wt, *([] if ss is None else [ss]))


def _causal_mask(chunk, reverse):
    r = lax.broadcasted_iota(jnp.int32, (chunk, chunk), 0)
    c = lax.broadcasted_iota(jnp.int32, (chunk, chunk), 1)
    return (c >= r) if reverse else (c <= r)


def _head_norm_gate(h, other, gate, nw):
    h = h + other
    return h * lax.rsqrt(jnp.mean(h * h, axis=-1, keepdims=True) + EPS) * nw * gate


def _gla_body(q_ref, k_ref, v_ref, lr_ref, wlr_ref, blr_ref, *rest, reverse, final, lb, hps):
    if final:
        gg_ref, ob_ref, nw_ref, o_ref, s_ref = rest
    else:
        o_ref, s_ref = rest
    dk = q_ref.shape[-1] // hps
    dv = v_ref.shape[-1] // hps
    chunk = GLA_CHUNK
    nch = lb // chunk
    qk = hps * dk

    @pl.when(pl.program_id(2) == 0)
    def _():
        s_ref[...] = jnp.zeros_like(s_ref)

    lh, ll = _split_bf16(lr_ref[...])
    wh, wl = _split_bf16(wlr_ref[...])
    z = _dot(lh, wh) + _dot(lh, wl) + _dot(ll, wh) + blr_ref[...]
    g = jnp.maximum(_log_sigmoid(z) * (1.0 / GLA_TAU), GLA_MIN_LOG_DECAY)
    mask = _causal_mask(chunk, reverse)
    tri = jnp.where(mask, 1.0, 0.0).astype(BF16)
    gh, gl = _split_bf16(jnp.concatenate([g[c * chunk:(c + 1) * chunk, :] for c in range(nch)], axis=1))
    bcum = _dot(tri, gh) + _dot(tri, gl)

    scale = dk ** -0.5
    zero_blk = jnp.zeros((chunk, chunk), BF16)
    npair = nch // 2
    for pb in (range(npair - 1, -1, -1) if reverse else range(npair)):
        rows = slice(2 * pb * chunk, (2 * pb + 2) * chunk)
        c_first, c_second = (2 * pb + 1, 2 * pb) if reverse else (2 * pb, 2 * pb + 1)
        for hh in range(hps):
            kc = slice(hh * dk, (hh + 1) * dk)
            vc = slice(hh * dv, (hh + 1) * dv)

            def chunk_terms(c):
                b = bcum[:, c * qk + hh * dk:c * qk + (hh + 1) * dk]
                tot = b[0:1, :] if reverse else b[chunk - 1:chunk, :]
                r = slice(c * chunk, (c + 1) * chunk)
                q_dec = q_ref[r, kc].astype(F32) * (jnp.exp(b) * scale)
                k_inv = k_ref[r, kc].astype(F32) * jnp.exp(-b)
                return tot, q_dec, k_inv, k_inv * jnp.exp(tot)

            tot1, qd1, ki1, ke1 = chunk_terms(c_first)
            tot2, qd2, ki2, ke2 = chunk_terms(c_second)
            qd1b, qd2b = qd1.astype(BF16), qd2.astype(BF16)
            a11 = lax.dot_general(qd1b, ki1.astype(BF16), NT_DIMS, preferred_element_type=F32)
            a22 = lax.dot_general(qd2b, ki2.astype(BF16), NT_DIMS, preferred_element_type=F32)
            a21 = lax.dot_general(qd2b, ke1.astype(BF16), NT_DIMS, preferred_element_type=F32)
            a11 = jnp.where(mask, a11, 0.0).astype(BF16)
            a22 = jnp.where(mask, a22, 0.0).astype(BF16)
            a21 = a21.astype(BF16)
            qs2 = (qd2 * jnp.exp(tot1)).astype(BF16)
            kx1 = (ke1 * jnp.exp(tot2)).astype(BF16)
            ke2b = ke2.astype(BF16)
            if reverse:
                amat = jnp.concatenate([jnp.concatenate([a22, a21], axis=1),
                                        jnp.concatenate([zero_blk, a11], axis=1)], axis=0)
                qmat = jnp.concatenate([qs2, qd1b], axis=0)
                kmat = jnp.concatenate([ke2b, kx1], axis=0)
            else:
                amat = jnp.concatenate([jnp.concatenate([a11, zero_blk], axis=1),
                                        jnp.concatenate([a21, a22], axis=1)], axis=0)
                qmat = jnp.concatenate([qd1b, qs2], axis=0)
                kmat = jnp.concatenate([kx1, ke2b], axis=0)
            v = v_ref[rows, vc]
            s = s_ref[hh]
            o = _dot(amat, v) + _dot(qmat, s.astype(BF16))
            dec = jnp.exp(jnp.transpose(jnp.broadcast_to(tot1 + tot2, (LANE, dk))))
            dec = jnp.concatenate([dec] * (dv // LANE), axis=1)
            s_ref[hh] = s * dec + lax.dot_general(kmat, v, TN_DIMS, preferred_element_type=F32)
            if final:
                gg = gg_ref[rows, vc].astype(F32)
                o = _head_norm_gate(o, ob_ref[rows, vc], gg * _sigmoid(gg), nw_ref[:, vc])
            o_ref[rows, vc] = o.astype(o_ref.dtype)


def _gla_scan(proj, lr, wlr, blr, l, *, reverse, dk, dv, col_q, col_k, col_v, col_gate=None,
              other=None, norm_w=None):
    bsz, seq, _ = proj.shape
    lb = _pick(seq, 512, LANE)
    assert lb % (2 * GLA_CHUNK) == 0
    hps = HEADS
    nblk = seq // lb
    final = other is not None
    qk, width = hps * dk, hps * dv

    def tok(i):
        return (nblk - 1 - i) if reverse else i

    def col_spec(w, col0):
        return pl.BlockSpec((None, lb, w), lambda b, h, i: (b, tok(i), col0 // w + h))

    d = 1 if reverse else 0
    in_specs = [col_spec(qk, col_q), col_spec(qk, col_k), col_spec(width, col_v),
                pl.BlockSpec((None, lb, LANE), lambda b, h, i: (b, tok(i), 0)),
                pl.BlockSpec((None, None, LANE, qk), lambda b, h, i: (l, d, 0, h)),
                pl.BlockSpec((None, None, 1, qk), lambda b, h, i: (l, d, 0, h))]
    args = [proj, proj, proj, lr, wlr, blr]
    if final:
        in_specs += [col_spec(width, col_gate),
                     pl.BlockSpec((None, lb, width), lambda b, h, i: (b, tok(i), h)),
                     pl.BlockSpec((None, 1, width), lambda b, h, i: (l, 0, h))]
        args += [proj, other, norm_w]
    return pl.pallas_call(
        functools.partial(_gla_body, reverse=reverse, final=final, lb=lb, hps=hps),
        grid=(bsz, HEADS // hps, nblk),
        in_specs=in_specs,
        out_specs=pl.BlockSpec((None, lb, width), lambda b, h, i: (b, tok(i), h)),
        out_shape=jax.ShapeDtypeStruct((bsz, seq, HEADS * dv), BF16 if final else F32),
        scratch_shapes=[pltpu.VMEM((hps, dk, dv), F32)],
        compiler_params=_params(("parallel", "parallel", "arbitrary")),
        name="gla_fwd" if final else "gla_bwd",
    )(*args)


def _segment_cumsum_lanes(x, seg, reverse):
    n = x.shape[-1]
    pos = lax.broadcasted_iota(jnp.int32, x.shape, x.ndim - 1) % seg
    s = 1
    while s < seg:
        if reverse:
            x = x + jnp.where(pos < seg - s, pltpu.roll(x, n - s, x.ndim - 1), 0.0)
        else:
            x = x + jnp.where(pos >= s, pltpu.roll(x, s, x.ndim - 1), 0.0)
        s *= 2
    return x


def _mlstm_body(q_ref, k_ref, v_ref, gr_ref, br_ref, *rest, reverse, final, lb, hps):
    if final:
        mo_ref, hb_ref, nw_ref, o_ref, c_ref, m_ref = rest
    else:
        o_ref, c_ref, m_ref = rest
    dk = q_ref.shape[-1] // hps
    dv = v_ref.shape[-1] // hps
    chunk = min(MLSTM_FWD_CHUNK if final else MLSTM_BWD_CHUNK, lb)

    @pl.when(pl.program_id(2) == 0)
    def _():
        c_ref[...] = jnp.zeros_like(c_ref)
        m_ref[...] = jnp.full_like(m_ref, NEG_BIG)

    ti, tf = (2, 3) if reverse else (0, 1)
    gr = gr_ref[...] + br_ref[:, 0:1]
    bcum = _segment_cumsum_lanes(_log_sigmoid(gr), chunk, reverse)
    row_id = lax.broadcasted_iota(jnp.int32, gr.shape, 0) % SUBLANE
    rowform = jnp.where(row_id == tf, bcum, gr)
    pad_rows = jnp.zeros((LANE - SUBLANE, chunk), F32)

    mask = _causal_mask(chunk, reverse)
    ones_col = jnp.where(lax.broadcasted_iota(jnp.int32, (chunk, LANE), 1) == 0, 1.0, 0.0).astype(BF16)
    scale = dk ** -0.5
    nch = lb // chunk
    for ch in (range(nch - 1, -1, -1) if reverse else range(nch)):
        rows = slice(ch * chunk, (ch + 1) * chunk)
        for hh in range(hps):
            kc = slice(hh * dk, (hh + 1) * dk)
            vc = slice(hh * dv, (hh + 1) * dv)
            rf = rowform[hh * SUBLANE:(hh + 1) * SUBLANE, rows]
            u_row = rf[ti:ti + 1, :] - rf[tf:tf + 1, :]
            colform = jnp.transpose(jnp.concatenate([rf, pad_rows], axis=0))
            bc = colform[:, tf:tf + 1]
            ic = colform[:, ti:ti + 1]
            gtot = bc[0:1, :] if reverse else bc[chunk - 1:chunk, :]
            log_d = jnp.where(mask, bc + u_row, NEG_BIG)
            m_intra = jnp.max(log_d, axis=1, keepdims=True)
            q = q_ref[rows, kc]
            k = k_ref[rows, kc]
            v_aug = jnp.concatenate([v_ref[rows, vc], ones_col], axis=1)
            s = lax.dot_general(q, k, NT_DIMS, preferred_element_type=F32) * jnp.exp(log_d - m_intra)
            intra = _dot(s.astype(BF16), v_aug)
            c = c_ref[hh]
            inter = _dot(q, c.astype(BF16))
            m_prev = m_ref[hh, 0:1, 0:1]
            bm = bc + m_prev
            m_j = jnp.maximum(bm, m_intra)
            comb = (scale * jnp.exp(bm - m_j)) * inter + (scale * jnp.exp(m_intra - m_j)) * intra
            denom = jnp.maximum(jnp.abs(comb[:, dv:dv + 1]), jnp.exp(-m_j))
            h = comb[:, :dv] * (1.0 / denom)
            a_col = gtot - bc + ic
            m_new = jnp.maximum(gtot + m_prev, jnp.max(a_col, axis=0, keepdims=True))
            kw = (k.astype(F32) * jnp.exp(a_col - m_new)).astype(BF16)
            c_ref[hh] = (jnp.exp(gtot + m_prev - m_new) * c
                         + lax.dot_general(kw, v_aug, TN_DIMS, preferred_element_type=F32))
            m_ref[hh] = jnp.broadcast_to(m_new, (SUBLANE, LANE))
            if final:
                h = _head_norm_gate(h, hb_ref[rows, vc], _sigmoid(mo_ref[rows, vc].astype(F32)), nw_ref[:, vc])
            o_ref[rows, vc] = h.astype(o_ref.dtype)


def _mlstm_scan(proj, gates_t, bias_r, l, *, reverse, dk, dv, col_q, col_k, col_v,
                col_gate=None, other=None, norm_w=None, hps=HEADS):
    bsz, seq, _ = proj.shape
    lb = _pick(seq, 512, LANE)
    nblk = seq // lb
    final = other is not None
    assert HEADS % hps == 0
    qk, width = hps * dk, hps * dv

    def tok(i):
        return (nblk - 1 - i) if reverse else i

    def col_spec(w, col0):
        return pl.BlockSpec((None, lb, w), lambda b, h, i: (b, tok(i), col0 // w + h))

    in_specs = [col_spec(qk, col_q), col_spec(qk, col_k), col_spec(width, col_v),
                pl.BlockSpec((hps * SUBLANE, lb), lambda b, h, i: (h, b * nblk + tok(i))),
                pl.BlockSpec((None, hps * SUBLANE, LANE), lambda b, h, i: (l, h, 0))]
    args = [proj, proj, proj, gates_t, bias_r]
    if final:
        in_specs += [col_spec(width, col_gate),
                     pl.BlockSpec((None, lb, width), lambda b, h, i: (b, tok(i), h)),
                     pl.BlockSpec((None, 1, width), lambda b, h, i: (l, 0, h))]
        args += [proj, other, norm_w]
    return pl.pallas_call(
        functools.partial(_mlstm_body, reverse=reverse, final=final, lb=lb, hps=hps),
        grid=(bsz, HEADS // hps, nblk),
        in_specs=in_specs,
        out_specs=pl.BlockSpec((None, lb, width), lambda b, h, i: (b, tok(i), h)),
        out_shape=jax.ShapeDtypeStruct((bsz, seq, HEADS * dv), BF16 if final else F32),
        scratch_shapes=[pltpu.VMEM((hps, dk, dv + LANE), F32), pltpu.VMEM((hps, SUBLANE, LANE), F32)],
        compiler_params=_params(("parallel", "parallel", "arbitrary")),
        name="mlstm_fwd" if final else "mlstm_bwd",
    )(*args)


def _xattn_body(x_ref, nw_ref, wq_ref, kv_ref, wo_ref, nw2_ref, o_ref, n_ref, *, heads):
    x = x_ref[...]
    xn = (x * lax.rsqrt(jnp.mean(x * x, axis=-1, keepdims=True) + EPS) * nw_ref[...]).astype(BF16)
    width = wq_ref.shape[1]
    hd = width // heads
    q = (_dot(xn, wq_ref[...]) * (hd ** -0.5)).astype(BF16)
    outs = []
    for h in range(heads):
        kh = kv_ref[:, h * hd:(h + 1) * hd]
        vh = kv_ref[:, width + h * hd:width + (h + 1) * hd]
        s = lax.dot_general(q[:, h * hd:(h + 1) * hd], kh, NT_DIMS, preferred_element_type=F32)
        p = jnp.exp(s - jnp.max(s, axis=-1, keepdims=True))
        p = p * (1.0 / jnp.sum(p, axis=-1, keepdims=True))
        outs.append(_dot(p.astype(BF16), vh))
    o = jnp.concatenate(outs, axis=1).astype(BF16)
    y = x + _dot(o, wo_ref[...])
    o_ref[...] = y
    n_ref[...] = (y * lax.rsqrt(jnp.mean(y * y, axis=-1, keepdims=True) + EPS) * nw2_ref[...]).astype(BF16)


def _xattn(x, norm_w, wq, kv, wo, next_norm_w, l, tq=512):
    bsz, seq, d = x.shape
    mem = kv.shape[1]
    width = wq.shape[2]
    tq = _pick(seq, tq, 16)
    return pl.pallas_call(
        functools.partial(_xattn_body, heads=XATTN_HEADS),
        grid=(bsz, seq // tq),
        in_specs=[pl.BlockSpec((None, tq, d), lambda b, i: (b, i, 0)),
                  pl.BlockSpec((None, 1, d), lambda b, i: (l, 0, 0)),
                  pl.BlockSpec((None, d, width), lambda b, i: (l, 0, 0), pipeline_mode=pl.Buffered(1)),
                  pl.BlockSpec((None, mem, 2 * width), lambda b, i: (b, 0, 0)),
                  pl.BlockSpec((None, width, d), lambda b, i: (l, 0, 0), pipeline_mode=pl.Buffered(1)),
                  pl.BlockSpec((None, 1, d), lambda b, i: (l, 0, 0))],
        out_specs=[pl.BlockSpec((None, tq, d), lambda b, i: (b, i, 0)),
                   pl.BlockSpec((None, tq, d), lambda b, i: (b, i, 0))],
        out_shape=[jax.ShapeDtypeStruct((bsz, seq, d), F32),
                   jax.ShapeDtypeStruct((bsz, seq, d), BF16)],
        compiler_params=_params(("parallel", "arbitrary")),
        name="mem_xattn",
    )(x, norm_w, wq, kv, wo, next_norm_w)


def _norm_w(w):
    return w.astype(F32).reshape(w.shape[0], 1, w.shape[1])


def _ffn(x, xn, w_gate, w_up, w_down, l, next_gain=None):
    hid = _swiglu_up(xn, w_gate, w_up, l)
    return _matmul_residual(hid, w_down, l, x, 0.5, next_gain=next_gain, name="ffn_down")


def _mixer_weights(d, w_in, gla_w_lr, gla_b_lr, mlstm_gate_b):
    depth = w_in.shape[0]
    half = d // 2
    qk = half // 2
    o_lr = 2 * qk + 2 * half
    o_m = o_lr + 2 * GLA_RANK
    o_mg = o_m + 2 * qk + 2 * half
    w_gla = w_in[:, :, :o_lr].astype(BF16)
    w_ml = w_in[:, :, o_m:o_mg].astype(BF16)
    w_lr = jnp.pad(w_in[:, :, o_lr:o_m], ((0, 0), (0, 0), (0, LANE - 2 * GLA_RANK))).astype(BF16)
    mg = w_in[:, :, o_mg:o_mg + 4 * HEADS].reshape(depth, d, 4, HEADS).transpose(0, 3, 2, 1)
    w_gates_t = jnp.pad(mg, ((0, 0), (0, 0), (0, SUBLANE - 4), (0, 0))).reshape(depth, HEADS * SUBLANE, d)
    w_gates_t = w_gates_t.astype(BF16)
    bias_r = jnp.pad(mlstm_gate_b.astype(F32).transpose(0, 2, 1), ((0, 0), (0, 0), (0, SUBLANE - 4)))
    bias_r = jnp.broadcast_to(bias_r.reshape(depth, HEADS * SUBLANE, 1), (depth, HEADS * SUBLANE, LANE))
    wlr = jnp.stack([jnp.pad(gla_w_lr[:, 0], ((0, 0), (0, LANE - GLA_RANK), (0, 0))),
                     jnp.pad(gla_w_lr[:, 1], ((0, 0), (GLA_RANK, LANE - 2 * GLA_RANK), (0, 0)))],
                    axis=1).astype(F32)
    blr = gla_b_lr.astype(F32).reshape(depth, 2, 1, qk)
    return w_gla, w_ml, w_lr, w_gates_t, bias_r, wlr, blr


def _mixer(x, xn, bsz, seq, l, w_gla, w_ml, w_lr, w_gates_t, bias_r, wlr, blr, gla_norm, mlstm_norm, w_out):
    m, d = x.shape
    half = d // 2
    dv = half // HEADS
    dk = dv // 2
    qk = HEADS * dk
    proj_g = _matmul(xn, w_gla, l, BF16, name="gla_in_proj").reshape(bsz, seq, -1)
    proj_m = _matmul(xn, w_ml, l, BF16, name="mlstm_in_proj").reshape(bsz, seq, -1)
    lr, gates_t = _gate_proj(xn, w_lr, w_gates_t, l)
    lr = lr.reshape(bsz, seq, LANE)

    kw = dict(dk=dk, dv=dv, col_q=0, col_k=qk, col_v=2 * qk)
    g_b = _gla_scan(proj_g, lr, wlr, blr, l, reverse=True, **kw)
    g_out = _gla_scan(proj_g, lr, wlr, blr, l, reverse=False, col_gate=2 * qk + half, other=g_b,
                      norm_w=gla_norm, **kw)
    m_b = _mlstm_scan(proj_m, gates_t, bias_r, l, reverse=True, **kw)
    m_out = _mlstm_scan(proj_m, gates_t, bias_r, l, reverse=False, col_gate=2 * qk + half,
                        other=m_b, norm_w=mlstm_norm, hps=MLSTM_FWD_HEADS_PER_STEP, **kw)
    return _out_proj(g_out.reshape(m, half), m_out.reshape(m, half), w_out, l, x)


def kernel(x_prompt, x_sample, mem_prompt, mem_sample, ffn1_norm, ffn1_w_gate, ffn1_w_up, ffn1_w_down, mix_norm, w_in, gla_w_lr, gla_b_lr, gla_out_norm, mlstm_gate_b, mlstm_out_norm, w_out, xattn_norm, mem_norm, xattn_wq, xattn_wk, xattn_wv, xattn_wo, ffn2_norm, ffn2_w_gate, ffn2_w_up, ffn2_w_down, final_norm):
    assert x_prompt.shape[1:] == x_sample.shape[1:] and mem_prompt.shape[1:] == mem_sample.shape[1:]
    n_p, seq, d = x_prompt.shape
    n_s = x_sample.shape[0]
    bsz = n_p + n_s
    m = bsz * seq
    x = (x_prompt.reshape(n_p * seq, d), x_sample.reshape(n_s * seq, d))
    mem = jnp.concatenate([mem_prompt, mem_sample], axis=0)
    n_mem = mem.shape[1]
    mem = mem.reshape(bsz * n_mem, d)
    depth = ffn1_norm.shape[0]

    ffn1 = (ffn1_w_gate.astype(BF16), ffn1_w_up.astype(BF16), ffn1_w_down.astype(BF16))
    ffn2 = (ffn2_w_gate.astype(BF16), ffn2_w_up.astype(BF16), ffn2_w_down.astype(BF16))
    mixer_w = _mixer_weights(d, w_in, gla_w_lr, gla_b_lr, mlstm_gate_b)
    mixer_rest = (_norm_w(gla_out_norm), _norm_w(mlstm_out_norm), w_out.astype(BF16))
    w_kv = jnp.concatenate([xattn_wk, xattn_wv], axis=2).astype(BF16)
    wq = xattn_wq.astype(BF16)
    wo = xattn_wo.astype(BF16)
    ffn1_n, ffn2_n = _norm_w(ffn1_norm), _norm_w(ffn2_norm)
    mix_n, xattn_n, mem_n = _norm_w(mix_norm), _norm_w(xattn_norm), _norm_w(mem_norm)

    xn = _rmsnorm_pair(*x, ffn1_n, 0, BF16)
    for l in range(depth):
        x, xn = _ffn(x, xn, *ffn1, l, next_gain=(mix_n, l))
        x = _mixer(x, xn, bsz, seq, l, *mixer_w, *mixer_rest)
        kv = _matmul(_rmsnorm(mem, mem_n, l, BF16), w_kv, l, BF16, tm=512, name="mem_kv_proj")
        x, xn = _xattn(x.reshape(bsz, seq, d), xattn_n, wq, kv.reshape(bsz, n_mem, -1), wo, ffn2_n, l)
        x, xn = x.reshape(m, d), xn.reshape(m, d)
        if l + 1 < depth:
            x, xn = _ffn(x, xn, *ffn2, l, next_gain=(ffn1_n, l + 1))
        else:
            x = _ffn(x, xn, *ffn2, l)
    fin = _norm_w(final_norm.reshape(1, d))
    y_p = _rmsnorm(x, fin, 0, F32, 0, n_p * seq).reshape(n_p, seq, d)
    y_s = _rmsnorm(x, fin, 0, F32, n_p * seq, n_s * seq).reshape(n_s, seq, d)
    return (y_p, y_s)
```

```python
import functools

import jax
import jax.numpy as jnp
from jax import lax
from jax.experimental import pallas as pl
from jax.experimental.pallas import tpu as pltpu

F32 = jnp.float32
BF16 = jnp.bfloat16

EPS = 1e-6
NEG_BIG = -1e30
HEADS = 4
GLA_RANK = 16
GLA_TAU = 16.0
GLA_MIN_LOG_DECAY = -1.0
GLA_CHUNK = 64
MLSTM_BWD_CHUNK = 512
MLSTM_FWD_CHUNK = 512
MLSTM_FWD_HEADS_PER_STEP = 4
XATTN_HEADS = 4
LANE = 128
SUBLANE = 8
VMEM_LIMIT = 56 * 1024 * 1024

NT_DIMS = (((1,), (1,)), ((), ()))
TN_DIMS = (((0,), (0,)), ((), ()))


def _pick(n, target, mult):
    if n <= target:
        return n
    t = (target // mult) * mult
    while t >= mult:
        if n % t == 0:
            return t
        t -= mult
    return n


def _params(sem):
    return pltpu.CompilerParams(dimension_semantics=sem, vmem_limit_bytes=VMEM_LIMIT)


def _log_sigmoid(x):
    return jnp.minimum(x, 0.0) - jnp.log(1.0 + jnp.exp(-jnp.abs(x)))


def _sigmoid(x):
    return 0.5 * jnp.tanh(0.5 * x) + 0.5


def _split_bf16(x):
    hi = x.astype(BF16)
    return hi, (x - hi.astype(F32)).astype(BF16)


def _dot(a, b):
    return jnp.dot(a, b, preferred_element_type=F32)


def _rmsnorm_body(x_ref, w_ref, o_ref):
    x = x_ref[...]
    ms = jnp.mean(x * x, axis=-1, keepdims=True)
    o_ref[...] = (x * lax.rsqrt(ms + EPS) * w_ref[...]).astype(o_ref.dtype)


def _pair_specs(block, na, col=lambda *g: 0):
    return [pl.BlockSpec(block, lambda *g: (jnp.minimum(g[0], na - 1), col(*g))),
            pl.BlockSpec(block, lambda *g: (jnp.maximum(g[0] - na, 0), col(*g)))]


def _rmsnorm_pair_body(xa_ref, xb_ref, w_ref, o_ref, *, na):
    @pl.when(pl.program_id(0) < na)
    def _():
        _rmsnorm_body(xa_ref, w_ref, o_ref)

    @pl.when(pl.program_id(0) >= na)
    def _():
        _rmsnorm_body(xb_ref, w_ref, o_ref)


def _rmsnorm_pair(xa, xb, w, l, out_dtype):
    d = xa.shape[1]
    tr = _pick(xa.shape[0], 256, SUBLANE * 2)
    assert xa.shape[0] % tr == 0 and xb.shape[0] % tr == 0
    na, nb = xa.shape[0] // tr, xb.shape[0] // tr
    return pl.pallas_call(
        functools.partial(_rmsnorm_pair_body, na=na),
        grid=(na + nb,),
        in_specs=_pair_specs((tr, d), na) + [pl.BlockSpec((None, 1, d), lambda i: (l, 0, 0))],
        out_specs=pl.BlockSpec((tr, d), lambda i: (i, 0)),
        out_shape=jax.ShapeDtypeStruct(((na + nb) * tr, d), out_dtype),
        compiler_params=_params(("parallel",)),
        name="rmsnorm",
    )(xa, xb, w)


def _rmsnorm(x, w, l, out_dtype, row_start=0, rows=None):
    m, d = x.shape
    rows = m if rows is None else rows
    tr = _pick(rows, 256, SUBLANE * 2)
    assert row_start % tr == 0
    off = row_start // tr
    return pl.pallas_call(
        _rmsnorm_body,
        grid=(rows // tr,),
        in_specs=[pl.BlockSpec((tr, d), lambda i: (i + off, 0)),
                  pl.BlockSpec((None, 1, d), lambda i: (l, 0, 0))],
        out_specs=pl.BlockSpec((tr, d), lambda i: (i, 0)),
        out_shape=jax.ShapeDtypeStruct((rows, d), out_dtype),
        compiler_params=_params(("parallel",)),
        name="rmsnorm",
    )(x, w)


def _row_scale(rs_ref, width):
    return jnp.concatenate([rs_ref[...]] * (width // LANE), axis=1)


def _split_act(a):
    return a if isinstance(a, tuple) else (a, None)


def _ss_spec(tm, ss):
    return [] if ss is None else [pl.BlockSpec((tm, LANE), lambda i, *_: (i, 0))]


def _mm_body(a_ref, b_ref, *rest):
    *ss_ref, o_ref = rest
    p = _dot(a_ref[...], b_ref[...])
    if ss_ref:
        p = p * _row_scale(ss_ref[0], p.shape[1])
    o_ref[...] = p.astype(o_ref.dtype)


def _matmul(a, w, l, out_dtype, tm=1024, tn=1024, name="matmul"):
    a, ss = _split_act(a)
    m, k = a.shape
    n = w.shape[2]
    tm = _pick(m, tm, 16)
    tn = _pick(n, tn, LANE)
    return pl.pallas_call(
        _mm_body,
        grid=(m // tm, n // tn),
        in_specs=[pl.BlockSpec((tm, k), lambda i, j: (i, 0)),
                  pl.BlockSpec((None, k, tn), lambda i, j: (l, 0, j))] + _ss_spec(tm, ss),
        out_specs=pl.BlockSpec((tm, tn), lambda i, j: (i, j)),
        out_shape=jax.ShapeDtypeStruct((m, n), out_dtype),
        compiler_params=_params(("parallel", "arbitrary")),
        name=name,
    )(a, w, *([] if ss is None else [ss]))


SWIGLU_ROW_TILES = 4


def _swiglu_body(a_ref, wg_ref, wu_ref, *rest):
    *ss_ref, o_ref = rest
    rows = a_ref.shape[0] // SWIGLU_ROW_TILES
    for t in range(SWIGLU_ROW_TILES):
        r = slice(t * rows, (t + 1) * rows)
        a = a_ref[r, :]
        g = _dot(a, wg_ref[...])
        u = _dot(a, wu_ref[...])
        if ss_ref:
            rs = jnp.concatenate([ss_ref[0][r, :]] * (g.shape[1] // LANE), axis=1)
            g, u = g * rs, u * rs
        o_ref[r, :] = (g * _sigmoid(g) * u).astype(o_ref.dtype)


def _swiglu_up(a, wg, wu, l, tm=2048, tn=256):
    a, ss = _split_act(a)
    m, k = a.shape
    n = wg.shape[2]
    tm = _pick(m, tm, 16 * SWIGLU_ROW_TILES)
    tn = min(tn, n)
    return pl.pallas_call(
        _swiglu_body,
        grid=(m // tm, pl.cdiv(n, tn)),
        in_specs=[pl.BlockSpec((tm, k), lambda i, j: (i, 0)),
                  pl.BlockSpec((None, k, tn), lambda i, j: (l, 0, j)),
                  pl.BlockSpec((None, k, tn), lambda i, j: (l, 0, j))] + _ss_spec(tm, ss),
        out_specs=pl.BlockSpec((tm, tn), lambda i, j: (i, j)),
        out_shape=jax.ShapeDtypeStruct((m, n), BF16),
        compiler_params=_params(("parallel", "arbitrary")),
        name="swiglu_up",
    )(a, wg, wu, *([] if ss is None else [ss]))


def _mm_res_body(a_ref, b_ref, *rest, scale, na, fuse, width):
    n_res = 1 if na is None else 2
    res_refs, rest = rest[:n_res], rest[n_res:]
    if na is None:
        r = res_refs[0][...]
    else:
        r = jnp.where(pl.program_id(0) < na, res_refs[0][...], res_refs[1][...])
    y = r + scale * _dot(a_ref[...], b_ref[...])
    if not fuse:
        rest[0][...] = y
        return
    gain_ref, o_ref, yw_ref, ss_ref = rest
    o_ref[...] = y
    yw_ref[...] = (y * gain_ref[...]).astype(yw_ref.dtype)
    sq = y * y
    part = sq[:, 0:LANE]
    for t in range(1, sq.shape[1] // LANE):
        part = part + sq[:, t * LANE:(t + 1) * LANE]

    @pl.when(pl.program_id(1) == 0)
    def _():
        ss_ref[...] = part

    @pl.when(pl.program_id(1) > 0)
    def _():
        ss_ref[...] += part

    @pl.when(pl.program_id(1) == pl.num_programs(1) - 1)
    def _():
        tot = jnp.sum(ss_ref[...], axis=1, keepdims=True)
        ss_ref[...] = jnp.broadcast_to(lax.rsqrt(tot * (1.0 / width) + EPS), ss_ref.shape)


def _matmul_residual(a, w, l, res, scale, tm=512, tn=512, next_gain=None, name="matmul_res"):
    m, k = a.shape
    n = w.shape[2]
    tm = _pick(m, tm, 16)
    tn = _pick(n, tn, LANE)
    if isinstance(res, tuple):
        assert res[0].shape[0] % tm == 0 and res[1].shape[0] % tm == 0
        na = res[0].shape[0] // tm
        res_specs = _pair_specs((tm, tn), na, col=lambda i, j: j)
    else:
        na, res = None, (res,)
        res_specs = [pl.BlockSpec((tm, tn), lambda i, j: (i, j))]
    in_specs = [pl.BlockSpec((tm, k), lambda i, j: (i, 0)),
                pl.BlockSpec((None, k, tn), lambda i, j: (l, 0, j))] + res_specs
    out_specs = pl.BlockSpec((tm, tn), lambda i, j: (i, j))
    out_shape = jax.ShapeDtypeStruct((m, n), F32)
    args = [a, w, *res]
    fuse = next_gain is not None
    if fuse:
        gains, gl = next_gain
        in_specs.append(pl.BlockSpec((None, 1, tn), lambda i, j: (gl, 0, j)))
        args.append(gains)
        out_specs = [out_specs, pl.BlockSpec((tm, tn), lambda i, j: (i, j)),
                     pl.BlockSpec((tm, LANE), lambda i, j: (i, 0))]
        out_shape = [out_shape, jax.ShapeDtypeStruct((m, n), BF16), jax.ShapeDtypeStruct((m, LANE), F32)]
    out = pl.pallas_call(
        functools.partial(_mm_res_body, scale=scale, na=na, fuse=fuse, width=n),
        grid=(m // tm, n // tn),
        in_specs=in_specs,
        out_specs=out_specs,
        out_shape=out_shape,
        compiler_params=_params(("parallel", "arbitrary")),
        name=name,
    )(*args)
    return (out[0], (out[1], out[2])) if fuse else out


def _mm2_res_body(a1_ref, a2_ref, b1_ref, b2_ref, r_ref, o_ref):
    o_ref[...] = r_ref[...] + (_dot(a1_ref[...], b1_ref[...]) + _dot(a2_ref[...], b2_ref[...]))


def _out_proj(a1, a2, w, l, res, tm=1024, tn=512):
    m, k1 = a1.shape
    k2 = a2.shape[1]
    assert k1 == k2 and w.shape[1] == k1 + k2
    n = w.shape[2]
    tm = _pick(m, tm, 16)
    tn = _pick(n, tn, LANE)
    return pl.pallas_call(
        _mm2_res_body,
        grid=(m // tm, n // tn),
        in_specs=[pl.BlockSpec((tm, k1), lambda i, j: (i, 0)),
                  pl.BlockSpec((tm, k2), lambda i, j: (i, 0)),
                  pl.BlockSpec((None, k1, tn), lambda i, j: (l, 0, j)),
                  pl.BlockSpec((None, k2, tn), lambda i, j: (l, 1, j)),
                  pl.BlockSpec((tm, tn), lambda i, j: (i, j))],
        out_specs=pl.BlockSpec((tm, tn), lambda i, j: (i, j)),
        out_shape=jax.ShapeDtypeStruct((m, n), F32),
        compiler_params=_params(("parallel", "arbitrary")),
        name="mixer_out_proj",
    )(a1, a2, w, w, res)


def _gate_proj_body(a_ref, w_ref, wt_ref, *rest):
    *ss_ref, o_ref, ot_ref = rest
    a = a_ref[...]
    o = _dot(a, w_ref[...])
    ot = lax.dot_general(wt_ref[...], a, NT_DIMS, preferred_element_type=F32)
    if ss_ref:
        o = o * _row_scale(ss_ref[0], o.shape[1])
        ot = ot * jnp.transpose(ss_ref[0][...])[0:1, :]
    o_ref[...] = o
    ot_ref[...] = ot


def _gate_proj(a, w, wt, l, tm=1024):
    a, ss = _split_act(a)
    m, k = a.shape
    n = w.shape[2]
    nt = wt.shape[1]
    tm = _pick(m, tm, LANE)
    return pl.pallas_call(
        _gate_proj_body,
        grid=(m // tm,),
        in_specs=[pl.BlockSpec((tm, k), lambda i: (i, 0)),
                  pl.BlockSpec((None, k, n), lambda i: (l, 0, 0)),
                  pl.BlockSpec((None, nt, k), lambda i: (l, 0, 0))] + _ss_spec(tm, ss),
        out_specs=[pl.BlockSpec((tm, n), lambda i: (i, 0)),
                   pl.BlockSpec((nt, tm), lambda i: (0, i))],
        out_shape=[jax.ShapeDtypeStruct((m, n), F32),
                   jax.ShapeDtypeStruct((nt, m), F32)],
        compiler_params=_params(("parallel",)),
        name="gate_proj",
    )(a, w, wt, *([] if ss is None else [ss]))


def _causal_mask(chunk, reverse):
    r = lax.broadcasted_iota(jnp.int32, (chunk, chunk), 0)
    c = lax.broadcasted_iota(jnp.int32, (chunk, chunk), 1)
    return (c >= r) if reverse else (c <= r)


def _head_norm_gate(h, other, gate, nw):
    h = h + other
    return h * lax.rsqrt(jnp.mean(h * h, axis=-1, keepdims=True) + EPS) * nw * gate


def _gla_body(q_ref, k_ref, v_ref, lr_ref, wlr_ref, blr_ref, *rest, reverse, final, lb, hps):
    if final:
        gg_ref, ob_ref, nw_ref, o_ref, s_ref = rest
    else:
        o_ref, s_ref = rest
    dk = q_ref.shape[-1] // hps
    dv = v_ref.shape[-1] // hps
    chunk = GLA_CHUNK
    nch = lb // chunk
    qk = hps * dk

    @pl.when(pl.program_id(2) == 0)
    def _():
        s_ref[...] = jnp.zeros_like(s_ref)

    lh, ll = _split_bf16(lr_ref[...])
    wh, wl = _split_bf16(wlr_ref[...])
    z = _dot(lh, wh) + _dot(lh, wl) + _dot(ll, wh) + blr_ref[...]
    g = jnp.maximum(_log_sigmoid(z) * (1.0 / GLA_TAU), GLA_MIN_LOG_DECAY)
    mask = _causal_mask(chunk, reverse)
    tri = jnp.where(mask, 1.0, 0.0).astype(BF16)
    gh, gl = _split_bf16(jnp.concatenate([g[c * chunk:(c + 1) * chunk, :] for c in range(nch)], axis=1))
    bcum = _dot(tri, gh) + _dot(tri, gl)

    scale = dk ** -0.5
    zero_blk = jnp.zeros((chunk, chunk), BF16)
    npair = nch // 2
    for pb in (range(npair - 1, -1, -1) if reverse else range(npair)):
        rows = slice(2 * pb * chunk, (2 * pb + 2) * chunk)
        c_first, c_second = (2 * pb + 1, 2 * pb) if reverse else (2 * pb, 2 * pb + 1)
        for hh in range(hps):
            kc = slice(hh * dk, (hh + 1) * dk)
            vc = slice(hh * dv, (hh + 1) * dv)

            def chunk_terms(c):
                b = bcum[:, c * qk + hh * dk:c * qk + (hh + 1) * dk]
                tot = b[0:1, :] if reverse else b[chunk - 1:chunk, :]
                r = slice(c * chunk, (c + 1) * chunk)
                q_dec = q_ref[r, kc].astype(F32) * (jnp.exp(b) * scale)
                k_inv = k_ref[r, kc].astype(F32) * jnp.exp(-b)
                return tot, q_dec, k_inv, k_inv * jnp.exp(tot)

            tot1, qd1, ki1, ke1 = chunk_terms(c_first)
            tot2, qd2, ki2, ke2 = chunk_terms(c_second)
            qd1b, qd2b = qd1.astype(BF16), qd2.astype(BF16)
            a11 = lax.dot_general(qd1b, ki1.astype(BF16), NT_DIMS, preferred_element_type=F32)
            a22 = lax.dot_general(qd2b, ki2.astype(BF16), NT_DIMS, preferred_element_type=F32)
            a21 = lax.dot_general(qd2b, ke1.astype(BF16), NT_DIMS, preferred_element_type=F32)
            a11 = jnp.where(mask, a11, 0.0).astype(BF16)
            a22 = jnp.where(mask, a22, 0.0).astype(BF16)
            a21 = a21.astype(BF16)
            qs2 = (qd2 * jnp.exp(tot1)).astype(BF16)
            kx1 = (ke1 * jnp.exp(tot2)).astype(BF16)
            ke2b = ke2.astype(BF16)
            if reverse:
                amat = jnp.concatenate([jnp.concatenate([a22, a21], axis=1),
                                        jnp.concatenate([zero_blk, a11], axis=1)], axis=0)
                qmat = jnp.concatenate([qs2, qd1b], axis=0)
                kmat = jnp.concatenate([ke2b, kx1], axis=0)
            else:
                amat = jnp.concatenate([jnp.concatenate([a11, zero_blk], axis=1),
                                        jnp.concatenate([a21, a22], axis=1)], axis=0)
                qmat = jnp.concatenate([qd1b, qs2], axis=0)
                kmat = jnp.concatenate([kx1, ke2b], axis=0)
            v = v_ref[rows, vc]
            s = s_ref[hh]
            o = _dot(amat, v) + _dot(qmat, s.astype(BF16))
            dec = jnp.exp(jnp.transpose(jnp.broadcast_to(tot1 + tot2, (LANE, dk))))
            dec = jnp.concatenate([dec] * (dv // LANE), axis=1)
            s_ref[hh] = s * dec + lax.dot_general(kmat, v, TN_DIMS, preferred_element_type=F32)
            if final:
                gg = gg_ref[rows, vc].astype(F32)
                o = _head_norm_gate(o, ob_ref[rows, vc], gg * _sigmoid(gg), nw_ref[:, vc])
            o_ref[rows, vc] = o.astype(o_ref.dtype)


def _gla_scan(proj, lr, wlr, blr, l, *, reverse, dk, dv, col_q, col_k, col_v, col_gate=None,
              other=None, norm_w=None):
    bsz, seq, _ = proj.shape
    lb = _pick(seq, 512, LANE)
    assert lb % (2 * GLA_CHUNK) == 0
    hps = HEADS
    nblk = seq // lb
    final = other is not None
    qk, width = hps * dk, hps * dv

    def tok(i):
        return (nblk - 1 - i) if reverse else i

    def col_spec(w, col0):
        return pl.BlockSpec((None, lb, w), lambda b, h, i: (b, tok(i), col0 // w + h))

    d = 1 if reverse else 0
    in_specs = [col_spec(qk, col_q), col_spec(qk, col_k), col_spec(width, col_v),
                pl.BlockSpec((None, lb, LANE), lambda b, h, i: (b, tok(i), 0)),
                pl.BlockSpec((None, None, LANE, qk), lambda b, h, i: (l, d, 0, h)),
                pl.BlockSpec((None, None, 1, qk), lambda b, h, i: (l, d, 0, h))]
    args = [proj, proj, proj, lr, wlr, blr]
    if final:
        in_specs += [col_spec(width, col_gate),
                     pl.BlockSpec((None, lb, width), lambda b, h, i: (b, tok(i), h)),
                     pl.BlockSpec((None, 1, width), lambda b, h, i: (l, 0, h))]
        args += [proj, other, norm_w]
    return pl.pallas_call(
        functools.partial(_gla_body, reverse=reverse, final=final, lb=lb, hps=hps),
        grid=(bsz, HEADS // hps, nblk),
        in_specs=in_specs,
        out_specs=pl.BlockSpec((None, lb, width), lambda b, h, i: (b, tok(i), h)),
        out_shape=jax.ShapeDtypeStruct((bsz, seq, HEADS * dv), BF16 if final else F32),
        scratch_shapes=[pltpu.VMEM((hps, dk, dv), F32)],
        compiler_params=_params(("parallel", "parallel", "arbitrary")),
        name="gla_fwd" if final else "gla_bwd",
    )(*args)


def _segment_cumsum_lanes(x, seg, reverse):
    n = x.shape[-1]
    pos = lax.broadcasted_iota(jnp.int32, x.shape, x.ndim - 1) % seg
    s = 1
    while s < seg:
        if reverse:
            x = x + jnp.where(pos < seg - s, pltpu.roll(x, n - s, x.ndim - 1), 0.0)
        else:
            x = x + jnp.where(pos >= s, pltpu.roll(x, s, x.ndim - 1), 0.0)
        s *= 2
    return x


def _mlstm_body(q_ref, k_ref, v_ref, gr_ref, br_ref, *rest, reverse, final, lb, hps):
    if final:
        mo_ref, hb_ref, nw_ref, o_ref, c_ref, m_ref = rest
    else:
        o_ref, c_ref, m_ref = rest
    dk = q_ref.shape[-1] // hps
    dv = v_ref.shape[-1] // hps
    chunk = min(MLSTM_FWD_CHUNK if final else MLSTM_BWD_CHUNK, lb)

    @pl.when(pl.program_id(2) == 0)
    def _():
        c_ref[...] = jnp.zeros_like(c_ref)
        m_ref[...] = jnp.full_like(m_ref, NEG_BIG)

    ti, tf = (2, 3) if reverse else (0, 1)
    gr = gr_ref[...] + br_ref[:, 0:1]
    bcum = _segment_cumsum_lanes(_log_sigmoid(gr), chunk, reverse)
    row_id = lax.broadcasted_iota(jnp.int32, gr.shape, 0) % SUBLANE
    rowform = jnp.where(row_id == tf, bcum, gr)
    pad_rows = jnp.zeros((LANE - SUBLANE, chunk), F32)

    mask = _causal_mask(chunk, reverse)
    ones_col = jnp.where(lax.broadcasted_iota(jnp.int32, (chunk, LANE), 1) == 0, 1.0, 0.0).astype(BF16)
    scale = dk ** -0.5
    nch = lb // chunk
    for ch in (range(nch - 1, -1, -1) if reverse else range(nch)):
        rows = slice(ch * chunk, (ch + 1) * chunk)
        for hh in range(hps):
            kc = slice(hh * dk, (hh + 1) * dk)
            vc = slice(hh * dv, (hh + 1) * dv)
            rf = rowform[hh * SUBLANE:(hh + 1) * SUBLANE, rows]
            u_row = rf[ti:ti + 1, :] - rf[tf:tf + 1, :]
            colform = jnp.transpose(jnp.concatenate([rf, pad_rows], axis=0))
            bc = colform[:, tf:tf + 1]
            ic = colform[:, ti:ti + 1]
            gtot = bc[0:1, :] if reverse else bc[chunk - 1:chunk, :]
            log_d = jnp.where(mask, bc + u_row, NEG_BIG)
            m_intra = jnp.max(log_d, axis=1, keepdims=True)
            q = q_ref[rows, kc]
            k = k_ref[rows, kc]
            v_aug = jnp.concatenate([v_ref[rows, vc], ones_col], axis=1)
            s = lax.dot_general(q, k, NT_DIMS, preferred_element_type=F32) * jnp.exp(log_d - m_intra)
            intra = _dot(s.astype(BF16), v_aug)
            c = c_ref[hh]
            inter = _dot(q, c.astype(BF16))
            m_prev = m_ref[hh, 0:1, 0:1]
            bm = bc + m_prev
            m_j = jnp.maximum(bm, m_intra)
            comb = (scale * jnp.exp(bm - m_j)) * inter + (scale * jnp.exp(m_intra - m_j)) * intra
            denom = jnp.maximum(jnp.abs(comb[:, dv:dv + 1]), jnp.exp(-m_j))
            h = comb[:, :dv] * (1.0 / denom)
            a_col = gtot - bc + ic
            m_new = jnp.maximum(gtot + m_prev, jnp.max(a_col, axis=0, keepdims=True))
            kw = (k.astype(F32) * jnp.exp(a_col - m_new)).astype(BF16)
            c_ref[hh] = (jnp.exp(gtot + m_prev - m_new) * c
                         + lax.dot_general(kw, v_aug, TN_DIMS, preferred_element_type=F32))
            m_ref[hh] = jnp.broadcast_to(m_new, (SUBLANE, LANE))
            if final:
                h = _head_norm_gate(h, hb_ref[rows, vc], _sigmoid(mo_ref[rows, vc].astype(F32)), nw_ref[:, vc])
            o_ref[rows, vc] = h.astype(o_ref.dtype)


def _mlstm_scan(proj, gates_t, bias_r, l, *, reverse, dk, dv, col_q, col_k, col_v,
                col_gate=None, other=None, norm_w=None, hps=HEADS):
    bsz, seq, _ = proj.shape
    lb = _pick(seq, 512, LANE)
    nblk = seq // lb
    final = other is not None
    assert HEADS % hps == 0
    qk, width = hps * dk, hps * dv

    def tok(i):
        return (nblk - 1 - i) if reverse else i

    def col_spec(w, col0):
        return pl.BlockSpec((None, lb, w), lambda b, h, i: (b, tok(i), col0 // w + h))

    in_specs = [col_spec(qk, col_q), col_spec(qk, col_k), col_spec(width, col_v),
                pl.BlockSpec((hps * SUBLANE, lb), lambda b, h, i: (h, b * nblk + tok(i))),
                pl.BlockSpec((None, hps * SUBLANE, LANE), lambda b, h, i: (l, h, 0))]
    args = [proj, proj, proj, gates_t, bias_r]
    if final:
        in_specs += [col_spec(width, col_gate),
                     pl.BlockSpec((None, lb, width), lambda b, h, i: (b, tok(i), h)),
                     pl.BlockSpec((None, 1, width), lambda b, h, i: (l, 0, h))]
        args += [proj, other, norm_w]
    return pl.pallas_call(
        functools.partial(_mlstm_body, reverse=reverse, final=final, lb=lb, hps=hps),
        grid=(bsz, HEADS // hps, nblk),
        in_specs=in_specs,
        out_specs=pl.BlockSpec((None, lb, width), lambda b, h, i: (b, tok(i), h)),
        out_shape=jax.ShapeDtypeStruct((bsz, seq, HEADS * dv), BF16 if final else F32),
        scratch_shapes=[pltpu.VMEM((hps, dk, dv + LANE), F32), pltpu.VMEM((hps, SUBLANE, LANE), F32)],
        compiler_params=_params(("parallel", "parallel", "arbitrary")),
        name="mlstm_fwd" if final else "mlstm_bwd",
    )(*args)


def _xattn_body(x_ref, nw_ref, wq_ref, kv_ref, wo_ref, nw2_ref, o_ref, n_ref, *, heads):
    x = x_ref[...]
    xn = (x * lax.rsqrt(jnp.mean(x * x, axis=-1, keepdims=True) + EPS) * nw_ref[...]).astype(BF16)
    width = wq_ref.shape[1]
    hd = width // heads
    q = (_dot(xn, wq_ref[...]) * (hd ** -0.5)).astype(BF16)
    outs = []
    for h in range(heads):
        kh = kv_ref[:, h * hd:(h + 1) * hd]
        vh = kv_ref[:, width + h * hd:width + (h + 1) * hd]
        s = lax.dot_general(q[:, h * hd:(h + 1) * hd], kh, NT_DIMS, preferred_element_type=F32)
        p = jnp.exp(s - jnp.max(s, axis=-1, keepdims=True))
        p = p * (1.0 / jnp.sum(p, axis=-1, keepdims=True))
        outs.append(_dot(p.astype(BF16), vh))
    o = jnp.concatenate(outs, axis=1).astype(BF16)
    y = x + _dot(o, wo_ref[...])
    o_ref[...] = y
    n_ref[...] = (y * lax.rsqrt(jnp.mean(y * y, axis=-1, keepdims=True) + EPS) * nw2_ref[...]).astype(BF16)


def _xattn(x, norm_w, wq, kv, wo, next_norm_w, l, tq=512):
    bsz, seq, d = x.shape
    mem = kv.shape[1]
    width = wq.shape[2]
    tq = _pick(seq, tq, 16)
    return pl.pallas_call(
        functools.partial(_xattn_body, heads=XATTN_HEADS),
        grid=(bsz, seq // tq),
        in_specs=[pl.BlockSpec((None, tq, d), lambda b, i: (b, i, 0)),
                  pl.BlockSpec((None, 1, d), lambda b, i: (l, 0, 0)),
                  pl.BlockSpec((None, d, width), lambda b, i: (l, 0, 0), pipeline_mode=pl.Buffered(1)),
                  pl.BlockSpec((None, mem, 2 * width), lambda b, i: (b, 0, 0)),
                  pl.BlockSpec((None, width, d), lambda b, i: (l, 0, 0), pipeline_mode=pl.Buffered(1)),
                  pl.BlockSpec((None, 1, d), lambda b, i: (l, 0, 0))],
        out_specs=[pl.BlockSpec((None, tq, d), lambda b, i: (b, i, 0)),
                   pl.BlockSpec((None, tq, d), lambda b, i: (b, i, 0))],
        out_shape=[jax.ShapeDtypeStruct((bsz, seq, d), F32),
                   jax.ShapeDtypeStruct((bsz, seq, d), BF16)],
        compiler_params=_params(("parallel", "arbitrary")),
        name="mem_xattn",
    )(x, norm_w, wq, kv, wo, next_norm_w)


def _norm_w(w):
    return w.astype(F32).reshape(w.shape[0], 1, w.shape[1])


def _ffn(x, xn, w_gate, w_up, w_down, l, next_gain=None):
    hid = _swiglu_up(xn, w_gate, w_up, l)
    return _matmul_residual(hid, w_down, l, x, 0.5, next_gain=next_gain, name="ffn_down")


def _mixer_weights(d, w_in, gla_w_lr, gla_b_lr, mlstm_gate_b):
    depth = w_in.shape[0]
    half = d // 2
    qk = half // 2
    o_lr = 2 * qk + 2 * half
    o_m = o_lr + 2 * GLA_RANK
    o_mg = o_m + 2 * qk + 2 * half
    w_gla = w_in[:, :, :o_lr].astype(BF16)
    w_ml = w_in[:, :, o_m:o_mg].astype(BF16)
    w_lr = jnp.pad(w_in[:, :, o_lr:o_m], ((0, 0), (0, 0), (0, LANE - 2 * GLA_RANK))).astype(BF16)
    mg = w_in[:, :, o_mg:o_mg + 4 * HEADS].reshape(depth, d, 4, HEADS).transpose(0, 3, 2, 1)
    w_gates_t = jnp.pad(mg, ((0, 0), (0, 0), (0, SUBLANE - 4), (0, 0))).reshape(depth, HEADS * SUBLANE, d)
    w_gates_t = w_gates_t.astype(BF16)
    bias_r = jnp.pad(mlstm_gate_b.astype(F32).transpose(0, 2, 1), ((0, 0), (0, 0), (0, SUBLANE - 4)))
    bias_r = jnp.broadcast_to(bias_r.reshape(depth, HEADS * SUBLANE, 1), (depth, HEADS * SUBLANE, LANE))
    wlr = jnp.stack([jnp.pad(gla_w_lr[:, 0], ((0, 0), (0, LANE - GLA_RANK), (0, 0))),
                     jnp.pad(gla_w_lr[:, 1], ((0, 0), (GLA_RANK, LANE - 2 * GLA_RANK), (0, 0)))],
                    axis=1).astype(F32)
    blr = gla_b_lr.astype(F32).reshape(depth, 2, 1, qk)
    return w_gla, w_ml, w_lr, w_gates_t, bias_r, wlr, blr


def _mixer(x, xn, bsz, seq, l, w_gla, w_ml, w_lr, w_gates_t, bias_r, wlr, blr, gla_norm, mlstm_norm, w_out):
    m, d = x.shape
    half = d // 2
    dv = half // HEADS
    dk = dv // 2
    qk = HEADS * dk
    proj_g = _matmul(xn, w_gla, l, BF16, name="gla_in_proj").reshape(bsz, seq, -1)
    proj_m = _matmul(xn, w_ml, l, BF16, name="mlstm_in_proj").reshape(bsz, seq, -1)
    lr, gates_t = _gate_proj(xn, w_lr, w_gates_t, l)
    lr = lr.reshape(bsz, seq, LANE)

    kw = dict(dk=dk, dv=dv, col_q=0, col_k=qk, col_v=2 * qk)
    g_b = _gla_scan(proj_g, lr, wlr, blr, l, reverse=True, **kw)
    g_out = _gla_scan(proj_g, lr, wlr, blr, l, reverse=False, col_gate=2 * qk + half, other=g_b,
                      norm_w=gla_norm, **kw)
    m_b = _mlstm_scan(proj_m, gates_t, bias_r, l, reverse=True, **kw)
    m_out = _mlstm_scan(proj_m, gates_t, bias_r, l, reverse=False, col_gate=2 * qk + half,
                        other=m_b, norm_w=mlstm_norm, hps=MLSTM_FWD_HEADS_PER_STEP, **kw)
    return _out_proj(g_out.reshape(m, half), m_out.reshape(m, half), w_out, l, x)


def kernel(x_prompt, x_sample, mem_prompt, mem_sample, ffn1_norm, ffn1_w_gate, ffn1_w_up, ffn1_w_down, mix_norm, w_in, gla_w_lr, gla_b_lr, gla_out_norm, mlstm_gate_b, mlstm_out_norm, w_out, xattn_norm, mem_norm, xattn_wq, xattn_wk, xattn_wv, xattn_wo, ffn2_norm, ffn2_w_gate, ffn2_w_up, ffn2_w_down, final_norm):
    assert x_prompt.shape[1:] == x_sample.shape[1:] and mem_prompt.shape[1:] == mem_sample.shape[1:]
    n_p, seq, d = x_prompt.shape
    n_s = x_sample.shape[0]
    bsz = n_p + n_s
    m = bsz * seq
    x = (x_prompt.reshape(n_p * seq, d), x_sample.reshape(n_s * seq, d))
    mem = jnp.concatenate([mem_prompt, mem_sample], axis=0)
    n_mem = mem.shape[1]
    mem = mem.reshape(bsz * n_mem, d)
    depth = ffn1_norm.shape[0]

    ffn1 = (ffn1_w_gate.astype(BF16), ffn1_w_up.astype(BF16), ffn1_w_down.astype(BF16))
    ffn2 = (ffn2_w_gate.astype(BF16), ffn2_w_up.astype(BF16), ffn2_w_down.astype(BF16))
    mixer_w = _mixer_weights(d, w_in, gla_w_lr, gla_b_lr, mlstm_gate_b)
    mixer_rest = (_norm_w(gla_out_norm), _norm_w(mlstm_out_norm), w_out.astype(BF16))
    w_kv = jnp.concatenate([xattn_wk, xattn_wv], axis=2).astype(BF16)
    wq = xattn_wq.astype(BF16)
    wo = xattn_wo.astype(BF16)
    ffn1_n, ffn2_n = _norm_w(ffn1_norm), _norm_w(ffn2_norm)
    mix_n, xattn_n, mem_n = _norm_w(mix_norm), _norm_w(xattn_norm), _norm_w(mem_norm)

    xn = _rmsnorm_pair(*x, ffn1_n, 0, BF16)
    for l in range(depth):
        x, xn = _ffn(x, xn, *ffn1, l, next_gain=(mix_n, l))
        x = _mixer(x, xn, bsz, seq, l, *mixer_w, *mixer_rest)
        kv = _matmul(_rmsnorm(mem, mem_n, l, BF16), w_kv, l, BF16, tm=512, name="mem_kv_proj")
        x, xn = _xattn(x.reshape(bsz, seq, d), xattn_n, wq, kv.reshape(bsz, n_mem, -1), wo, ffn2_n, l)
        x, xn = x.reshape(m, d), xn.reshape(m, d)
        if l + 1 < depth:
            x, xn = _ffn(x, xn, *ffn2, l, next_gain=(ffn1_n, l + 1))
        else:
            x = _ffn(x, xn, *ffn2, l)
    fin = _norm_w(final_norm.reshape(1, d))
    y_p = _rmsnorm(x, fin, 0, F32, 0, n_p * seq).reshape(n_p, seq, d)
    y_s = _rmsnorm(x, fin, 0, F32, n_p * seq, n_s * seq).reshape(n_s, seq, d)
    return (y_p, y_s)
```

```python
import functools

import jax
import jax.numpy as jnp
from jax import lax
from jax.experimental import pallas as pl
from jax.experimental.pallas import tpu as pltpu

F32 = jnp.float32
BF16 = jnp.bfloat16

EPS = 1e-6
NEG_BIG = -1e30
HEADS = 4
GLA_RANK = 16
GLA_TAU = 16.0
GLA_MIN_LOG_DECAY = -1.0
GLA_CHUNK = 64
MLSTM_BWD_CHUNK = 512
MLSTM_FWD_CHUNK = 512
MLSTM_FWD_HEADS_PER_STEP = 4
XATTN_HEADS = 4
LANE = 128
SUBLANE = 8
VMEM_LIMIT = 56 * 1024 * 1024

NT_DIMS = (((1,), (1,)), ((), ()))
TN_DIMS = (((0,), (0,)), ((), ()))


def _pick(n, target, mult):
    if n <= target:
        return n
    t = (target // mult) * mult
    while t >= mult:
        if n % t == 0:
            return t
        t -= mult
    return n


def _params(sem):
    return pltpu.CompilerParams(dimension_semantics=sem, vmem_limit_bytes=VMEM_LIMIT)


def _log_sigmoid(x):
    return jnp.minimum(x, 0.0) - jnp.log(1.0 + jnp.exp(-jnp.abs(x)))


def _sigmoid(x):
    return 0.5 * jnp.tanh(0.5 * x) + 0.5


def _split_bf16(x):
    hi = x.astype(BF16)
    return hi, (x - hi.astype(F32)).astype(BF16)


def _dot(a, b):
    return jnp.dot(a, b, preferred_element_type=F32)


def _rmsnorm_body(x_ref, w_ref, o_ref):
    x = x_ref[...]
    ms = jnp.mean(x * x, axis=-1, keepdims=True)
    o_ref[...] = (x * lax.rsqrt(ms + EPS) * w_ref[...]).astype(o_ref.dtype)


def _pair_specs(block, na, col=lambda *g: 0):
    return [pl.BlockSpec(block, lambda *g: (jnp.minimum(g[0], na - 1), col(*g))),
            pl.BlockSpec(block, lambda *g: (jnp.maximum(g[0] - na, 0), col(*g)))]


def _rmsnorm_pair_body(xa_ref, xb_ref, w_ref, o_ref, *, na):
    @pl.when(pl.program_id(0) < na)
    def _():
        _rmsnorm_body(xa_ref, w_ref, o_ref)

    @pl.when(pl.program_id(0) >= na)
    def _():
        _rmsnorm_body(xb_ref, w_ref, o_ref)


def _rmsnorm_pair(xa, xb, w, l, out_dtype):
    d = xa.shape[1]
    tr = _pick(xa.shape[0], 256, SUBLANE * 2)
    assert xa.shape[0] % tr == 0 and xb.shape[0] % tr == 0
    na, nb = xa.shape[0] // tr, xb.shape[0] // tr
    return pl.pallas_call(
        functools.partial(_rmsnorm_pair_body, na=na),
        grid=(na + nb,),
        in_specs=_pair_specs((tr, d), na) + [pl.BlockSpec((None, 1, d), lambda i: (l, 0, 0))],
        out_specs=pl.BlockSpec((tr, d), lambda i: (i, 0)),
        out_shape=jax.ShapeDtypeStruct(((na + nb) * tr, d), out_dtype),
        compiler_params=_params(("parallel",)),
        name="rmsnorm",
    )(xa, xb, w)


def _rmsnorm(x, w, l, out_dtype, row_start=0, rows=None):
    m, d = x.shape
    rows = m if rows is None else rows
    tr = _pick(rows, 256, SUBLANE * 2)
    assert row_start % tr == 0
    off = row_start // tr
    return pl.pallas_call(
        _rmsnorm_body,
        grid=(rows // tr,),
        in_specs=[pl.BlockSpec((tr, d), lambda i: (i + off, 0)),
                  pl.BlockSpec((None, 1, d), lambda i: (l, 0, 0))],
        out_specs=pl.BlockSpec((tr, d), lambda i: (i, 0)),
        out_shape=jax.ShapeDtypeStruct((rows, d), out_dtype),
        compiler_params=_params(("parallel",)),
        name="rmsnorm",
    )(x, w)


def _row_scale(rs_ref, width):
    return jnp.concatenate([rs_ref[...]] * (width // LANE), axis=1)


def _split_act(a):
    return a if isinstance(a, tuple) else (a, None)


def _ss_spec(tm, ss):
    return [] if ss is None else [pl.BlockSpec((tm, LANE), lambda i, *_: (i, 0))]


def _mm_body(a_ref, b_ref, *rest):
    *ss_ref, o_ref = rest
    p = _dot(a_ref[...], b_ref[...])
    if ss_ref:
        p = p * _row_scale(ss_ref[0], p.shape[1])
    o_ref[...] = p.astype(o_ref.dtype)


def _matmul(a, w, l, out_dtype, tm=1024, tn=1024, name="matmul"):
    a, ss = _split_act(a)
    m, k = a.shape
    n = w.shape[2]
    tm = _pick(m, tm, 16)
    tn = _pick(n, tn, LANE)
    return pl.pallas_call(
        _mm_body,
        grid=(m // tm, n // tn),
        in_specs=[pl.BlockSpec((tm, k), lambda i, j: (i, 0)),
                  pl.BlockSpec((None, k, tn), lambda i, j: (l, 0, j))] + _ss_spec(tm, ss),
        out_specs=pl.BlockSpec((tm, tn), lambda i, j: (i, j)),
        out_shape=jax.ShapeDtypeStruct((m, n), out_dtype),
        compiler_params=_params(("parallel", "arbitrary")),
        name=name,
    )(a, w, *([] if ss is None else [ss]))


SWIGLU_ROW_TILES = 4


def _swiglu_body(a_ref, wg_ref, wu_ref, *rest):
    *ss_ref, o_ref = rest
    rows = a_ref.shape[0] // SWIGLU_ROW_TILES
    for t in range(SWIGLU_ROW_TILES):
        r = slice(t * rows, (t + 1) * rows)
        a = a_ref[r, :]
        g = _dot(a, wg_ref[...])
        u = _dot(a, wu_ref[...])
        if ss_ref:
            rs = jnp.concatenate([ss_ref[0][r, :]] * (g.shape[1] // LANE), axis=1)
            g, u = g * rs, u * rs
        o_ref[r, :] = (g * _sigmoid(g) * u).astype(o_ref.dtype)


def _swiglu_up(a, wg, wu, l, tm=2048, tn=256):
    a, ss = _split_act(a)
    m, k = a.shape
    n = wg.shape[2]
    tm = _pick(m, tm, 16 * SWIGLU_ROW_TILES)
    tn = min(tn, n)
    return pl.pallas_call(
        _swiglu_body,
        grid=(m // tm, pl.cdiv(n, tn)),
        in_specs=[pl.BlockSpec((tm, k), lambda i, j: (i, 0)),
                  pl.BlockSpec((None, k, tn), lambda i, j: (l, 0, j)),
                  pl.BlockSpec((None, k, tn), lambda i, j: (l, 0, j))] + _ss_spec(tm, ss),
        out_specs=pl.BlockSpec((tm, tn), lambda i, j: (i, j)),
        out_shape=jax.ShapeDtypeStruct((m, n), BF16),
        compiler_params=_params(("parallel", "arbitrary")),
        name="swiglu_up",
    )(a, wg, wu, *([] if ss is None else [ss]))


def _mm_res_body(a_ref, b_ref, *rest, scale, na, fuse, width):
    n_res = 1 if na is None else 2
    res_refs, rest = rest[:n_res], rest[n_res:]
    if na is None:
        r = res_refs[0][...]
    else:
        r = jnp.where(pl.program_id(0) < na, res_refs[0][...], res_refs[1][...])
    y = r + scale * _dot(a_ref[...], b_ref[...])
    if not fuse:
        rest[0][...] = y
        return
    gain_ref, o_ref, yw_ref, ss_ref = rest
    o_ref[...] = y
    yw_ref[...] = (y * gain_ref[...]).astype(yw_ref.dtype)
    sq = y * y
    part = sq[:, 0:LANE]
    for t in range(1, sq.shape[1] // LANE):
        part = part + sq[:, t * LANE:(t + 1) * LANE]

    @pl.when(pl.program_id(1) == 0)
    def _():
        ss_ref[...] = part

    @pl.when(pl.program_id(1) > 0)
    def _():
        ss_ref[...] += part

    @pl.when(pl.program_id(1) == pl.num_programs(1) - 1)
    def _():
        tot = jnp.sum(ss_ref[...], axis=1, keepdims=True)
        ss_ref[...] = jnp.broadcast_to(lax.rsqrt(tot * (1.0 / width) + EPS), ss_ref.shape)


def _matmul_residual(a, w, l, res, scale, tm=512, tn=512, next_gain=None, name="matmul_res"):
    m, k = a.shape
    n = w.shape[2]
    tm = _pick(m, tm, 16)
    tn = _pick(n, tn, LANE)
    if isinstance(res, tuple):
        assert res[0].shape[0] % tm == 0 and res[1].shape[0] % tm == 0
        na = res[0].shape[0] // tm
        res_specs = _pair_specs((tm, tn), na, col=lambda i, j: j)
    else:
        na, res = None, (res,)
        res_specs = [pl.BlockSpec((tm, tn), lambda i, j: (i, j))]
    in_specs = [pl.BlockSpec((tm, k), lambda i, j: (i, 0)),
                pl.BlockSpec((None, k, tn), lambda i, j: (l, 0, j))] + res_specs
    out_specs = pl.BlockSpec((tm, tn), lambda i, j: (i, j))
    out_shape = jax.ShapeDtypeStruct((m, n), F32)
    args = [a, w, *res]
    fuse = next_gain is not None
    if fuse:
        gains, gl = next_gain
        in_specs.append(pl.BlockSpec((None, 1, tn), lambda i, j: (gl, 0, j)))
        args.append(gains)
        out_specs = [out_specs, pl.BlockSpec((tm, tn), lambda i, j: (i, j)),
                     pl.BlockSpec((tm, LANE), lambda i, j: (i, 0))]
        out_shape = [out_shape, jax.ShapeDtypeStruct((m, n), BF16), jax.ShapeDtypeStruct((m, LANE), F32)]
    out = pl.pallas_call(
        functools.partial(_mm_res_body, scale=scale, na=na, fuse=fuse, width=n),
        grid=(m // tm, n // tn),
        in_specs=in_specs,
        out_specs=out_specs,
        out_shape=out_shape,
        compiler_params=_params(("parallel", "arbitrary")),
        name=name,
    )(*args)
    return (out[0], (out[1], out[2])) if fuse else out


def _mm2_res_body(a1_ref, a2_ref, b1_ref, b2_ref, r_ref, o_ref):
    a = jnp.concatenate([a1_ref[...], a2_ref[...]], axis=1)
    b = jnp.concatenate([b1_ref[...], b2_ref[...]], axis=0)
    o_ref[...] = r_ref[...] + _dot(a, b)


def _out_proj(a1, a2, w, l, res, tm=1024, tn=512):
    m, k1 = a1.shape
    k2 = a2.shape[1]
    assert k1 == k2 and w.shape[1] == k1 + k2
    n = w.shape[2]
    tm = _pick(m, tm, 16)
    tn = _pick(n, tn, LANE)
    return pl.pallas_call(
        _mm2_res_body,
        grid=(m // tm, n // tn),
        in_specs=[pl.BlockSpec((tm, k1), lambda i, j: (i, 0)),
                  pl.BlockSpec((tm, k2), lambda i, j: (i, 0)),
                  pl.BlockSpec((None, k1, tn), lambda i, j: (l, 0, j)),
                  pl.BlockSpec((None, k2, tn), lambda i, j: (l, 1, j)),
                  pl.BlockSpec((tm, tn), lambda i, j: (i, j))],
        out_specs=pl.BlockSpec((tm, tn), lambda i, j: (i, j)),
        out_shape=jax.ShapeDtypeStruct((m, n), F32),
        compiler_params=_params(("parallel", "arbitrary")),
        name="mixer_out_proj",
    )(a1, a2, w, w, res)


def _gate_proj_body(a_ref, w_ref, wt_ref, *rest):
    *ss_ref, o_ref, ot_ref = rest
    a = a_ref[...]
    o = _dot(a, w_ref[...])
    ot = lax.dot_general(wt_ref[...], a, NT_DIMS, preferred_element_type=F32)
    if ss_ref:
        o = o * _row_scale(ss_ref[0], o.shape[1])
        ot = ot * jnp.transpose(ss_ref[0][...])[0:1, :]
    o_ref[...] = o
    ot_ref[...] = ot


def _gate_proj(a, w, wt, l, tm=1024):
    a, ss = _split_act(a)
    m, k = a.shape
    n = w.shape[2]
    nt = wt.shape[1]
    tm = _pick(m, tm, LANE)
    return pl.pallas_call(
        _gate_proj_body,
        grid=(m // tm,),
        in_specs=[pl.BlockSpec((tm, k), lambda i: (i, 0)),
                  pl.BlockSpec((None, k, n), lambda i: (l, 0, 0)),
                  pl.BlockSpec((None, nt, k), lambda i: (l, 0, 0))] + _ss_spec(tm, ss),
        out_specs=[pl.BlockSpec((tm, n), lambda i: (i, 0)),
                   pl.BlockSpec((nt, tm), lambda i: (0, i))],
        out_shape=[jax.ShapeDtypeStruct((m, n), F32),
                   jax.ShapeDtypeStruct((nt, m), F32)],
        compiler_params=_params(("parallel",)),
        name="gate_proj",
    )(a, w, wt, *([] if ss is None else [ss]))


def _causal_mask(chunk, reverse):
    r = lax.broadcasted_iota(jnp.int32, (chunk, chunk), 0)
    c = lax.broadcasted_iota(jnp.int32, (chunk, chunk), 1)
    return (c >= r) if reverse else (c <= r)


def _head_norm_gate(h, other, gate, nw):
    h = h + other
    return h * lax.rsqrt(jnp.mean(h * h, axis=-1, keepdims=True) + EPS) * nw * gate


def _gla_body(q_ref, k_ref, v_ref, lr_ref, wlr_ref, blr_ref, *rest, reverse, final, lb, hps):
    if final:
        gg_ref, ob_ref, nw_ref, o_ref, s_ref = rest
    else:
        o_ref, s_ref = rest
    dk = q_ref.shape[-1] // hps
    dv = v_ref.shape[-1] // hps
    chunk = GLA_CHUNK
    nch = lb // chunk
    qk = hps * dk

    @pl.when(pl.program_id(2) == 0)
    def _():
        s_ref[...] = jnp.zeros_like(s_ref)

    lh, ll = _split_bf16(lr_ref[...])
    wh, wl = _split_bf16(wlr_ref[...])
    z = _dot(lh, wh) + _dot(lh, wl) + _dot(ll, wh) + blr_ref[...]
    g = jnp.maximum(_log_sigmoid(z) * (1.0 / GLA_TAU), GLA_MIN_LOG_DECAY)
    mask = _causal_mask(chunk, reverse)
    tri = jnp.where(mask, 1.0, 0.0).astype(BF16)
    gh, gl = _split_bf16(jnp.concatenate([g[c * chunk:(c + 1) * chunk, :] for c in range(nch)], axis=1))
    bcum = _dot(tri, gh) + _dot(tri, gl)

    scale = dk ** -0.5
    zero_blk = jnp.zeros((chunk, chunk), BF16)
    npair = nch // 2
    for pb in (range(npair - 1, -1, -1) if reverse else range(npair)):
        rows = slice(2 * pb * chunk, (2 * pb + 2) * chunk)
        c_first, c_second = (2 * pb + 1, 2 * pb) if reverse else (2 * pb, 2 * pb + 1)
        for hh in range(hps):
            kc = slice(hh * dk, (hh + 1) * dk)
            vc = slice(hh * dv, (hh + 1) * dv)

            def chunk_terms(c):
                b = bcum[:, c * qk + hh * dk:c * qk + (hh + 1) * dk]
                tot = b[0:1, :] if reverse else b[chunk - 1:chunk, :]
                r = slice(c * chunk, (c + 1) * chunk)
                q_dec = q_ref[r, kc].astype(F32) * (jnp.exp(b) * scale)
                k_inv = k_ref[r, kc].astype(F32) * jnp.exp(-b)
                return tot, q_dec, k_inv, k_inv * jnp.exp(tot)

            tot1, qd1, ki1, ke1 = chunk_terms(c_first)
            tot2, qd2, ki2, ke2 = chunk_terms(c_second)
            qd1b, qd2b = qd1.astype(BF16), qd2.astype(BF16)
            a11 = lax.dot_general(qd1b, ki1.astype(BF16), NT_DIMS, preferred_element_type=F32)
            a22 = lax.dot_general(qd2b, ki2.astype(BF16), NT_DIMS, preferred_element_type=F32)
            a21 = lax.dot_general(qd2b, ke1.astype(BF16), NT_DIMS, preferred_element_type=F32)
            a11 = jnp.where(mask, a11, 0.0).astype(BF16)
            a22 = jnp.where(mask, a22, 0.0).astype(BF16)
            a21 = a21.astype(BF16)
            qs2 = (qd2 * jnp.exp(tot1)).astype(BF16)
            kx1 = (ke1 * jnp.exp(tot2)).astype(BF16)
            ke2b = ke2.astype(BF16)
            if reverse:
                amat = jnp.concatenate([jnp.concatenate([a22, a21], axis=1),
                                        jnp.concatenate([zero_blk, a11], axis=1)], axis=0)
                qmat = jnp.concatenate([qs2, qd1b], axis=0)
                kmat = jnp.concatenate([ke2b, kx1], axis=0)
            else:
                amat = jnp.concatenate([jnp.concatenate([a11, zero_blk], axis=1),
                                        jnp.concatenate([a21, a22], axis=1)], axis=0)
                qmat = jnp.concatenate([qd1b, qs2], axis=0)
                kmat = jnp.concatenate([kx1, ke2b], axis=0)
            v = v_ref[rows, vc]
            s = s_ref[hh]
            o = _dot(amat, v) + _dot(qmat, s.astype(BF16))
            dec = jnp.exp(jnp.transpose(jnp.broadcast_to(tot1 + tot2, (LANE, dk))))
            dec = jnp.concatenate([dec] * (dv // LANE), axis=1)
            s_ref[hh] = s * dec + lax.dot_general(kmat, v, TN_DIMS, preferred_element_type=F32)
            if final:
                gg = gg_ref[rows, vc].astype(F32)
                o = _head_norm_gate(o, ob_ref[rows, vc], gg * _sigmoid(gg), nw_ref[:, vc])
            o_ref[rows, vc] = o.astype(o_ref.dtype)


def _gla_scan(proj, lr, wlr, blr, l, *, reverse, dk, dv, col_q, col_k, col_v, col_gate=None,
              other=None, norm_w=None):
    bsz, seq, _ = proj.shape
    lb = _pick(seq, 512, LANE)
    assert lb % (2 * GLA_CHUNK) == 0
    hps = HEADS
    nblk = seq // lb
    final = other is not None
    qk, width = hps * dk, hps * dv

    def tok(i):
        return (nblk - 1 - i) if reverse else i

    def col_spec(w, col0):
        return pl.BlockSpec((None, lb, w), lambda b, h, i: (b, tok(i), col0 // w + h))

    d = 1 if reverse else 0
    in_specs = [col_spec(qk, col_q), col_spec(qk, col_k), col_spec(width, col_v),
                pl.BlockSpec((None, lb, LANE), lambda b, h, i: (b, tok(i), 0)),
                pl.BlockSpec((None, None, LANE, qk), lambda b, h, i: (l, d, 0, h)),
                pl.BlockSpec((None, None, 1, qk), lambda b, h, i: (l, d, 0, h))]
    args = [proj, proj, proj, lr, wlr, blr]
    if final:
        in_specs += [col_spec(width, col_gate),
                     pl.BlockSpec((None, lb, width), lambda b, h, i: (b, tok(i), h)),
                     pl.BlockSpec((None, 1, width), lambda b, h, i: (l, 0, h))]
        args += [proj, other, norm_w]
    return pl.pallas_call(
        functools.partial(_gla_body, reverse=reverse, final=final, lb=lb, hps=hps),
        grid=(bsz, HEADS // hps, nblk),
        in_specs=in_specs,
        out_specs=pl.BlockSpec((None, lb, width), lambda b, h, i: (b, tok(i), h)),
        out_shape=jax.ShapeDtypeStruct((bsz, seq, HEADS * dv), BF16 if final else F32),
        scratch_shapes=[pltpu.VMEM((hps, dk, dv), F32)],
        compiler_params=_params(("parallel", "parallel", "arbitrary")),
        name="gla_fwd" if final else "gla_bwd",
    )(*args)


def _segment_cumsum_lanes(x, seg, reverse):
    n = x.shape[-1]
    pos = lax.broadcasted_iota(jnp.int32, x.shape, x.ndim - 1) % seg
    s = 1
    while s < seg:
        if reverse:
            x = x + jnp.where(pos < seg - s, pltpu.roll(x, n - s, x.ndim - 1), 0.0)
        else:
            x = x + jnp.where(pos >= s, pltpu.roll(x, s, x.ndim - 1), 0.0)
        s *= 2
    return x


def _mlstm_body(q_ref, k_ref, v_ref, gr_ref, br_ref, *rest, reverse, final, lb, hps):
    if final:
        mo_ref, hb_ref, nw_ref, o_ref, c_ref, m_ref = rest
    else:
        o_ref, c_ref, m_ref = rest
    dk = q_ref.shape[-1] // hps
    dv = v_ref.shape[-1] // hps
    chunk = min(MLSTM_FWD_CHUNK if final else MLSTM_BWD_CHUNK, lb)

    @pl.when(pl.program_id(2) == 0)
    def _():
        c_ref[...] = jnp.zeros_like(c_ref)
        m_ref[...] = jnp.full_like(m_ref, NEG_BIG)

    ti, tf = (2, 3) if reverse else (0, 1)
    gr = gr_ref[...] + br_ref[:, 0:1]
    bcum = _segment_cumsum_lanes(_log_sigmoid(gr), chunk, reverse)
    row_id = lax.broadcasted_iota(jnp.int32, gr.shape, 0) % SUBLANE
    rowform = jnp.where(row_id == tf, bcum, gr)
    pad_rows = jnp.zeros((LANE - SUBLANE, chunk), F32)

    mask = _causal_mask(chunk, reverse)
    ones_col = jnp.where(lax.broadcasted_iota(jnp.int32, (chunk, LANE), 1) == 0, 1.0, 0.0).astype(BF16)
    scale = dk ** -0.5
    nch = lb // chunk
    for ch in (range(nch - 1, -1, -1) if reverse else range(nch)):
        rows = slice(ch * chunk, (ch + 1) * chunk)
        for hh in range(hps):
            kc = slice(hh * dk, (hh + 1) * dk)
            vc = slice(hh * dv, (hh + 1) * dv)
            rf = rowform[hh * SUBLANE:(hh + 1) * SUBLANE, rows]
            u_row = rf[ti:ti + 1, :] - rf[tf:tf + 1, :]
            colform = jnp.transpose(jnp.concatenate([rf, pad_rows], axis=0))
            bc = colform[:, tf:tf + 1]
            ic = colform[:, ti:ti + 1]
            gtot = bc[0:1, :] if reverse else bc[chunk - 1:chunk, :]
            log_d = jnp.where(mask, bc + u_row, NEG_BIG)
            m_intra = jnp.max(log_d, axis=1, keepdims=True)
            q = q_ref[rows, kc]
            k = k_ref[rows, kc]
            v_aug = jnp.concatenate([v_ref[rows, vc], ones_col], axis=1)
            s = lax.dot_general(q, k, NT_DIMS, preferred_element_type=F32) * jnp.exp(log_d - m_intra)
            intra = _dot(s.astype(BF16), v_aug)
            c = c_ref[hh]
            inter = _dot(q, c.astype(BF16))
            m_prev = m_ref[hh, 0:1, 0:1]
            bm = bc + m_prev
            m_j = jnp.maximum(bm, m_intra)
            comb = (scale * jnp.exp(bm - m_j)) * inter + (scale * jnp.exp(m_intra - m_j)) * intra
            denom = jnp.maximum(jnp.abs(comb[:, dv:dv + 1]), jnp.exp(-m_j))
            h = comb[:, :dv] * (1.0 / denom)
            a_col = gtot - bc + ic
            m_new = jnp.maximum(gtot + m_prev, jnp.max(a_col, axis=0, keepdims=True))
            kw = (k.astype(F32) * jnp.exp(a_col - m_new)).astype(BF16)
            c_ref[hh] = (jnp.exp(gtot + m_prev - m_new) * c
                         + lax.dot_general(kw, v_aug, TN_DIMS, preferred_element_type=F32))
            m_ref[hh] = jnp.broadcast_to(m_new, (SUBLANE, LANE))
            if final:
                h = _head_norm_gate(h, hb_ref[rows, vc], _sigmoid(mo_ref[rows, vc].astype(F32)), nw_ref[:, vc])
            o_ref[rows, vc] = h.astype(o_ref.dtype)


def _mlstm_scan(proj, gates_t, bias_r, l, *, reverse, dk, dv, col_q, col_k, col_v,
                col_gate=None, other=None, norm_w=None, hps=HEADS):
    bsz, seq, _ = proj.shape
    lb = _pick(seq, 512, LANE)
    nblk = seq // lb
    final = other is not None
    assert HEADS % hps == 0
    qk, width = hps * dk, hps * dv

    def tok(i):
        return (nblk - 1 - i) if reverse else i

    def col_spec(w, col0):
        return pl.BlockSpec((None, lb, w), lambda b, h, i: (b, tok(i), col0 // w + h))

    in_specs = [col_spec(qk, col_q), col_spec(qk, col_k), col_spec(width, col_v),
                pl.BlockSpec((hps * SUBLANE, lb), lambda b, h, i: (h, b * nblk + tok(i))),
                pl.BlockSpec((None, hps * SUBLANE, LANE), lambda b, h, i: (l, h, 0))]
    args = [proj, proj, proj, gates_t, bias_r]
    if final:
        in_specs += [col_spec(width, col_gate),
                     pl.BlockSpec((None, lb, width), lambda b, h, i: (b, tok(i), h)),
                     pl.BlockSpec((None, 1, width), lambda b, h, i: (l, 0, h))]
        args += [proj, other, norm_w]
    return pl.pallas_call(
        functools.partial(_mlstm_body, reverse=reverse, final=final, lb=lb, hps=hps),
        grid=(bsz, HEADS // hps, nblk),
        in_specs=in_specs,
        out_specs=pl.BlockSpec((None, lb, width), lambda b, h, i: (b, tok(i), h)),
        out_shape=jax.ShapeDtypeStruct((bsz, seq, HEADS * dv), BF16 if final else F32),
        scratch_shapes=[pltpu.VMEM((hps, dk, dv + LANE), F32), pltpu.VMEM((hps, SUBLANE, LANE), F32)],
        compiler_params=_params(("parallel", "parallel", "arbitrary")),
        name="mlstm_fwd" if final else "mlstm_bwd",
    )(*args)


def _xattn_body(x_ref, nw_ref, wq_ref, kv_ref, wo_ref, nw2_ref, o_ref, n_ref, *, heads):
    x = x_ref[...]
    xn = (x * lax.rsqrt(jnp.mean(x * x, axis=-1, keepdims=True) + EPS) * nw_ref[...]).astype(BF16)
    width = wq_ref.shape[1]
    hd = width // heads
    q = (_dot(xn, wq_ref[...]) * (hd ** -0.5)).astype(BF16)
    outs = []
    for h in range(heads):
        kh = kv_ref[:, h * hd:(h + 1) * hd]
        vh = kv_ref[:, width + h * hd:width + (h + 1) * hd]
        s = lax.dot_general(q[:, h * hd:(h + 1) * hd], kh, NT_DIMS, preferred_element_type=F32)
        p = jnp.exp(s - jnp.max(s, axis=-1, keepdims=True))
        p = p * (1.0 / jnp.sum(p, axis=-1, keepdims=True))
        outs.append(_dot(p.astype(BF16), vh))
    o = jnp.concatenate(outs, axis=1).astype(BF16)
    y = x + _dot(o, wo_ref[...])
    o_ref[...] = y
    n_ref[...] = (y * lax.rsqrt(jnp.mean(y * y, axis=-1, keepdims=True) + EPS) * nw2_ref[...]).astype(BF16)


def _xattn(x, norm_w, wq, kv, wo, next_norm_w, l, tq=512):
    bsz, seq, d = x.shape
    mem = kv.shape[1]
    width = wq.shape[2]
    tq = _pick(seq, tq, 16)
    return pl.pallas_call(
        functools.partial(_xattn_body, heads=XATTN_HEADS),
        grid=(bsz, seq // tq),
        in_specs=[pl.BlockSpec((None, tq, d), lambda b, i: (b, i, 0)),
                  pl.BlockSpec((None, 1, d), lambda b, i: (l, 0, 0)),
                  pl.BlockSpec((None, d, width), lambda b, i: (l, 0, 0), pipeline_mode=pl.Buffered(1)),
                  pl.BlockSpec((None, mem, 2 * width), lambda b, i: (b, 0, 0)),
                  pl.BlockSpec((None, width, d), lambda b, i: (l, 0, 0), pipeline_mode=pl.Buffered(1)),
                  pl.BlockSpec((None, 1, d), lambda b, i: (l, 0, 0))],
        out_specs=[pl.BlockSpec((None, tq, d), lambda b, i: (b, i, 0)),
                   pl.BlockSpec((None, tq, d), lambda b, i: (b, i, 0))],
        out_shape=[jax.ShapeDtypeStruct((bsz, seq, d), F32),
                   jax.ShapeDtypeStruct((bsz, seq, d), BF16)],
        compiler_params=_params(("parallel", "arbitrary")),
        name="mem_xattn",
    )(x, norm_w, wq, kv, wo, next_norm_w)


def _norm_w(w):
    return w.astype(F32).reshape(w.shape[0], 1, w.shape[1])


def _ffn(x, xn, w_gate, w_up, w_down, l, next_gain=None):
    hid = _swiglu_up(xn, w_gate, w_up, l)
    return _matmul_residual(hid, w_down, l, x, 0.5, next_gain=next_gain, name="ffn_down")


def _mixer_weights(d, w_in, gla_w_lr, gla_b_lr, mlstm_gate_b):
    depth = w_in.shape[0]
    half = d // 2
    qk = half // 2
    o_lr = 2 * qk + 2 * half
    o_m = o_lr + 2 * GLA_RANK
    o_mg = o_m + 2 * qk + 2 * half
    w_gla = w_in[:, :, :o_lr].astype(BF16)
    w_ml = w_in[:, :, o_m:o_mg].astype(BF16)
    w_lr = jnp.pad(w_in[:, :, o_lr:o_m], ((0, 0), (0, 0), (0, LANE - 2 * GLA_RANK))).astype(BF16)
    mg = w_in[:, :, o_mg:o_mg + 4 * HEADS].reshape(depth, d, 4, HEADS).transpose(0, 3, 2, 1)
    w_gates_t = jnp.pad(mg, ((0, 0), (0, 0), (0, SUBLANE - 4), (0, 0))).reshape(depth, HEADS * SUBLANE, d)
    w_gates_t = w_gates_t.astype(BF16)
    bias_r = jnp.pad(mlstm_gate_b.astype(F32).transpose(0, 2, 1), ((0, 0), (0, 0), (0, SUBLANE - 4)))
    bias_r = jnp.broadcast_to(bias_r.reshape(depth, HEADS * SUBLANE, 1), (depth, HEADS * SUBLANE, LANE))
    wlr = jnp.stack([jnp.pad(gla_w_lr[:, 0], ((0, 0), (0, LANE - GLA_RANK), (0, 0))),
                     jnp.pad(gla_w_lr[:, 1], ((0, 0), (GLA_RANK, LANE - 2 * GLA_RANK), (0, 0)))],
                    axis=1).astype(F32)
    blr = gla_b_lr.astype(F32).reshape(depth, 2, 1, qk)
    return w_gla, w_ml, w_lr, w_gates_t, bias_r, wlr, blr


def _mixer(x, xn, bsz, seq, l, w_gla, w_ml, w_lr, w_gates_t, bias_r, wlr, blr, gla_norm, mlstm_norm, w_out):
    m, d = x.shape
    half = d // 2
    dv = half // HEADS
    dk = dv // 2
    qk = HEADS * dk
    proj_g = _matmul(xn, w_gla, l, BF16, name="gla_in_proj").reshape(bsz, seq, -1)
    proj_m = _matmul(xn, w_ml, l, BF16, name="mlstm_in_proj").reshape(bsz, seq, -1)
    lr, gates_t = _gate_proj(xn, w_lr, w_gates_t, l)
    lr = lr.reshape(bsz, seq, LANE)

    kw = dict(dk=dk, dv=dv, col_q=0, col_k=qk, col_v=2 * qk)
    g_b = _gla_scan(proj_g, lr, wlr, blr, l, reverse=True, **kw)
    g_out = _gla_scan(proj_g, lr, wlr, blr, l, reverse=False, col_gate=2 * qk + half, other=g_b,
                      norm_w=gla_norm, **kw)
    m_b = _mlstm_scan(proj_m, gates_t, bias_r, l, reverse=True, **kw)
    m_out = _mlstm_scan(proj_m, gates_t, bias_r, l, reverse=False, col_gate=2 * qk + half,
                        other=m_b, norm_w=mlstm_norm, hps=MLSTM_FWD_HEADS_PER_STEP, **kw)
    return _out_proj(g_out.reshape(m, half), m_out.reshape(m, half), w_out, l, x)


def kernel(x_prompt, x_sample, mem_prompt, mem_sample, ffn1_norm, ffn1_w_gate, ffn1_w_up, ffn1_w_down, mix_norm, w_in, gla_w_lr, gla_b_lr, gla_out_norm, mlstm_gate_b, mlstm_out_norm, w_out, xattn_norm, mem_norm, xattn_wq, xattn_wk, xattn_wv, xattn_wo, ffn2_norm, ffn2_w_gate, ffn2_w_up, ffn2_w_down, final_norm):
    assert x_prompt.shape[1:] == x_sample.shape[1:] and mem_prompt.shape[1:] == mem_sample.shape[1:]
    n_p, seq, d = x_prompt.shape
    n_s = x_sample.shape[0]
    bsz = n_p + n_s
    m = bsz * seq
    x = (x_prompt.reshape(n_p * seq, d), x_sample.reshape(n_s * seq, d))
    mem = jnp.concatenate([mem_prompt, mem_sample], axis=0)
    n_mem = mem.shape[1]
    mem = mem.reshape(bsz * n_mem, d)
    depth = ffn1_norm.shape[0]

    ffn1 = (ffn1_w_gate.astype(BF16), ffn1_w_up.astype(BF16), ffn1_w_down.astype(BF16))
    ffn2 = (ffn2_w_gate.astype(BF16), ffn2_w_up.astype(BF16), ffn2_w_down.astype(BF16))
    mixer_w = _mixer_weights(d, w_in, gla_w_lr, gla_b_lr, mlstm_gate_b)
    mixer_rest = (_norm_w(gla_out_norm), _norm_w(mlstm_out_norm), w_out.astype(BF16))
    w_kv = jnp.concatenate([xattn_wk, xattn_wv], axis=2).astype(BF16)
    wq = xattn_wq.astype(BF16)
    wo = xattn_wo.astype(BF16)
    ffn1_n, ffn2_n = _norm_w(ffn1_norm), _norm_w(ffn2_norm)
    mix_n, xattn_n, mem_n = _norm_w(mix_norm), _norm_w(xattn_norm), _norm_w(mem_norm)

    xn = _rmsnorm_pair(*x, ffn1_n, 0, BF16)
    for l in range(depth):
        x, xn = _ffn(x, xn, *ffn1, l, next_gain=(mix_n, l))
        x = _mixer(x, xn, bsz, seq, l, *mixer_w, *mixer_rest)
        kv = _matmul(_rmsnorm(mem, mem_n, l, BF16), w_kv, l, BF16, tm=512, name="mem_kv_proj")
        x, xn = _xattn(x.reshape(bsz, seq, d), xattn_n, wq, kv.reshape(bsz, n_mem, -1), wo, ffn2_n, l)
        x, xn = x.reshape(m, d), xn.reshape(m, d)
        if l + 1 < depth:
            x, xn = _ffn(x, xn, *ffn2, l, next_gain=(ffn1_n, l + 1))
        else:
            x = _ffn(x, xn, *ffn2, l)
    fin = _norm_w(final_norm.reshape(1, d))
    y_p = _rmsnorm(x, fin, 0, F32, 0, n_p * seq).reshape(n_p, seq, d)
    y_s = _rmsnorm(x, fin, 0, F32, n_p * seq, n_s * seq).reshape(n_s, seq, d)
    return (y_p, y_s)
```
